```python
import math
import jax, jax.numpy as jnp
from jax import lax
import numpy as np

D_MODEL = 2048
BATCH = 32
SEQ = 256
DEPTH = 2
DEC_BATCH = 4
DEC_SEQ = 4096
PAST_LEN = 512

F32 = jnp.float32
GRID_W = 64
HEAD_DIM = 128
ATT_WIDTH = D_MODEL // 2
N_HEADS = ATT_WIDTH // HEAD_DIM
N_KV_HEADS = N_HEADS // 4
KV_WIDTH = N_KV_HEADS * HEAD_DIM
HY_WIDTH = D_MODEL - ATT_WIDTH
HY_ORDER = 2
HY_SHORT = 3
HY_BANDS = 16
HY_EMB = 1 + 2 * HY_BANDS
HY_HIDDEN = 64
HY_TARGET = 1e-2
HY_FAST = 0.3
HY_SLOW = 1.5
HY_MIN_DECAY = math.log(HY_TARGET) / HY_SLOW
HY_MAX_DECAY = math.log(HY_TARGET) / HY_FAST
IN_WIDTH = ATT_WIDTH + 2 * KV_WIDTH + (HY_ORDER + 1) * HY_WIDTH
CONV_WIDTH = 31
N_GROUPS = 4
EXPERTS_PER_GROUP = 8
N_EXPERTS = N_GROUPS * EXPERTS_PER_GROUP
TOP_K = 2
D_EXPERT = D_MODEL // 2
MOE_BLOCK = 128
Q_BLOCK = 128
ROPE_THETA = 10000.0
ROPE_PAIRS = HEAD_DIM // 4
N_ATTN_LAYERS = (DEPTH + 1) // 2
N_CONV_LAYERS = DEPTH // 2
DEEPNORM_ALPHA = (2 * DEPTH) ** 0.25
DEEPNORM_BETA = (8 * DEPTH) ** -0.25
EPS = 1e-6

kernel_name = "hybrid_diffusion_ctx_prefix_step"


def layer_norm(x, g, b):
    xf = x.astype(F32)
    mu = jnp.mean(xf, axis=-1, keepdims=True)
    var = jnp.mean(jnp.square(xf - mu), axis=-1, keepdims=True)
    return ((xf - mu) * lax.rsqrt(var + EPS) * g.astype(F32) + b.astype(F32)).astype(x.dtype)


def rms_norm(x, g):
    xf = x.astype(F32)
    inv = lax.rsqrt(jnp.mean(jnp.square(xf), axis=-1, keepdims=True) + EPS)
    return (xf * inv * g.astype(F32)).astype(x.dtype)


def depthwise_conv(x, w, b):
    width = w.shape[0]
    y = lax.conv_general_dilated(
        x, w[:, None, :].astype(x.dtype), window_strides=(1,),
        padding=[(width // 2, width // 2)],
        dimension_numbers=("NWC", "WIO", "NWC"),
        feature_group_count=x.shape[-1])
    return y + b.astype(x.dtype)


def rope_2d(n_tokens):
    rows = n_tokens // GRID_W
    row = jnp.repeat(jnp.arange(rows, dtype=F32), GRID_W)
    col = jnp.tile(jnp.arange(GRID_W, dtype=F32), rows)
    inv_freq = ROPE_THETA ** (-jnp.arange(ROPE_PAIRS, dtype=F32) / ROPE_PAIRS)
    ang = jnp.concatenate([row[:, None] * inv_freq, col[:, None] * inv_freq], axis=-1)
    return jnp.cos(ang), jnp.sin(ang)


def apply_rope_2d(x, cos, sin):
    B, L, H, _ = x.shape
    xr = x.astype(F32).reshape(B, L, H, 2, 2, ROPE_PAIRS)
    x1, x2 = xr[..., 0, :], xr[..., 1, :]
    c = cos.reshape(L, 1, 2, ROPE_PAIRS)
    s = sin.reshape(L, 1, 2, ROPE_PAIRS)
    out = jnp.stack([x1 * c - x2 * s, x2 * c + x1 * s], axis=-2)
    return out.reshape(x.shape).astype(x.dtype)


def block_attention(q, k, v):
    B, Hq, L, hd = q.shape
    Hkv = k.shape[1]
    G = Hq // Hkv
    nb = L // Q_BLOCK
    qb = q.reshape(B, Hkv, G, nb, Q_BLOCK, hd).transpose(3, 0, 1, 2, 4, 5)
    scale = HEAD_DIM ** -0.5

    def one_block(qi):
        s = jnp.einsum("bkgqd,bksd->bkgqs", qi, k).astype(F32) * scale
        p = jax.nn.softmax(s, axis=-1).astype(v.dtype)
        return jnp.einsum("bkgqs,bksd->bkgqd", p, v)

    o = lax.map(one_block, qb)
    return o.transpose(1, 2, 3, 0, 4, 5).reshape(B, Hq, L, hd)


def hyena_filters(L, w1, b1, w2, b2, w3, b3, sin_freq):
    t = jnp.arange(L, dtype=F32)
    t01 = jnp.linspace(0.0, 1.0, L, dtype=F32)
    bands = jnp.linspace(1e-4, HY_BANDS - 1, HY_BANDS, dtype=F32)
    ang = (2.0 * math.pi * t / L)[:, None] * bands
    z = jnp.concatenate([t01[:, None], jnp.cos(ang), -jnp.sin(ang)], axis=-1)
    sf = sin_freq.astype(F32)
    h = jnp.sin(sf[0] * (z @ w1.astype(F32) + b1.astype(F32)))
    h = jnp.sin(sf[1] * (h @ w2.astype(F32) + b2.astype(F32)))
    h = h @ w3.astype(F32) + b3.astype(F32)
    deltas = jnp.abs(jnp.linspace(HY_MIN_DECAY, HY_MAX_DECAY, HY_WIDTH, dtype=F32))
    window = jnp.exp(-t01[:, None] * deltas)
    h = h.reshape(L, HY_ORDER, 2, HY_WIDTH) * window[:, None, None, :]
    h_fwd, h_bwd = h[:, :, 0], h[:, :, 1]
    kern = jnp.concatenate(
        [h_fwd, jnp.zeros((1, HY_ORDER, HY_WIDTH), F32), h_bwd[1:][::-1]], axis=0)
    return kern / jnp.sum(jnp.abs(kern), axis=0, keepdims=True)


def long_conv(z, kern_f, skip):
    L = z.shape[1]
    zf = z.astype(F32)
    y = jnp.fft.irfft(jnp.fft.rfft(zf, n=2 * L, axis=1) * kern_f, n=2 * L, axis=1)[:, :L]
    return (y + zf * skip.astype(F32)).astype(z.dtype)


def hyena(u, short_w, short_b, w1, b1, w2, b2, w3, b3, sin_freq, skip):
    u = depthwise_conv(u, short_w, short_b)
    v, x1, x2 = jnp.split(u, 3, axis=-1)
    L = u.shape[1]
    kern_f = jnp.fft.rfft(hyena_filters(L, w1, b1, w2, b2, w3, b3, sin_freq), axis=0)
    z = x1 * long_conv(v, kern_f[:, 0], skip[0])
    return x2 * long_conv(z, kern_f[:, 1], skip[1])


def mixer_attn_hyena(h, rope, ctx_k, ctx_v, w_in, q_g, k_g, short_w, short_b,
                     fw1, fb1, fw2, fb2, fw3, fb3, sin_freq, skip, w_out):
    B, L, _ = h.shape
    zp = h @ w_in
    q, k, v, u = jnp.split(zp, [ATT_WIDTH, ATT_WIDTH + KV_WIDTH, ATT_WIDTH + 2 * KV_WIDTH], axis=-1)
    q = rms_norm(q.reshape(B, L, N_HEADS, HEAD_DIM), q_g)
    k = rms_norm(k.reshape(B, L, N_KV_HEADS, HEAD_DIM), k_g)
    v = v.reshape(B, L, N_KV_HEADS, HEAD_DIM)
    if rope is not None:
        q = apply_rope_2d(q, rope[0], rope[1])
        k = apply_rope_2d(k, rope[0], rope[1])
    q, k, v = (t.transpose(0, 2, 1, 3) for t in (q, k, v))
    k_all, v_all = k, v
    if ctx_k is not None:
        k_all = jnp.concatenate([ctx_k.astype(k.dtype), k], axis=2)
        v_all = jnp.concatenate([ctx_v.astype(v.dtype), v], axis=2)
    o = block_attention(q, k_all, v_all).transpose(0, 2, 1, 3).reshape(B, L, ATT_WIDTH)
    y = hyena(u, short_w, short_b, fw1, fb1, fw2, fb2, fw3, fb3, sin_freq, skip)
    return jnp.concatenate([o, y], axis=-1) @ w_out, k, v


def conformer_conv(h, pw1, dw_w, dw_b, ln_g, ln_b, pw2):
    a = h @ pw1
    a = a[..., :D_MODEL] * jax.nn.sigmoid(a[..., D_MODEL:])
    a = depthwise_conv(a, dw_w, dw_b)
    a = jax.nn.silu(layer_norm(a, ln_g, ln_b))
    return a @ pw2


def hier_moe(x, w_grp, b_grp, w_exp, b_exp, w_gate, w_up, w_down):
    T = x.shape[0]
    lg = (x @ w_grp).astype(F32) + b_grp.astype(F32)
    pg = jax.nn.softmax(lg, axis=-1)
    g = jnp.argmax(lg, axis=-1)
    gate_g = jnp.take_along_axis(pg, g[:, None], axis=-1)
    le = ((x @ w_exp).astype(F32) + b_exp.astype(F32)).reshape(T, N_GROUPS, EXPERTS_PER_GROUP)
    le = jnp.take_along_axis(le, g[:, None, None], axis=1)[:, 0]
    top_p, top_i = lax.top_k(jax.nn.softmax(le, axis=-1), TOP_K)
    gates = gate_g * top_p / jnp.sum(top_p, axis=-1, keepdims=True)
    experts = g[:, None] * EXPERTS_PER_GROUP + top_i
    A = T * TOP_K
    e_flat = experts.reshape(A)
    tok_flat = jnp.arange(A, dtype=jnp.int32) // TOP_K
    order = jnp.argsort(e_flat)
    e_sorted = e_flat[order]
    counts = jnp.bincount(e_flat, length=N_EXPERTS)
    starts = jnp.cumsum(counts) - counts
    padded = (counts + MOE_BLOCK - 1) // MOE_BLOCK * MOE_BLOCK
    pends = jnp.cumsum(padded)
    pstarts = pends - padded
    dest = pstarts[e_sorted] + (jnp.arange(A) - starts[e_sorted])
    NB = -(-A // MOE_BLOCK) + N_EXPERTS
    slot_tok = jnp.full((NB * MOE_BLOCK,), T, jnp.int32).at[dest].set(tok_flat[order])
    slot_gate = jnp.zeros((NB * MOE_BLOCK,), x.dtype).at[dest].set(gates.reshape(A)[order].astype(x.dtype))
    blk_exp = jnp.minimum(jnp.searchsorted(pends, jnp.arange(NB) * MOE_BLOCK, side="right"), N_EXPERTS - 1)
    x_pad = jnp.concatenate([x, jnp.zeros((1, x.shape[1]), x.dtype)], axis=0)
    xs = x_pad[slot_tok].reshape(NB, MOE_BLOCK, x.shape[1])

    def expert_block(args):
        xb, e = args
        hb = jax.nn.silu(xb @ w_gate[e]) * (xb @ w_up[e])
        return hb @ w_down[e]

    ys = lax.map(expert_block, (xs, blk_exp)).reshape(NB * MOE_BLOCK, x.shape[1])
    out = jnp.zeros((T + 1, x.shape[1]), x.dtype).at[slot_tok].add(ys * slot_gate[:, None])
    return out[:T]


def trunk(x, cond, rope, ctx_k, ctx_v, P):
    new_k, new_v = [], []
    for layer in range(DEPTH):
        i = layer // 2
        mod = (jax.nn.silu(cond) @ P["w_ada"][layer] + P["b_ada"][layer])[:, None, :]
        sh1, sc1, g1, sh2, sc2, g2 = jnp.split(mod, 6, axis=-1)
        h = x * (1 + sc1) + sh1
        if layer % 2 == 0:
            y, k, v = mixer_attn_hyena(
                h, rope,
                None if ctx_k is None else ctx_k[:, i],
                None if ctx_v is None else ctx_v[:, i],
                P["w_in_ab"][i], P["q_norm_g"][i], P["k_norm_g"][i],
                P["hy_short_w"][i], P["hy_short_b"][i],
                P["hy_f_w1"][i], P["hy_f_b1"][i], P["hy_f_w2"][i], P["hy_f_b2"][i],
                P["hy_f_w3"][i], P["hy_f_b3"][i], P["hy_sin_freq"][i], P["hy_skip"][i],
                P["w_out_ab"][i])
            new_k.append(k)
            new_v.append(v)
        else:
            y = conformer_conv(h, P["cv_pw1"][i], P["cv_dw_w"][i], P["cv_dw_b"][i],
                               P["cv_ln_g"][i], P["cv_ln_b"][i], P["cv_pw2"][i])
        x = layer_norm(DEEPNORM_ALPHA * x + g1 * y, P["ln_g"][layer, 0], P["ln_b"][layer, 0])
        h = x * (1 + sc2) + sh2
        y = hier_moe(h.reshape(-1, D_MODEL), P["moe_w_grp"][layer], P["moe_b_grp"][layer],
                     P["moe_w_exp"][layer], P["moe_b_exp"][layer], P["moe_w_gate"][layer],
                     P["moe_w_up"][layer], P["moe_w_down"][layer]).reshape(x.shape)
        x = layer_norm(DEEPNORM_ALPHA * x + g2 * y, P["ln_g"][layer, 1], P["ln_b"][layer, 1])
    return x, new_k, new_v


def setup_inputs(seed: int = 0) -> dict:
    key = jax.random.key(seed)
    ks = list(jax.random.split(key, 48))

    def nrm(shape, scale):
        return jax.random.normal(ks.pop(), shape, F32) * scale

    D = D_MODEL
    NA, NC = N_ATTN_LAYERS, N_CONV_LAYERS
    inv_d = D ** -0.5
    return {
        "x_prompt": nrm((BATCH, SEQ, D), 1.0),
        "x_sample": nrm((DEC_BATCH, DEC_SEQ, D), 1.0),
        "cache_k": nrm((DEC_BATCH, NA, N_KV_HEADS, PAST_LEN, HEAD_DIM), 1.0),
        "cache_v": nrm((DEC_BATCH, NA, N_KV_HEADS, PAST_LEN, HEAD_DIM), 1.0),
        "c": nrm((DEC_BATCH, D), 1.0),
        "c_ctx": nrm((D,), 1.0),
        "w_ada": nrm((DEPTH, D, 6 * D), 0.5 * inv_d),
        "b_ada": nrm((DEPTH, 6 * D), 0.02),
        "ln_g": 1.0 + nrm((DEPTH, 2, D), 0.02),
        "ln_b": nrm((DEPTH, 2, D), 0.02),
        "w_in_ab": nrm((NA, D, IN_WIDTH), inv_d),
        "q_norm_g": 1.0 + nrm((NA, HEAD_DIM), 0.02),
        "k_norm_g": 1.0 + nrm((NA, HEAD_DIM), 0.02),
        "hy_short_w": nrm((NA, HY_SHORT, 3 * HY_WIDTH), HY_SHORT ** -0.5),
        "hy_short_b": nrm((NA, 3 * HY_WIDTH), 0.02),
        "hy_f_w1": nrm((NA, HY_EMB, HY_HIDDEN), HY_EMB ** -0.5),
        "hy_f_b1": nrm((NA, HY_HIDDEN), 0.02),
        "hy_f_w2": nrm((NA, HY_HIDDEN, HY_HIDDEN), HY_HIDDEN ** -0.5),
        "hy_f_b2": nrm((NA, HY_HIDDEN), 0.02),
        "hy_f_w3": nrm((NA, HY_HIDDEN, HY_ORDER * 2 * HY_WIDTH), HY_HIDDEN ** -0.5),
        "hy_f_b3": nrm((NA, HY_ORDER * 2 * HY_WIDTH), 0.02),
        "hy_sin_freq": 1.0 + nrm((NA, 2, HY_HIDDEN), 0.1),
        "hy_skip": nrm((NA, HY_ORDER, HY_WIDTH), 1.0),
        "w_out_ab": nrm((NA, D, D), inv_d * DEEPNORM_BETA),
        "cv_pw1": nrm((NC, D, 2 * D), inv_d),
        "cv_dw_w": nrm((NC, CONV_WIDTH, D), CONV_WIDTH ** -0.5),
        "cv_dw_b": nrm((NC, D), 0.02),
        "cv_ln_g": 1.0 + nrm((NC, D), 0.02),
        "cv_ln_b": nrm((NC, D), 0.02),
        "cv_pw2": nrm((NC, D, D), inv_d * DEEPNORM_BETA),
        "moe_w_grp": nrm((DEPTH, D, N_GROUPS), inv_d),
        "moe_b_grp": nrm((DEPTH, N_GROUPS), 0.01),
        "moe_w_exp": nrm((DEPTH, D, N_EXPERTS), inv_d),
        "moe_b_exp": nrm((DEPTH, N_EXPERTS), 0.01),
        "moe_w_gate": nrm((DEPTH, N_EXPERTS, D, D_EXPERT), inv_d),
        "moe_w_up": nrm((DEPTH, N_EXPERTS, D, D_EXPERT), inv_d),
        "moe_w_down": nrm((DEPTH, N_EXPERTS, D_EXPERT, D), D_EXPERT ** -0.5 * DEEPNORM_BETA),
    }


def reference(x_prompt, x_sample, cache_k, cache_v, c, c_ctx, w_ada, b_ada, ln_g, ln_b,
              w_in_ab, q_norm_g, k_norm_g, hy_short_w, hy_short_b, hy_f_w1, hy_f_b1,
              hy_f_w2, hy_f_b2, hy_f_w3, hy_f_b3, hy_sin_freq, hy_skip, w_out_ab,
              cv_pw1, cv_dw_w, cv_dw_b, cv_ln_g, cv_ln_b, cv_pw2,
              moe_w_grp, moe_b_grp, moe_w_exp, moe_b_exp, moe_w_gate, moe_w_up, moe_w_down):
    P = dict(w_ada=w_ada, b_ada=b_ada, ln_g=ln_g, ln_b=ln_b, w_in_ab=w_in_ab,
             q_norm_g=q_norm_g, k_norm_g=k_norm_g, hy_short_w=hy_short_w,
             hy_short_b=hy_short_b, hy_f_w1=hy_f_w1, hy_f_b1=hy_f_b1, hy_f_w2=hy_f_w2,
             hy_f_b2=hy_f_b2, hy_f_w3=hy_f_w3, hy_f_b3=hy_f_b3, hy_sin_freq=hy_sin_freq,
             hy_skip=hy_skip, w_out_ab=w_out_ab, cv_pw1=cv_pw1, cv_dw_w=cv_dw_w,
             cv_dw_b=cv_dw_b, cv_ln_g=cv_ln_g, cv_ln_b=cv_ln_b, cv_pw2=cv_pw2,
             moe_w_grp=moe_w_grp, moe_b_grp=moe_b_grp, moe_w_exp=moe_w_exp,
             moe_b_exp=moe_b_exp, moe_w_gate=moe_w_gate, moe_w_up=moe_w_up,
             moe_w_down=moe_w_down)
    y_prompt, ks, vs = trunk(x_prompt, c_ctx[None, :], None, None, None, P)
    new_k = jnp.stack(ks, axis=1)
    new_v = jnp.stack(vs, axis=1)
    rope = rope_2d(x_sample.shape[1])
    y_sample, _, _ = trunk(x_sample, c, rope, cache_k, cache_v, P)
    return (y_prompt, y_sample, new_k, new_v)
```

```python
import functools
import math

import jax
import jax.numpy as jnp
import numpy as np
from jax import lax
from jax.experimental import pallas as pl
from jax.experimental.pallas import tpu as pltpu

F32 = jnp.float32
BF16 = jnp.bfloat16
HIGHEST = lax.Precision.HIGHEST

GRID_W = 64
HEAD_DIM = 128
ROPE_THETA = 10000.0
ROPE_PAIRS = HEAD_DIM // 4
HY_BANDS = 16
HY_TARGET = 1e-2
HY_FAST = 0.3
HY_SLOW = 1.5
TOP_K = 2
EPS = 1e-6

LANES = 128
SUBLANES = 8
VMEM_LIMIT = 56 * 1024 * 1024
MOE_BLK = 256
FFT_SPLIT = 64
FFT_DENSE_MAX = 1024


def _cp(*sem):
    return pltpu.CompilerParams(dimension_semantics=sem, vmem_limit_bytes=VMEM_LIMIT)


def _tile(n, target, mult=SUBLANES):
    if n <= target:
        return n
    for t in range(target, 0, -1):
        if n % t == 0 and t % mult == 0:
            return t
    return n


def _silu(x):
    return x * jax.nn.sigmoid(x)


def _ln_rows(x, g, b):
    mu = jnp.mean(x, axis=-1, keepdims=True)
    xc = x - mu
    var = jnp.mean(xc * xc, axis=-1, keepdims=True)
    return xc * lax.rsqrt(var + EPS) * g + b


def _ada_kernel(c_ref, w_ref, b_ref, o_ref):
    s = _silu(c_ref[...])
    o_ref[...] = jnp.dot(s, w_ref[...], precision=HIGHEST, preferred_element_type=F32) + b_ref[...]


def _ada(cond, w_ada, b_ada):
    depth, d, n = w_ada.shape
    g = cond.shape[0]
    tn = _tile(n, 1024, LANES)
    return pl.pallas_call(
        _ada_kernel,
        grid=(depth, n // tn),
        in_specs=[
            pl.BlockSpec((g, d), lambda l, j: (0, 0)),
            pl.BlockSpec((None, d, tn), lambda l, j: (l, 0, j)),
            pl.BlockSpec((None, 1, tn), lambda l, j: (l, 0, j)),
        ],
        out_specs=pl.BlockSpec((None, g, tn), lambda l, j: (l, 0, j)),
        out_shape=jax.ShapeDtypeStruct((depth, g, n), F32),
        compiler_params=_cp("arbitrary", "arbitrary"),
        name="ada_mod",
    )(cond, w_ada, b_ada.reshape(depth, 1, n))


def _mm_mod_kernel(x_ref, sc_ref, sh_ref, w_ref, o_ref, h_scr):
    @pl.when(pl.program_id(1) == 0)
    def _():
        h_scr[...] = (x_ref[...] * (1.0 + sc_ref[...]) + sh_ref[...]).astype(BF16)

    o_ref[...] = jnp.dot(h_scr[...], w_ref[...], preferred_element_type=F32)


def _mm_glu_kernel(x_ref, sc_ref, sh_ref, wa_ref, wg_ref, o_ref, h_scr):
    @pl.when(pl.program_id(1) == 0)
    def _():
        h_scr[...] = (x_ref[...] * (1.0 + sc_ref[...]) + sh_ref[...]).astype(BF16)

    a = jnp.dot(h_scr[...], wa_ref[...], preferred_element_type=F32)
    g = jnp.dot(h_scr[...], wg_ref[...], preferred_element_type=F32)
    o_ref[...] = a * jax.nn.sigmoid(g)


class _Groups:
    def __init__(self, tp, ls, nmod):
        self.tp, self.ls, self.nmod = tp, ls, nmod

    def gid(self, i, tm):
        row = i * tm
        return jnp.where(row < self.tp, 0, 1 + (row - self.tp) // self.ls)

    def mod_spec(self, which, tm, d, nargs=2):
        if nargs == 2:
            return pl.BlockSpec((None, 1, d), lambda i, j: (self.gid(i, tm) * self.nmod + which, 0, 0))
        return pl.BlockSpec((None, 1, d), lambda i: (self.gid(i, tm) * self.nmod + which, 0, 0))


def _mm_mod(x, modl, grp, which_sc, which_sh, w_bf, glu=False):
    t, d = x.shape
    n = w_bf.shape[1] // (2 if glu else 1)
    tm = _tile(math.gcd(grp.tp, grp.ls), 1024)
    tn = _tile(n, 512, LANES)
    specs = [
        pl.BlockSpec((tm, d), lambda i, j: (i, 0)),
        grp.mod_spec(which_sc, tm, d),
        grp.mod_spec(which_sh, tm, d),
        pl.BlockSpec((d, tn), lambda i, j: (0, j)),
    ]
    args = [x, modl, modl, w_bf]
    if glu:
        noff = n // tn
        specs.append(pl.BlockSpec((d, tn), lambda i, j: (0, j + noff)))
        args.append(w_bf)
    return pl.pallas_call(
        _mm_glu_kernel if glu else _mm_mod_kernel,
        grid=(t // tm, n // tn),
        in_specs=specs,
        out_specs=pl.BlockSpec((tm, tn), lambda i, j: (i, j)),
        out_shape=jax.ShapeDtypeStruct((t, n), F32),
        scratch_shapes=[pltpu.VMEM((tm, d), BF16)],
        compiler_params=_cp("arbitrary", "arbitrary"),
        name="mm_glu" if glu else "mm_mod",
    )(*args)


def _qkprep_kernel(q_ref, k_ref, c_ref, s_ref, qg_ref, kg_ref, qo_ref, ko_ref, kb_ref, *, nq, nk, scale):
    c = c_ref[...]
    s = s_ref[...]
    lane = lax.broadcasted_iota(jnp.int32, c.shape, 1)
    first = (lane % (2 * ROPE_PAIRS)) < ROPE_PAIRS

    def prep(xh, g):
        ms = jnp.mean(xh * xh, axis=-1, keepdims=True)
        xn = xh * lax.rsqrt(ms + EPS) * g
        rot = jnp.where(first, pltpu.roll(xn, HEAD_DIM - ROPE_PAIRS, 1), pltpu.roll(xn, ROPE_PAIRS, 1))
        return xn * c + rot * s

    for h in range(nq):
        sl = slice(h * HEAD_DIM, (h + 1) * HEAD_DIM)
        qo_ref[:, sl] = (prep(q_ref[:, sl], qg_ref[...]) * scale).astype(BF16)
    for h in range(nk):
        sl = slice(h * HEAD_DIM, (h + 1) * HEAD_DIM)
        kh = prep(k_ref[:, sl], kg_ref[...])
        ko_ref[:, sl] = kh
        kb_ref[:, sl] = kh.astype(BF16)


def _qkprep(zp, cos_t, sin_t, q_g, k_g, att_w, kv_w):
    t = zp.shape[0]
    tm = _tile(t, 512)
    nq, nk = att_w // HEAD_DIM, kv_w // HEAD_DIM
    assert att_w % kv_w == 0
    kern = functools.partial(_qkprep_kernel, nq=nq, nk=nk, scale=HEAD_DIM ** -0.5)
    return pl.pallas_call(
        kern,
        grid=(t // tm,),
        in_specs=[
            pl.BlockSpec((tm, att_w), lambda i: (i, 0)),
            pl.BlockSpec((tm, kv_w), lambda i: (i, att_w // kv_w)),
            pl.BlockSpec((tm, HEAD_DIM), lambda i: (i, 0)),
            pl.BlockSpec((tm, HEAD_DIM), lambda i: (i, 0)),
            pl.BlockSpec((1, HEAD_DIM), lambda i: (0, 0)),
            pl.BlockSpec((1, HEAD_DIM), lambda i: (0, 0)),
        ],
        out_specs=[
            pl.BlockSpec((tm, att_w), lambda i: (i, 0)),
            pl.BlockSpec((tm, kv_w), lambda i: (i, 0)),
            pl.BlockSpec((tm, kv_w), lambda i: (i, 0)),
        ],
        out_shape=[
            jax.ShapeDtypeStruct((t, att_w), BF16),
            jax.ShapeDtypeStruct((t, kv_w), F32),
            jax.ShapeDtypeStruct((t, kv_w), BF16),
        ],
        compiler_params=_cp("arbitrary"),
        name="qk_prep",
    )(zp, zp, cos_t, sin_t, q_g.reshape(1, HEAD_DIM), k_g.reshape(1, HEAD_DIM))


def _attn_kernel(*refs, has_cache, g, tq):
    if has_cache:
        q_ref, k_ref, v_ref, kc_ref, vc_ref, o_ref = refs
    else:
        q_ref, k_ref, v_ref, o_ref = refs
    qs = jnp.concatenate([q_ref[:, h * HEAD_DIM:(h + 1) * HEAD_DIM] for h in range(g)], axis=0)
    nt = (((1,), (1,)), ((), ()))
    s = lax.dot_general(qs, k_ref[...], nt, preferred_element_type=F32)
    m = jnp.max(s, axis=-1, keepdims=True)
    if has_cache:
        sc = lax.dot_general(qs, kc_ref[...].astype(BF16), nt, preferred_element_type=F32)
        m = jnp.maximum(m, jnp.max(sc, axis=-1, keepdims=True))
    p = jnp.exp(s - m)
    l = jnp.sum(p, axis=-1, keepdims=True)
    o = jnp.dot(p.astype(BF16), v_ref[...].astype(BF16), preferred_element_type=F32)
    if has_cache:
        pc = jnp.exp(sc - m)
        l = l + jnp.sum(pc, axis=-1, keepdims=True)
        o = o + jnp.dot(pc.astype(BF16), vc_ref[...].astype(BF16), preferred_element_type=F32)
    o = o / l
    for h in range(g):
        o_ref[:, h * HEAD_DIM:(h + 1) * HEAD_DIM] = o[h * tq:(h + 1) * tq].astype(BF16)


def _attention(q_bf, k_bf, zp, row0, nb, l, att_w, kv_w, cache_k=None, cache_v=None, layer=0):
    nkv = kv_w // HEAD_DIM
    g = att_w // kv_w
    tq = _tile(l, max(SUBLANES, 512 // g))
    rb = row0 // l
    assert row0 % l == 0
    vcol = (att_w + kv_w) // HEAD_DIM
    has_cache = cache_k is not None
    specs = [
        pl.BlockSpec((tq, g * HEAD_DIM), lambda b, h, i: ((row0 + b * l) // tq + i, h)),
        pl.BlockSpec((l, HEAD_DIM), lambda b, h, i: (rb + b, h)),
        pl.BlockSpec((l, HEAD_DIM), lambda b, h, i: (rb + b, vcol + h)),
    ]
    args = [q_bf, k_bf, zp]
    if has_cache:
        past = cache_k.shape[3]
        cspec = pl.BlockSpec((None, None, None, past, HEAD_DIM), lambda b, h, i: (b, layer, h, 0, 0))
        specs += [cspec, cspec]
        args += [cache_k, cache_v]
    kern = functools.partial(_attn_kernel, has_cache=has_cache, g=g, tq=tq)
    return pl.pallas_call(
        kern,
        grid=(nb, nkv, l // tq),
        in_specs=specs,
        out_specs=pl.BlockSpec((tq, g * HEAD_DIM), lambda b, h, i: (b * (l // tq) + i, h)),
        out_shape=jax.ShapeDtypeStruct((nb * l, att_w), BF16),
        compiler_params=_cp("arbitrary", "arbitrary", "arbitrary"),
        name="attn_cache" if has_cache else "attn",
    )(*args)


def _dwconv_kernel(x_ref, w_ref, b_ref, o_ref, pad, *, l, width, rc, off):
    ct = x_ref.shape[1]
    zeros = jnp.zeros((off, ct), F32)
    pad[0:off, :] = zeros
    pad[off + l:off + l + off, :] = zeros
    pad[off:off + l, :] = x_ref[...]
    w = w_ref[...]
    bias = b_ref[...]
    shift = off - width // 2

    def chunk(ci, carry):
        base = pl.multiple_of(ci * rc, rc)
        win = pad[pl.ds(base, rc + 2 * off), :]
        acc = jnp.zeros((rc, ct), F32) + bias
        for k in range(width):
            acc = acc + w[k:k + 1, :] * win[k + shift:k + shift + rc, :]
        o_ref[pl.ds(base, rc), :] = acc
        return carry

    lax.fori_loop(0, l // rc, chunk, 0)


def _dwconv(x, col0, ncols, w, b, row0, nb, l):
    width = w.shape[0]
    off = 16
    assert width // 2 <= off
    ct = _tile(ncols, 256, LANES)
    rc = _tile(l, 64)
    wp = jnp.zeros((32, ncols), F32).at[:width].set(w)
    assert col0 % ct == 0 and row0 % l == 0
    kern = functools.partial(_dwconv_kernel, l=l, width=width, rc=rc, off=off)
    return pl.pallas_call(
        kern,
        grid=(nb, ncols // ct),
        in_specs=[
            pl.BlockSpec((l, ct), lambda b_, j: (row0 // l + b_, col0 // ct + j)),
            pl.BlockSpec((32, ct), lambda b_, j: (0, j)),
            pl.BlockSpec((1, ct), lambda b_, j: (0, j)),
        ],
        out_specs=pl.BlockSpec((l, ct), lambda b_, j: (b_, j)),
        out_shape=jax.ShapeDtypeStruct((nb * l, ncols), F32),
        scratch_shapes=[pltpu.VMEM((l + 2 * off, ct), F32)],
        compiler_params=_cp("arbitrary", "arbitrary"),
        name="dwconv%d" % width,
    )(x, wp, b.reshape(1, ncols))


def _fft_plan(l):
    n = 2 * l
    if n <= FFT_DENSE_MAX or n % FFT_SPLIT != 0:
        return n, 1
    return n // FFT_SPLIT, FFT_SPLIT


def _stack3_lhs(m):
    hi = m.astype(np.float32).astype(jnp.bfloat16)
    lo = (m - np.asarray(hi.astype(np.float32), np.float64)).astype(np.float32).astype(jnp.bfloat16)
    return jnp.asarray(np.concatenate([hi, lo, hi], axis=1))


def _split3(x):
    hi = x.astype(BF16)
    lo = (x - hi.astype(F32)).astype(BF16)
    return jnp.concatenate([hi, hi, lo], axis=0)


def _mm3(lhs3, x):
    return jnp.dot(lhs3, _split3(x), preferred_element_type=F32)


def _fft_consts(l):
    n1, n2 = _fft_plan(l)
    n = n1 * n2
    n1h = n1 // 2
    k1 = np.arange(n1, dtype=np.float64)[:, None]
    ang = 2.0 * np.pi * k1 * np.arange(n1h, dtype=np.float64)[None, :] / n1
    fr, fi = np.cos(ang), -np.sin(ang)
    a_fwd = np.block([[fr, -fi], [fi, fr]])
    ang = 2.0 * np.pi * k1 * np.arange(n1, dtype=np.float64)[None, :] / n1
    a_real = np.concatenate([np.cos(ang), -np.sin(ang)], axis=0)
    angi = 2.0 * np.pi * np.arange(n1h, dtype=np.float64)[:, None] * np.arange(n1, dtype=np.float64)[None, :] / n1
    cr, ci = np.cos(angi) / n, np.sin(angi) / n
    a_inv = np.block([[cr, -ci], [ci, cr]])
    k2 = np.arange(n2, dtype=np.float64)
    angb = 2.0 * np.pi * k2[:, None] * k2[None, :] / n2
    gr, gi = np.cos(angb), -np.sin(angb)
    b_fwd = np.block([[gr, -gi], [gi, gr]])
    b_inv = np.block([[gr, gi], [-gi, gr]])
    angt = 2.0 * np.pi * (np.arange(n1, dtype=np.float64)[:, None] * k2[None, :]).reshape(n, 1) / n
    twr = np.broadcast_to(np.cos(angt), (n, LANES)).astype(np.float32)
    twi = np.broadcast_to(-np.sin(angt), (n, LANES)).astype(np.float32)
    return dict(n1=n1, n2=n2, a_fwd=_stack3_lhs(a_fwd), a_real=_stack3_lhs(a_real), a_inv=_stack3_lhs(a_inv),
                b_fwd=_stack3_lhs(b_fwd), b_inv=_stack3_lhs(b_inv), twr=jnp.asarray(twr), twi=jnp.asarray(twi))


def _rows(ref, start, size, stride):
    if stride == 1:
        return ref[pl.ds(start, size), :]
    return ref[pl.ds(start, size, stride=stride), :]


def _stage_b_fwd(pr, pi, twr_ref, twi_ref, bfwd_ref, k1, n1, n2):
    xr = _rows(pr, k1, n2, n1)
    xi = _rows(pi, k1, n2, n1)
    r0 = pl.multiple_of(k1 * n2, n2)
    cr = twr_ref[pl.ds(r0, n2), :]
    ci = twi_ref[pl.ds(r0, n2), :]
    ar = xr * cr - xi * ci
    ai = xr * ci + xi * cr
    x = _mm3(bfwd_ref[...], jnp.concatenate([ar, ai], axis=0))
    return x[:n2], x[n2:], cr, ci


def _lconv_kernel(a_ref, g_ref, kf_ref, skip_ref, afwd_ref, ainv_ref, bfwd_ref, binv_ref, twr_ref, twi_ref,
                  o_ref, pr, pi, qr, qi, *, l, n1, n2):
    n1h = n1 // 2

    def stage_a(i2, carry):
        slab = jnp.concatenate([_rows(a_ref, i2, n1h, n2), _rows(a_ref, l + i2, n1h, n2)], axis=0)
        out = _mm3(afwd_ref[...], slab)
        r0 = pl.multiple_of(i2 * n1, n1)
        pr[pl.ds(r0, n1), :] = out[:n1]
        pi[pl.ds(r0, n1), :] = out[n1:]
        return carry

    lax.fori_loop(0, n2, stage_a, 0)

    if n2 == 1:
        kr, ki = kf_ref[0], kf_ref[1]
        xr, xi = pr[...], pi[...]
        qr[...] = xr * kr - xi * ki
        qi[...] = xr * ki + xi * kr
    else:
        def stage_b(k1, carry):
            xr, xi, cr, ci = _stage_b_fwd(pr, pi, twr_ref, twi_ref, bfwd_ref, k1, n1, n2)
            r0 = pl.multiple_of(k1 * n2, n2)
            kr = kf_ref[0, pl.ds(r0, n2), :]
            ki = kf_ref[1, pl.ds(r0, n2), :]
            yr = xr * kr - xi * ki
            yi = xr * ki + xi * kr
            bv = _mm3(binv_ref[...], jnp.concatenate([yr, yi], axis=0))
            br, bi = bv[:n2], bv[n2:]
            qr[pl.ds(r0, n2), :] = br * cr + bi * ci
            qi[pl.ds(r0, n2), :] = bi * cr - br * ci
            return carry

        lax.fori_loop(0, n1, stage_b, 0)

    skip = skip_ref[...]

    def stage_c(i2, carry):
        slab = jnp.concatenate([_rows(qr, i2, n1, n2), _rows(qi, i2, n1, n2)], axis=0)
        y = _mm3(ainv_ref[...], slab)
        for half in range(2):
            start = half * l + i2
            a = _rows(a_ref, start, n1h, n2)
            g = _rows(g_ref, start, n1h, n2)
            val = g * (y[half * n1h:(half + 1) * n1h] + skip * a)
            if n2 == 1:
                o_ref[pl.ds(start, n1h), :] = val
            else:
                o_ref[pl.ds(start, n1h, stride=n2), :] = val
        return carry

    lax.fori_loop(0, n2, stage_c, 0)


def _const_spec(shape):
    zeros = (0,) * len(shape)
    return pl.BlockSpec(shape, lambda a, b: zeros, pipeline_mode=pl.Buffered(1))


def _lconv(a, a_col0, a_row0, gate, g_col0, g_row0, kf, skip, fc, nb, l, c):
    n1, n2 = fc["n1"], fc["n2"]
    n = n1 * n2
    ct = LANES
    assert nb % 2 == 0 and a_row0 % (2 * l) == 0 and g_row0 % (2 * l) == 0
    single = pl.Buffered(1)
    kern = functools.partial(_lconv_kernel, l=l, n1=n1, n2=n2)
    return pl.pallas_call(
        kern,
        grid=(c // ct, nb // 2),
        in_specs=[
            pl.BlockSpec((2 * l, ct), lambda j, p: (a_row0 // (2 * l) + p, a_col0 // ct + j), pipeline_mode=single),
            pl.BlockSpec((2 * l, ct), lambda j, p: (g_row0 // (2 * l) + p, g_col0 // ct + j), pipeline_mode=single),
            pl.BlockSpec((2, n, ct), lambda j, p: (0, 0, j), pipeline_mode=single),
            pl.BlockSpec((1, ct), lambda j, p: (0, j)),
            _const_spec(fc["a_fwd"].shape), _const_spec(fc["a_inv"].shape),
            _const_spec(fc["b_fwd"].shape), _const_spec(fc["b_inv"].shape),
            _const_spec((n, LANES)), _const_spec((n, LANES)),
        ],
        out_specs=pl.BlockSpec((2 * l, ct), lambda j, p: (p, j), pipeline_mode=single),
        out_shape=jax.ShapeDtypeStruct((nb * l, c), F32),
        scratch_shapes=[pltpu.VMEM((n, ct), F32) for _ in range(4)],
        compiler_params=_cp("arbitrary", "arbitrary"),
        name="hyena_lconv",
    )(a, gate, kf, skip.reshape(1, c), fc["a_fwd"], fc["a_inv"], fc["b_fwd"], fc["b_inv"], fc["twr"], fc["twi"])


def _hyfilt_kernel(zz_ref, tt_ref, w1_ref, b1_ref, w2_ref, b2_ref, sf_ref, w3f_ref, w3b_ref, b3f_ref, b3b_ref,
                   dl_ref, areal_ref, bfwd_ref, twr_ref, twi_ref, o_ref, h_scr, kern_scr, pr, pi, *, l, n1, n2):
    rc = _tile(l, 256)
    nch = l // rc

    @pl.when((pl.program_id(0) == 0) & (pl.program_id(1) == 0))
    def _():
        def mlp(ci, carry):
            r0 = pl.multiple_of(ci * rc, rc)
            h = jnp.dot(zz_ref[pl.ds(r0, rc), :], w1_ref[...], precision=HIGHEST, preferred_element_type=F32)
            h = jnp.sin(sf_ref[0:1, :] * (h + b1_ref[...]))
            h = jnp.dot(h, w2_ref[...], precision=HIGHEST, preferred_element_type=F32) + b2_ref[...]
            h_scr[pl.ds(r0, rc), :] = jnp.sin(sf_ref[1:2, :] * h)
            return carry

        lax.fori_loop(0, 2 * nch, mlp, 0)

    def filt(w3_ref, b3_ref, first_chunk):
        def body(ci, acc):
            r0 = pl.multiple_of((first_chunk + ci) * rc, rc)
            hk = jnp.dot(h_scr[pl.ds(r0, rc), :], w3_ref[...], precision=HIGHEST, preferred_element_type=F32)
            win = jnp.exp(-tt_ref[pl.ds(r0, rc), :] * dl_ref[...])
            row = r0 + lax.broadcasted_iota(jnp.int32, (rc, 1), 0)
            k = jnp.where(row == l, 0.0, (hk + b3_ref[...]) * win)
            kern_scr[pl.ds(r0, rc), :] = k
            return acc + jnp.sum(jnp.abs(k), axis=0, keepdims=True)
        return body

    zero = jnp.zeros((1, kern_scr.shape[1]), F32)
    total = lax.fori_loop(0, nch, filt(w3f_ref, b3f_ref, 0), zero)
    total = lax.fori_loop(0, nch, filt(w3b_ref, b3b_ref, nch), total)
    inv = 1.0 / total

    def stage_a(i2, carry):
        out = _mm3(areal_ref[...], _rows(kern_scr, i2, n1, n2) * inv)
        r0 = pl.multiple_of(i2 * n1, n1)
        pr[pl.ds(r0, n1), :] = out[:n1]
        pi[pl.ds(r0, n1), :] = out[n1:]
        return carry

    lax.fori_loop(0, n2, stage_a, 0)

    if n2 == 1:
        o_ref[0] = pr[...]
        o_ref[1] = pi[...]
    else:
        def stage_b(k1, carry):
            xr, xi, _, _ = _stage_b_fwd(pr, pi, twr_ref, twi_ref, bfwd_ref, k1, n1, n2)
            r0 = pl.multiple_of(k1 * n2, n2)
            o_ref[0, pl.ds(r0, n2), :] = xr
            o_ref[1, pl.ds(r0, n2), :] = xi
            return carry

        lax.fori_loop(0, n1, stage_b, 0)


def _hyena_filters(l, fc, w1, b1, w2, b2, w3, b3, sin_freq, c):
    n1, n2 = fc["n1"], fc["n2"]
    n = 2 * l
    order = w3.shape[1] // (2 * c)
    hid = w1.shape[1]
    ct = LANES
    pos = np.concatenate([np.arange(l), [0], np.arange(l - 1, 0, -1)]).astype(np.float32)
    t01 = np.linspace(0.0, 1.0, l, dtype=np.float32)[pos.astype(np.int64)]
    bands = np.linspace(1e-4, HY_BANDS - 1, HY_BANDS, dtype=np.float32)
    ang = (np.float32(2.0 * math.pi) * pos / np.float32(l))[:, None] * bands
    emb = 1 + 2 * HY_BANDS
    feat = np.zeros((n, hid), np.float32)
    feat[:, :emb] = np.concatenate([t01[:, None], np.cos(ang), -np.sin(ang)], axis=-1)
    w1p = jnp.zeros((hid, hid), F32).at[:emb].set(w1)
    tt = jnp.asarray(np.broadcast_to(t01[:, None], (n, LANES)).copy())
    dmin = math.log(HY_TARGET) / HY_SLOW
    dmax = math.log(HY_TARGET) / HY_FAST
    deltas = jnp.abs(jnp.linspace(dmin, dmax, c, dtype=F32)).reshape(1, c)
    cb = c // ct
    kern = functools.partial(_hyfilt_kernel, l=l, n1=n1, n2=n2)
    return pl.pallas_call(
        kern,
        grid=(order, cb),
        in_specs=[
            _const_spec((n, hid)), _const_spec((n, LANES)),
            _const_spec((hid, hid)), _const_spec((1, hid)), _const_spec((hid, hid)), _const_spec((1, hid)),
            _const_spec((2, hid)),
            pl.BlockSpec((hid, ct), lambda o, j: (0, (2 * o) * cb + j)),
            pl.BlockSpec((hid, ct), lambda o, j: (0, (2 * o + 1) * cb + j)),
            pl.BlockSpec((1, ct), lambda o, j: (0, (2 * o) * cb + j)),
            pl.BlockSpec((1, ct), lambda o, j: (0, (2 * o + 1) * cb + j)),
            pl.BlockSpec((1, ct), lambda o, j: (0, j)),
            _const_spec(fc["a_real"].shape), _const_spec(fc["b_fwd"].shape),
            _const_spec((n, LANES)), _const_spec((n, LANES)),
        ],
        out_specs=pl.BlockSpec((None, 2, n, ct), lambda o, j: (o, 0, 0, j), pipeline_mode=pl.Buffered(1)),
        out_shape=jax.ShapeDtypeStruct((order, 2, n, c), F32),
        scratch_shapes=[pltpu.VMEM((n, hid), F32), pltpu.VMEM((n, ct), F32), pltpu.VMEM((n, ct), F32),
                        pltpu.VMEM((n, ct), F32)],
        compiler_params=_cp("arbitrary", "arbitrary"),
        name="hyena_filters",
    )(jnp.asarray(feat), tt, w1p, b1.reshape(1, hid), w2, b2.reshape(1, hid), sin_freq, w3, w3,
      b3.reshape(1, -1), b3.reshape(1, -1), deltas, fc["a_real"], fc["b_fwd"], fc["twr"], fc["twi"])


def _res_ln_epilogue(acc, x_ref, g_ref, lng_ref, lnb_ref, sc_ref, sh_ref, wr_ref, br_ref, xo_ref, h_ref, lg_ref, alpha):
    xn = _ln_rows(alpha * x_ref[...] + g_ref[...] * acc, lng_ref[...], lnb_ref[...])
    xo_ref[...] = xn
    h = xn * (1.0 + sc_ref[...]) + sh_ref[...]
    h_ref[...] = h
    lg_ref[...] = jnp.dot(h, wr_ref[...], precision=HIGHEST, preferred_element_type=F32) + br_ref[...]


def _out_ln_kernel(o_ref, y_ref, w1_ref, w2_ref, *rest, alpha):
    acc = jnp.dot(o_ref[...], w1_ref[...], preferred_element_type=F32)
    acc = acc + jnp.dot(y_ref[...].astype(BF16), w2_ref[...], preferred_element_type=F32)
    _res_ln_epilogue(acc, *rest, alpha)


def _pw2_ln_kernel(a_ref, cg_ref, cb_ref, w_ref, *rest, alpha):
    a = _silu(_ln_rows(a_ref[...], cg_ref[...], cb_ref[...])).astype(BF16)
    acc = jnp.dot(a, w_ref[...], preferred_element_type=F32)
    _res_ln_epilogue(acc, *rest, alpha)


def _res_ln_call(kern, name, lead_args, lead_specs, x, modl, grp, which_g, which_sc, which_sh, ln_g, ln_b, wr, br, tm):
    t, d = x.shape
    nr = wr.shape[1]
    row = lambda i: (i, 0)
    const = lambda i: (0, 0)
    specs = lead_specs + [
        pl.BlockSpec((tm, d), row),
        grp.mod_spec(which_g, tm, d, 1),
        pl.BlockSpec((1, d), const), pl.BlockSpec((1, d), const),
        grp.mod_spec(which_sc, tm, d, 1), grp.mod_spec(which_sh, tm, d, 1),
        pl.BlockSpec((d, nr), const, pipeline_mode=pl.Buffered(1)), pl.BlockSpec((1, nr), const),
    ]
    return pl.pallas_call(
        kern,
        grid=(t // tm,),
        in_specs=specs,
        out_specs=[pl.BlockSpec((tm, d), row), pl.BlockSpec((tm, d), row), pl.BlockSpec((tm, nr), row)],
        out_shape=[jax.ShapeDtypeStruct((t, d), F32), jax.ShapeDtypeStruct((t, d), F32),
                   jax.ShapeDtypeStruct((t, nr), F32)],
        compiler_params=_cp("arbitrary"),
        name=name,
    )(*lead_args, x, modl, ln_g.reshape(1, d), ln_b.reshape(1, d), modl, modl, wr, br)


def _out_ln(o_bf, y, w_out_bf, x, modl, grp, ln_g, ln_b, wr, br, alpha):
    t, d = x.shape
    ka = o_bf.shape[1]
    kb = y.shape[1]
    assert ka == kb
    tm = _tile(math.gcd(grp.tp, grp.ls), 512)
    single = pl.Buffered(1)
    lead_specs = [
        pl.BlockSpec((tm, ka), lambda i: (i, 0)),
        pl.BlockSpec((tm, kb), lambda i: (i, 0)),
        pl.BlockSpec((ka, d), lambda i: (0, 0), pipeline_mode=single),
        pl.BlockSpec((kb, d), lambda i: (1, 0), pipeline_mode=single),
    ]
    kern = functools.partial(_out_ln_kernel, alpha=alpha)
    return _res_ln_call(kern, "out_proj_ln", [o_bf, y, w_out_bf, w_out_bf], lead_specs, x, modl, grp, 2, 4, 3,
                        ln_g, ln_b, wr, br, tm)


def _pw2_ln(a, cv_g, cv_b, w_bf, x, modl, grp, ln_g, ln_b, wr, br, alpha):
    t, d = x.shape
    tm = _tile(math.gcd(grp.tp, grp.ls), 512)
    lead_specs = [
        pl.BlockSpec((tm, d), lambda i: (i, 0)),
        pl.BlockSpec((1, d), lambda i: (0, 0)), pl.BlockSpec((1, d), lambda i: (0, 0)),
        pl.BlockSpec((d, d), lambda i: (0, 0), pipeline_mode=pl.Buffered(1)),
    ]
    kern = functools.partial(_pw2_ln_kernel, alpha=alpha)
    return _res_ln_call(kern, "conv_proj_ln", [a, cv_g.reshape(1, d), cv_b.reshape(1, d), w_bf], lead_specs, x,
                        modl, grp, 2, 4, 3, ln_g, ln_b, wr, br, tm)


def _route_kernel(lg_ref, tri_ref, idx_ref, gate_ref, cnt_ref, carry, *, ng, epg):
    @pl.when(pl.program_id(0) == 0)
    def _():
        carry[...] = jnp.zeros_like(carry)

    lg = lg_ref[...]
    lane = lax.broadcasted_iota(jnp.int32, lg.shape, 1)
    neg = -jnp.inf
    big = lg.shape[1]
    gmask = lane < ng
    lgm = jnp.where(gmask, lg, neg)
    mg = jnp.max(lgm, axis=-1, keepdims=True)
    g = jnp.min(jnp.where(lgm == mg, lane, big), axis=-1, keepdims=True)
    gate_g = 1.0 / jnp.sum(jnp.exp(lgm - mg), axis=-1, keepdims=True)
    el = lane - ng
    lo = g * epg
    emask = (el >= lo) & (el < lo + epg)
    le = jnp.where(emask, lg, neg)
    m1 = jnp.max(le, axis=-1, keepdims=True)
    i1 = jnp.min(jnp.where(le == m1, lane, big), axis=-1, keepdims=True)
    le2 = jnp.where(lane == i1, neg, le)
    m2 = jnp.max(le2, axis=-1, keepdims=True)
    i2 = jnp.min(jnp.where(le2 == m2, lane, big), axis=-1, keepdims=True)
    r = jnp.exp(m2 - m1)
    den = 1.0 + r
    g1 = gate_g / den
    g2 = gate_g * r / den
    e1 = i1 - ng
    e2 = i2 - ng
    oh1 = lane == e1
    oh2 = lane == e2
    oh1f = jnp.where(oh1, 1.0, 0.0)
    oh2f = jnp.where(oh2, 1.0, 0.0)
    p1 = jnp.dot(tri_ref[...], oh1f.astype(BF16), preferred_element_type=F32)
    p2 = jnp.dot(tri_ref[...], oh2f.astype(BF16), preferred_element_type=F32)
    c = carry[0:1, :]
    tot1 = jnp.sum(oh1f, axis=0, keepdims=True)
    tot2 = jnp.sum(oh2f, axis=0, keepdims=True)
    r1 = jnp.sum(jnp.where(oh1, p1 + c, 0.0), axis=-1, keepdims=True)
    r2 = jnp.sum(jnp.where(oh2, p2 + c + tot1, 0.0), axis=-1, keepdims=True)
    newc = c + tot1 + tot2
    carry[...] = jnp.broadcast_to(newc, carry.shape)
    cnt_ref[...] = jnp.broadcast_to(newc, cnt_ref.shape)
    zi = jnp.zeros_like(lane)
    idx_ref[...] = jnp.where(lane == 0, e1, jnp.where(lane == 1, e2, jnp.where(
        lane == 2, r1.astype(jnp.int32), jnp.where(lane == 3, r2.astype(jnp.int32), zi))))
    gate_ref[...] = jnp.where(lane == 0, g1, jnp.where(lane == 1, g2, 0.0))


def _route(lg, ng, epg):
    t, nr = lg.shape
    tm = _tile(t, 512)
    tri = jnp.asarray(np.tril(np.ones((tm, tm), np.float32), -1)).astype(BF16)
    kern = functools.partial(_route_kernel, ng=ng, epg=epg)
    return pl.pallas_call(
        kern,
        grid=(t // tm,),
        in_specs=[pl.BlockSpec((tm, nr), lambda i: (i, 0)), pl.BlockSpec((tm, tm), lambda i: (0, 0))],
        out_specs=[pl.BlockSpec((tm, nr), lambda i: (i, 0)), pl.BlockSpec((tm, nr), lambda i: (i, 0)),
                   pl.BlockSpec((SUBLANES, nr), lambda i: (0, 0))],
        out_shape=[jax.ShapeDtypeStruct((t, nr), jnp.int32), jax.ShapeDtypeStruct((t, nr), F32),
                   jax.ShapeDtypeStruct((SUBLANES, nr), F32)],
        scratch_shapes=[pltpu.VMEM((SUBLANES, nr), F32)],
        compiler_params=_cp("arbitrary"),
        name="moe_route",
    )(lg, tri)


def _dispatch_kernel(dest_ref, h_ref, xs_in_ref, xs_ref, sem, *, tm):
    del xs_in_ref
    base = pl.program_id(0) * tm

    def issue(r, carry):
        t = base + r
        for k in range(TOP_K):
            d = dest_ref[TOP_K * t + k]
            pltpu.make_async_copy(h_ref.at[pl.ds(t, 1)], xs_ref.at[pl.ds(d, 1)], sem).start()
        return carry

    lax.fori_loop(0, tm, issue, 0)

    def drain(r, carry):
        for k in range(TOP_K):
            pltpu.make_async_copy(h_ref.at[pl.ds(0, 1)], xs_ref.at[pl.ds(0, 1)], sem).wait()
        return carry

    lax.fori_loop(0, tm, drain, 0)


def _dispatch(dest_flat, h, nslots):
    t, d = h.shape
    tm = _tile(t, 512)
    kern = functools.partial(_dispatch_kernel, tm=tm)
    return pl.pallas_call(
        kern,
        grid_spec=pltpu.PrefetchScalarGridSpec(
            num_scalar_prefetch=1,
            grid=(t // tm,),
            in_specs=[pl.BlockSpec(memory_space=pl.ANY), pl.BlockSpec(memory_space=pl.ANY)],
            out_specs=pl.BlockSpec(memory_space=pl.ANY),
            scratch_shapes=[pltpu.SemaphoreType.DMA(())],
        ),
        out_shape=jax.ShapeDtypeStruct((nslots, d), F32),
        input_output_aliases={2: 0},
        compiler_params=_cp("arbitrary"),
        name="moe_dispatch",
    )(dest_flat, h, jnp.zeros((nslots, d), F32))


def _ffn_kernel(be_ref, nu_ref, x_ref, wg_ref, wu_ref, wd_ref, o_ref):
    b = pl.program_id(0)

    @pl.when(b < nu_ref[0])
    def _():
        x = x_ref[...].astype(BF16)
        gate = jnp.dot(x, wg_ref[...], preferred_element_type=F32)
        up = jnp.dot(x, wu_ref[...], preferred_element_type=F32)
        hid = (_silu(gate) * up).astype(BF16)
        o_ref[...] = jnp.dot(hid, wd_ref[...], preferred_element_type=F32)

    @pl.when(b >= nu_ref[0])
    def _():
        o_ref[...] = jnp.zeros_like(o_ref)


def _ffn(blk_exp, nused, xs, wg_bf, wu_bf, wd_bf):
    ns, d = xs.shape
    de = wg_bf.shape[2]
    nblk = ns // MOE_BLK
    return pl.pallas_call(
        _ffn_kernel,
        grid_spec=pltpu.PrefetchScalarGridSpec(
            num_scalar_prefetch=2,
            grid=(nblk,),
            in_specs=[
                pl.BlockSpec((MOE_BLK, d), lambda b, be, nu: (b, 0)),
                pl.BlockSpec((None, d, de), lambda b, be, nu: (be[b], 0, 0)),
                pl.BlockSpec((None, d, de), lambda b, be, nu: (be[b], 0, 0)),
                pl.BlockSpec((None, de, d), lambda b, be, nu: (be[b], 0, 0)),
            ],
            out_specs=pl.BlockSpec((MOE_BLK, d), lambda b, be, nu: (b, 0)),
        ),
        out_shape=jax.ShapeDtypeStruct((ns, d), F32),
        compiler_params=_cp("arbitrary"),
        name="moe_ffn",
    )(blk_exp, nused, xs, wg_bf, wu_bf, wd_bf)


def _combine_kernel(dest_ref, ys_ref, x_ref, gt_ref, g_ref, lng_ref, lnb_ref, o_ref, buf, sem, *, tm, alpha):
    base = pl.program_id(0) * tm

    def issue(r, carry):
        t = base + r
        for k in range(TOP_K):
            d = dest_ref[TOP_K * t + k]
            pltpu.make_async_copy(ys_ref.at[pl.ds(d, 1)], buf.at[k, pl.ds(r, 1)], sem).start()
        return carry

    lax.fori_loop(0, tm, issue, 0)

    def drain(r, carry):
        for k in range(TOP_K):
            pltpu.make_async_copy(ys_ref.at[pl.ds(0, 1)], buf.at[k, pl.ds(0, 1)], sem).wait()
        return carry

    lax.fori_loop(0, tm, drain, 0)
    gt = gt_ref[...]
    y = gt[:, 0:1] * buf[0] + gt[:, 1:2] * buf[1]
    o_ref[...] = _ln_rows(alpha * x_ref[...] + g_ref[...] * y, lng_ref[...], lnb_ref[...])


def _combine(dest_flat, ys, x, gates, modl, grp, which_g, ln_g, ln_b, alpha):
    t, d = x.shape
    nr = gates.shape[1]
    tm = _tile(math.gcd(grp.tp, grp.ls), 256)
    kern = functools.partial(_combine_kernel, tm=tm, alpha=alpha)
    gid = grp.gid
    nmod = grp.nmod
    return pl.pallas_call(
        kern,
        grid_spec=pltpu.PrefetchScalarGridSpec(
            num_scalar_prefetch=1,
            grid=(t // tm,),
            in_specs=[
                pl.BlockSpec(memory_space=pl.ANY),
                pl.BlockSpec((tm, d), lambda i, ds_: (i, 0)),
                pl.BlockSpec((tm, nr), lambda i, ds_: (i, 0)),
                pl.BlockSpec((None, 1, d), lambda i, ds_: (gid(i, tm) * nmod + which_g, 0, 0)),
                pl.BlockSpec((1, d), lambda i, ds_: (0, 0)),
                pl.BlockSpec((1, d), lambda i, ds_: (0, 0)),
            ],
            out_specs=pl.BlockSpec((tm, d), lambda i, ds_: (i, 0)),
            scratch_shapes=[pltpu.VMEM((TOP_K, tm, d), F32), pltpu.SemaphoreType.DMA(())],
        ),
        out_shape=jax.ShapeDtypeStruct((t, d), F32),
        compiler_params=_cp("arbitrary"),
        name="moe_combine_ln",
    )(dest_flat, ys, x, gates, modl, ln_g.reshape(1, d), ln_b.reshape(1, d))


def _moe(h, lg, x, modl, grp, ln_g, ln_b, wg_bf, wu_bf, wd_bf, ng, alpha):
    t, d = h.shape
    ne = wg_bf.shape[0]
    idx, gates, cnt = _route(lg, ng, ne // ng)
    experts = idx[:, 0:TOP_K]
    ranks = idx[:, TOP_K:2 * TOP_K]
    counts = cnt[0, :ne].astype(jnp.int32)
    padded = (counts + MOE_BLK - 1) // MOE_BLK * MOE_BLK
    pends = jnp.cumsum(padded)
    pstarts = pends - padded
    dest = (jnp.take(pstarts, experts) + ranks).reshape(-1).astype(jnp.int32)
    nblk = -(-(t * TOP_K) // MOE_BLK) + ne
    blk_exp = jnp.minimum(jnp.searchsorted(pends, jnp.arange(nblk) * MOE_BLK, side="right"), ne - 1).astype(jnp.int32)
    nused = (pends[-1:] // MOE_BLK).astype(jnp.int32)
    xs = _dispatch(dest, h, nblk * MOE_BLK)
    ys = _ffn(blk_exp, nused, xs, wg_bf, wu_bf, wd_bf)
    return _combine(dest, ys, x, gates, modl, grp, 5, ln_g, ln_b, alpha)


def _rope_tables(tp, nb, l):
    rows = l // GRID_W
    row = np.repeat(np.arange(rows, dtype=np.float32), GRID_W)
    col = np.tile(np.arange(GRID_W, dtype=np.float32), rows)
    inv_freq = (np.float32(ROPE_THETA) ** (-np.arange(ROPE_PAIRS, dtype=np.float32) / np.float32(ROPE_PAIRS)))
    ar = row[:, None] * inv_freq
    ac = col[:, None] * inv_freq
    ang = np.concatenate([ar, ar, ac, ac], axis=-1).astype(np.float64)
    cos = np.cos(ang)
    sin = np.sin(ang)
    sign = np.tile(np.concatenate([-np.ones(ROPE_PAIRS), np.ones(ROPE_PAIRS)]), 2)
    cos_all = np.concatenate([np.ones((tp, HEAD_DIM)), np.tile(cos, (nb, 1))], axis=0)
    sin_all = np.concatenate([np.zeros((tp, HEAD_DIM)), np.tile(sin * sign, (nb, 1))], axis=0)
    return jnp.asarray(cos_all.astype(np.float32)), jnp.asarray(sin_all.astype(np.float32))


def kernel(x_prompt, x_sample, cache_k, cache_v, c, c_ctx, w_ada, b_ada, ln_g, ln_b, w_in_ab, q_norm_g, k_norm_g, hy_short_w, hy_short_b, hy_f_w1, hy_f_b1, hy_f_w2, hy_f_b2, hy_f_w3, hy_f_b3, hy_sin_freq, hy_skip, w_out_ab, cv_pw1, cv_dw_w, cv_dw_b, cv_ln_g, cv_ln_b, cv_pw2, moe_w_grp, moe_b_grp, moe_w_exp, moe_b_exp, moe_w_gate, moe_w_up, moe_w_down):
    bp, lp, d = x_prompt.shape
    bs, ls, _ = x_sample.shape
    depth = w_ada.shape[0]
    tp, ts = bp * lp, bs * ls
    t = tp + ts
    kv_w = cache_k.shape[2] * HEAD_DIM
    hy_w = hy_skip.shape[2]
    att_w = d - hy_w
    ng = moe_w_grp.shape[2]
    ne = moe_w_exp.shape[2]
    alpha = (2 * depth) ** 0.25
    nmod = 6
    ngrp = 1 + bs
    gpad = -(-ngrp // SUBLANES) * SUBLANES
    grp = _Groups(tp, ls, nmod)

    x = jnp.concatenate([x_prompt.reshape(tp, d), x_sample.reshape(ts, d)], axis=0)
    cond = jnp.zeros((gpad, d), F32).at[0].set(c_ctx).at[1:ngrp].set(c)
    mod = _ada(cond, w_ada, b_ada)
    nrt = LANES
    new_k, new_v = None, None
    for layer in range(depth):
        i = layer // 2
        modl = mod[layer].reshape(gpad * nmod, 1, d)
        wr = jnp.zeros((d, nrt), F32).at[:, :ng].set(moe_w_grp[layer]).at[:, ng:ng + ne].set(moe_w_exp[layer])
        br = jnp.zeros((1, nrt), F32).at[0, :ng].set(moe_b_grp[layer]).at[0, ng:ng + ne].set(moe_b_exp[layer])
        if layer % 2 == 0:
            zp = _mm_mod(x, modl, grp, 1, 0, w_in_ab[i].astype(BF16))
            cos_t, sin_t = _rope_tables(tp, bs, ls)
            q_bf, k_f, k_bf = _qkprep(zp, cos_t, sin_t, q_norm_g[i], k_norm_g[i], att_w, kv_w)
            o_p = _attention(q_bf, k_bf, zp, 0, bp, lp, att_w, kv_w)
            o_s = _attention(q_bf, k_bf, zp, tp, bs, ls, att_w, kv_w, cache_k, cache_v, i)
            o_bf = jnp.concatenate([o_p, o_s], axis=0)
            ucol = att_w + 2 * kv_w
            ys = []
            for row0, nb, l in ((0, bp, lp), (tp, bs, ls)):
                fc = _fft_consts(l)
                kf = _hyena_filters(l, fc, hy_f_w1[i], hy_f_b1[i], hy_f_w2[i], hy_f_b2[i], hy_f_w3[i], hy_f_b3[i],
                                    hy_sin_freq[i], hy_w)
                usc = _dwconv(zp, ucol, 3 * hy_w, hy_short_w[i], hy_short_b[i], row0, nb, l)
                z1 = _lconv(usc, 0, 0, usc, hy_w, 0, kf[0], hy_skip[i, 0], fc, nb, l, hy_w)
                ys.append(_lconv(z1, 0, 0, usc, 2 * hy_w, 0, kf[1], hy_skip[i, 1], fc, nb, l, hy_w))
            y_hy = jnp.concatenate(ys, axis=0)
            x, h, lg = _out_ln(o_bf, y_hy, w_out_ab[i].astype(BF16), x, modl, grp, ln_g[layer, 0], ln_b[layer, 0],
                               wr, br, alpha)
            nkv = kv_w // HEAD_DIM
            kk = k_f[:tp].reshape(bp, lp, nkv, HEAD_DIM).transpose(0, 2, 1, 3)
            vv = zp[:tp, att_w + kv_w:att_w + 2 * kv_w].reshape(bp, lp, nkv, HEAD_DIM).transpose(0, 2, 1, 3)
            new_k = kk if new_k is None else jnp.concatenate([new_k, kk], axis=1)
            new_v = vv if new_v is None else jnp.concatenate([new_v, vv], axis=1)
        else:
            a = _mm_mod(x, modl, grp, 1, 0, cv_pw1[i].astype(BF16), glu=True)
            a = jnp.concatenate([_dwconv(a, 0, d, cv_dw_w[i], cv_dw_b[i], 0, bp, lp),
                                 _dwconv(a, 0, d, cv_dw_w[i], cv_dw_b[i], tp, bs, ls)], axis=0)
            x, h, lg = _pw2_ln(a, cv_ln_g[i], cv_ln_b[i], cv_pw2[i].astype(BF16), x, modl, grp,
                               ln_g[layer, 0], ln_b[layer, 0], wr, br, alpha)
        x = _moe(h, lg, x, modl, grp, ln_g[layer, 1], ln_b[layer, 1], moe_w_gate[layer].astype(BF16),
                 moe_w_up[layer].astype(BF16), moe_w_down[layer].astype(BF16), ng, alpha)
    nkv = kv_w // HEAD_DIM
    n_attn = (depth + 1) // 2
    new_k = new_k.reshape(bp, n_attn, nkv, lp, HEAD_DIM)
    new_v = new_v.reshape(bp, n_attn, nkv, lp, HEAD_DIM)
    return (x[:tp].reshape(bp, lp, d), x[tp:].reshape(bs, ls, d), new_k, new_v)
```

```python
import functools
import math

import jax
import jax.numpy as jnp
import numpy as np
from jax import lax
from jax.experimental import pallas as pl
from jax.experimental.pallas import tpu as pltpu

F32 = jnp.float32
BF16 = jnp.bfloat16
HIGHEST = lax.Precision.HIGHEST

GRID_W = 64
HEAD_DIM = 128
ROPE_THETA = 10000.0
ROPE_PAIRS = HEAD_DIM // 4
HY_BANDS = 16
HY_TARGET = 1e-2
HY_FAST = 0.3
HY_SLOW = 1.5
TOP_K = 2
EPS = 1e-6

LANES = 128
SUBLANES = 8
VMEM_LIMIT = 56 * 1024 * 1024
MOE_BLK = 256
FFT_SPLIT = 64
FFT_DENSE_MAX = 1024
FFT_UNROLL = 4


def _cp(*sem):
    return pltpu.CompilerParams(dimension_semantics=sem, vmem_limit_bytes=VMEM_LIMIT)


def _tile(n, target, mult=SUBLANES):
    if n <= target:
        return n
    for t in range(target, 0, -1):
        if n % t == 0 and t % mult == 0:
            return t
    return n


def _silu(x):
    return x * jax.nn.sigmoid(x)


def _ln_rows(x, g, b):
    mu = jnp.mean(x, axis=-1, keepdims=True)
    xc = x - mu
    var = jnp.mean(xc * xc, axis=-1, keepdims=True)
    return xc * lax.rsqrt(var + EPS) * g + b


def _ada_kernel(c_ref, w_ref, b_ref, o_ref):
    s = _silu(c_ref[...])
    o_ref[...] = jnp.dot(s, w_ref[...], precision=HIGHEST, preferred_element_type=F32) + b_ref[...]


def _ada(cond, w_ada, b_ada):
    depth, d, n = w_ada.shape
    g = cond.shape[0]
    tn = _tile(n, 1024, LANES)
    return pl.pallas_call(
        _ada_kernel,
        grid=(depth, n // tn),
        in_specs=[
            pl.BlockSpec((g, d), lambda l, j: (0, 0)),
            pl.BlockSpec((None, d, tn), lambda l, j: (l, 0, j)),
            pl.BlockSpec((None, 1, tn), lambda l, j: (l, 0, j)),
        ],
        out_specs=pl.BlockSpec((None, g, tn), lambda l, j: (l, 0, j)),
        out_shape=jax.ShapeDtypeStruct((depth, g, n), F32),
        compiler_params=_cp("arbitrary", "arbitrary"),
        name="ada_mod",
    )(cond, w_ada, b_ada.reshape(depth, 1, n))


def _mm_mod_kernel(x_ref, sc_ref, sh_ref, w_ref, o_ref, h_scr):
    @pl.when(pl.program_id(1) == 0)
    def _():
        h_scr[...] = (x_ref[...] * (1.0 + sc_ref[...]) + sh_ref[...]).astype(BF16)

    o_ref[...] = jnp.dot(h_scr[...], w_ref[...], preferred_element_type=F32)


def _mm_glu_kernel(x_ref, sc_ref, sh_ref, wa_ref, wg_ref, o_ref, h_scr):
    @pl.when(pl.program_id(1) == 0)
    def _():
        h_scr[...] = (x_ref[...] * (1.0 + sc_ref[...]) + sh_ref[...]).astype(BF16)

    a = jnp.dot(h_scr[...], wa_ref[...], preferred_element_type=F32)
    g = jnp.dot(h_scr[...], wg_ref[...], preferred_element_type=F32)
    o_ref[...] = a * jax.nn.sigmoid(g)


class _Groups:
    def __init__(self, tp, ls, nmod):
        self.tp, self.ls, self.nmod = tp, ls, nmod

    def gid(self, i, tm):
        row = i * tm
        return jnp.where(row < self.tp, 0, 1 + (row - self.tp) // self.ls)

    def mod_spec(self, which, tm, d, nargs=2):
        if nargs == 2:
            return pl.BlockSpec((None, 1, d), lambda i, j: (self.gid(i, tm) * self.nmod + which, 0, 0))
        return pl.BlockSpec((None, 1, d), lambda i: (self.gid(i, tm) * self.nmod + which, 0, 0))


def _mm_mod(x, modl, grp, which_sc, which_sh, w_bf, glu=False):
    t, d = x.shape
    n = w_bf.shape[1] // (2 if glu else 1)
    tm = _tile(math.gcd(grp.tp, grp.ls), 1024)
    tn = _tile(n, 512, LANES)
    specs = [
        pl.BlockSpec((tm, d), lambda i, j: (i, 0)),
        grp.mod_spec(which_sc, tm, d),
        grp.mod_spec(which_sh, tm, d),
        pl.BlockSpec((d, tn), lambda i, j: (0, j)),
    ]
    args = [x, modl, modl, w_bf]
    if glu:
        noff = n // tn
        specs.append(pl.BlockSpec((d, tn), lambda i, j: (0, j + noff)))
        args.append(w_bf)
    return pl.pallas_call(
        _mm_glu_kernel if glu else _mm_mod_kernel,
        grid=(t // tm, n // tn),
        in_specs=specs,
        out_specs=pl.BlockSpec((tm, tn), lambda i, j: (i, j)),
        out_shape=jax.ShapeDtypeStruct((t, n), F32),
        scratch_shapes=[pltpu.VMEM((tm, d), BF16)],
        compiler_params=_cp("arbitrary", "arbitrary"),
        name="mm_glu" if glu else "mm_mod",
    )(*args)


def _qkprep_kernel(q_ref, k_ref, c_ref, s_ref, qg_ref, kg_ref, qo_ref, ko_ref, kb_ref, *, nq, nk, scale):
    c = c_ref[...]
    s = s_ref[...]
    lane = lax.broadcasted_iota(jnp.int32, c.shape, 1)
    first = (lane % (2 * ROPE_PAIRS)) < ROPE_PAIRS

    def prep(xh, g):
        ms = jnp.mean(xh * xh, axis=-1, keepdims=True)
        xn = xh * lax.rsqrt(ms + EPS) * g
        rot = jnp.where(first, pltpu.roll(xn, HEAD_DIM - ROPE_PAIRS, 1), pltpu.roll(xn, ROPE_PAIRS, 1))
        return xn * c + rot * s

    for h in range(nq):
        sl = slice(h * HEAD_DIM, (h + 1) * HEAD_DIM)
        qo_ref[:, sl] = (prep(q_ref[:, sl], qg_ref[...]) * scale).astype(BF16)
    for h in range(nk):
        sl = slice(h * HEAD_DIM, (h + 1) * HEAD_DIM)
        kh = prep(k_ref[:, sl], kg_ref[...])
        ko_ref[:, sl] = kh
        kb_ref[:, sl] = kh.astype(BF16)


def _qkprep(zp, cos_t, sin_t, q_g, k_g, att_w, kv_w):
    t = zp.shape[0]
    tm = _tile(t, 512)
    nq, nk = att_w // HEAD_DIM, kv_w // HEAD_DIM
    assert att_w % kv_w == 0
    kern = functools.partial(_qkprep_kernel, nq=nq, nk=nk, scale=HEAD_DIM ** -0.5)
    return pl.pallas_call(
        kern,
        grid=(t // tm,),
        in_specs=[
            pl.BlockSpec((tm, att_w), lambda i: (i, 0)),
            pl.BlockSpec((tm, kv_w), lambda i: (i, att_w // kv_w)),
            pl.BlockSpec((tm, HEAD_DIM), lambda i: (i, 0)),
            pl.BlockSpec((tm, HEAD_DIM), lambda i: (i, 0)),
            pl.BlockSpec((1, HEAD_DIM), lambda i: (0, 0)),
            pl.BlockSpec((1, HEAD_DIM), lambda i: (0, 0)),
        ],
        out_specs=[
            pl.BlockSpec((tm, att_w), lambda i: (i, 0)),
            pl.BlockSpec((tm, kv_w), lambda i: (i, 0)),
            pl.BlockSpec((tm, kv_w), lambda i: (i, 0)),
        ],
        out_shape=[
            jax.ShapeDtypeStruct((t, att_w), BF16),
            jax.ShapeDtypeStruct((t, kv_w), F32),
            jax.ShapeDtypeStruct((t, kv_w), BF16),
        ],
        compiler_params=_cp("arbitrary"),
        name="qk_prep",
    )(zp, zp, cos_t, sin_t, q_g.reshape(1, HEAD_DIM), k_g.reshape(1, HEAD_DIM))


def _attn_kernel(*refs, has_cache, g, tq):
    if has_cache:
        q_ref, k_ref, v_ref, kc_ref, vc_ref, o_ref = refs
    else:
        q_ref, k_ref, v_ref, o_ref = refs
    qs = jnp.concatenate([q_ref[:, h * HEAD_DIM:(h + 1) * HEAD_DIM] for h in range(g)], axis=0)
    nt = (((1,), (1,)), ((), ()))
    s = lax.dot_general(qs, k_ref[...], nt, preferred_element_type=F32)
    m = jnp.max(s, axis=-1, keepdims=True)
    if has_cache:
        sc = lax.dot_general(qs, kc_ref[...].astype(BF16), nt, preferred_element_type=F32)
        m = jnp.maximum(m, jnp.max(sc, axis=-1, keepdims=True))
    p = jnp.exp(s - m)
    l = jnp.sum(p, axis=-1, keepdims=True)
    o = jnp.dot(p.astype(BF16), v_ref[...].astype(BF16), preferred_element_type=F32)
    if has_cache:
        pc = jnp.exp(sc - m)
        l = l + jnp.sum(pc, axis=-1, keepdims=True)
        o = o + jnp.dot(pc.astype(BF16), vc_ref[...].astype(BF16), preferred_element_type=F32)
    o = o / l
    for h in range(g):
        o_ref[:, h * HEAD_DIM:(h + 1) * HEAD_DIM] = o[h * tq:(h + 1) * tq].astype(BF16)


def _attention(q_bf, k_bf, zp, row0, nb, l, att_w, kv_w, cache_k=None, cache_v=None, layer=0):
    nkv = kv_w // HEAD_DIM
    g = att_w // kv_w
    tq = _tile(l, max(SUBLANES, 512 // g))
    rb = row0 // l
    assert row0 % l == 0
    vcol = (att_w + kv_w) // HEAD_DIM
    has_cache = cache_k is not None
    specs = [
        pl.BlockSpec((tq, g * HEAD_DIM), lambda b, h, i: ((row0 + b * l) // tq + i, h)),
        pl.BlockSpec((l, HEAD_DIM), lambda b, h, i: (rb + b, h)),
        pl.BlockSpec((l, HEAD_DIM), lambda b, h, i: (rb + b, vcol + h)),
    ]
    args = [q_bf, k_bf, zp]
    if has_cache:
        past = cache_k.shape[3]
        cspec = pl.BlockSpec((None, None, None, past, HEAD_DIM), lambda b, h, i: (b, layer, h, 0, 0))
        specs += [cspec, cspec]
        args += [cache_k, cache_v]
    kern = functools.partial(_attn_kernel, has_cache=has_cache, g=g, tq=tq)
    return pl.pallas_call(
        kern,
        grid=(nb, nkv, l // tq),
        in_specs=specs,
        out_specs=pl.BlockSpec((tq, g * HEAD_DIM), lambda b, h, i: (b * (l // tq) + i, h)),
        out_shape=jax.ShapeDtypeStruct((nb * l, att_w), BF16),
        compiler_params=_cp("arbitrary", "arbitrary", "arbitrary"),
        name="attn_cache" if has_cache else "attn",
    )(*args)


def _dwconv_kernel(x_ref, w_ref, b_ref, o_ref, pad, *, l, width, rc, off):
    ct = x_ref.shape[1]
    zeros = jnp.zeros((off, ct), F32)
    pad[0:off, :] = zeros
    pad[off + l:off + l + off, :] = zeros
    pad[off:off + l, :] = x_ref[...]
    w = w_ref[...]
    bias = b_ref[...]
    shift = off - width // 2

    def chunk(ci, carry):
        base = pl.multiple_of(ci * rc, rc)
        win = pad[pl.ds(base, rc + 2 * off), :]
        acc = jnp.zeros((rc, ct), F32) + bias
        span = rc + 2 * off - SUBLANES
        for p in range(SUBLANES):
            taps = [k for k in range(width) if (k + shift) % SUBLANES == p]
            if taps:
                shifted = win[p:p + span, :]
                for k in taps:
                    o = k + shift - p
                    acc = acc + w[k:k + 1, :] * shifted[o:o + rc, :]
        o_ref[pl.ds(base, rc), :] = acc
        return carry

    lax.fori_loop(0, l // rc, chunk, 0)


def _dwconv(x, col0, ncols, w, b, row0, nb, l):
    width = w.shape[0]
    off = 16
    assert width // 2 <= off
    ct = _tile(ncols, 256, LANES)
    rc = _tile(l, 64)
    wp = jnp.zeros((32, ncols), F32).at[:width].set(w)
    assert col0 % ct == 0 and row0 % l == 0
    kern = functools.partial(_dwconv_kernel, l=l, width=width, rc=rc, off=off)
    return pl.pallas_call(
        kern,
        grid=(nb, ncols // ct),
        in_specs=[
            pl.BlockSpec((l, ct), lambda b_, j: (row0 // l + b_, col0 // ct + j)),
            pl.BlockSpec((32, ct), lambda b_, j: (0, j)),
            pl.BlockSpec((1, ct), lambda b_, j: (0, j)),
        ],
        out_specs=pl.BlockSpec((l, ct), lambda b_, j: (b_, j)),
        out_shape=jax.ShapeDtypeStruct((nb * l, ncols), F32),
        scratch_shapes=[pltpu.VMEM((l + 2 * off, ct), F32)],
        compiler_params=_cp("arbitrary", "arbitrary"),
        name="dwconv%d" % width,
    )(x, wp, b.reshape(1, ncols))


def _fft_plan(l):
    n = 2 * l
    if n <= FFT_DENSE_MAX or n % FFT_SPLIT != 0:
        return n, 1
    return n // FFT_SPLIT, FFT_SPLIT


def _stack3_lhs(m):
    hi = m.astype(np.float32).astype(jnp.bfloat16)
    lo = (m - np.asarray(hi.astype(np.float32), np.float64)).astype(np.float32).astype(jnp.bfloat16)
    return jnp.asarray(np.concatenate([hi, lo, hi], axis=1))


def _split3(x):
    hi = x.astype(BF16)
    lo = (x - hi.astype(F32)).astype(BF16)
    return jnp.concatenate([hi, hi, lo], axis=0)


def _mm3(lhs3, x):
    return jnp.dot(lhs3, _split3(x), preferred_element_type=F32)


def _fft_consts(l):
    n1, n2 = _fft_plan(l)
    n = n1 * n2
    n1h = n1 // 2
    k1 = np.arange(n1, dtype=np.float64)[:, None]
    ang = 2.0 * np.pi * k1 * np.arange(n1h, dtype=np.float64)[None, :] / n1
    fr, fi = np.cos(ang), -np.sin(ang)
    a_fwd = np.block([[fr, -fi], [fi, fr]])
    ang = 2.0 * np.pi * k1 * np.arange(n1, dtype=np.float64)[None, :] / n1
    a_real = np.concatenate([np.cos(ang), -np.sin(ang)], axis=0)
    angi = 2.0 * np.pi * np.arange(n1h, dtype=np.float64)[:, None] * np.arange(n1, dtype=np.float64)[None, :] / n1
    cr, ci = np.cos(angi) / n, np.sin(angi) / n
    a_inv = np.block([[cr, -ci], [ci, cr]])
    k2 = np.arange(n2, dtype=np.float64)
    angb = 2.0 * np.pi * k2[:, None] * k2[None, :] / n2
    gr, gi = np.cos(angb), -np.sin(angb)
    b_fwd = np.block([[gr, -gi], [gi, gr]])
    b_inv = np.block([[gr, gi], [-gi, gr]])
    angt = 2.0 * np.pi * (np.arange(n1, dtype=np.float64)[:, None] * k2[None, :]).reshape(n, 1) / n
    twr = np.broadcast_to(np.cos(angt), (n, LANES)).astype(np.float32)
    twi = np.broadcast_to(-np.sin(angt), (n, LANES)).astype(np.float32)
    return dict(n1=n1, n2=n2, a_fwd=_stack3_lhs(a_fwd), a_real=_stack3_lhs(a_real), a_inv=_stack3_lhs(a_inv),
                b_fwd=_stack3_lhs(b_fwd), b_inv=_stack3_lhs(b_inv), twr=jnp.asarray(twr), twi=jnp.asarray(twi))


def _rows(ref, start, size, stride):
    if stride == 1:
        return ref[pl.ds(start, size), :]
    return ref[pl.ds(start, size, stride=stride), :]


def _stage_b_fwd(pr, pi, twr_ref, twi_ref, bfwd_ref, k1, n1, n2):
    xr = _rows(pr, k1, n2, n1)
    xi = _rows(pi, k1, n2, n1)
    r0 = pl.multiple_of(k1 * n2, n2)
    cr = twr_ref[pl.ds(r0, n2), :]
    ci = twi_ref[pl.ds(r0, n2), :]
    ar = xr * cr - xi * ci
    ai = xr * ci + xi * cr
    x = _mm3(bfwd_ref[...], jnp.concatenate([ar, ai], axis=0))
    return x[:n2], x[n2:], cr, ci


def _lconv_kernel(a_ref, g_ref, kf_ref, skip_ref, afwd_ref, ainv_ref, bfwd_ref, binv_ref, twr_ref, twi_ref,
                  o_ref, pr, pi, qr, qi, *, l, n1, n2):
    n1h = n1 // 2

    def stage_a(i2, carry):
        slab = jnp.concatenate([_rows(a_ref, i2, n1h, n2), _rows(a_ref, l + i2, n1h, n2)], axis=0)
        out = _mm3(afwd_ref[...], slab)
        r0 = pl.multiple_of(i2 * n1, n1)
        pr[pl.ds(r0, n1), :] = out[:n1]
        pi[pl.ds(r0, n1), :] = out[n1:]
        return carry

    lax.fori_loop(0, n2, stage_a, 0, unroll=min(n2, FFT_UNROLL))

    if n2 == 1:
        kr, ki = kf_ref[0], kf_ref[1]
        xr, xi = pr[...], pi[...]
        qr[...] = xr * kr - xi * ki
        qi[...] = xr * ki + xi * kr
    else:
        def stage_b(k1, carry):
            xr, xi, cr, ci = _stage_b_fwd(pr, pi, twr_ref, twi_ref, bfwd_ref, k1, n1, n2)
            r0 = pl.multiple_of(k1 * n2, n2)
            kr = kf_ref[0, pl.ds(r0, n2), :]
            ki = kf_ref[1, pl.ds(r0, n2), :]
            yr = xr * kr - xi * ki
            yi = xr * ki + xi * kr
            bv = _mm3(binv_ref[...], jnp.concatenate([yr, yi], axis=0))
            br, bi = bv[:n2], bv[n2:]
            qr[pl.ds(r0, n2), :] = br * cr + bi * ci
            qi[pl.ds(r0, n2), :] = bi * cr - br * ci
            return carry

        lax.fori_loop(0, n1, stage_b, 0, unroll=FFT_UNROLL)

    skip = skip_ref[...]

    def stage_c(i2, carry):
        slab = jnp.concatenate([_rows(qr, i2, n1, n2), _rows(qi, i2, n1, n2)], axis=0)
        y = _mm3(ainv_ref[...], slab)
        for half in range(2):
            start = half * l + i2
            a = _rows(a_ref, start, n1h, n2)
            g = _rows(g_ref, start, n1h, n2)
            val = g * (y[half * n1h:(half + 1) * n1h] + skip * a)
            if n2 == 1:
                o_ref[pl.ds(start, n1h), :] = val
            else:
                o_ref[pl.ds(start, n1h, stride=n2), :] = val
        return carry

    lax.fori_loop(0, n2, stage_c, 0, unroll=min(n2, FFT_UNROLL))


def _const_spec(shape):
    zeros = (0,) * len(shape)
    return pl.BlockSpec(shape, lambda a, b: zeros, pipeline_mode=pl.Buffered(1))


def _lconv(a, a_col0, a_row0, gate, g_col0, g_row0, kf, skip, fc, nb, l, c):
    n1, n2 = fc["n1"], fc["n2"]
    n = n1 * n2
    ct = LANES
    assert nb % 2 == 0 and a_row0 % (2 * l) == 0 and g_row0 % (2 * l) == 0
    single = pl.Buffered(1)
    kern = functools.partial(_lconv_kernel, l=l, n1=n1, n2=n2)
    return pl.pallas_call(
        kern,
        grid=(c // ct, nb // 2),
        in_specs=[
            pl.BlockSpec((2 * l, ct), lambda j, p: (a_row0 // (2 * l) + p, a_col0 // ct + j), pipeline_mode=single),
            pl.BlockSpec((2 * l, ct), lambda j, p: (g_row0 // (2 * l) + p, g_col0 // ct + j), pipeline_mode=single),
            pl.BlockSpec((2, n, ct), lambda j, p: (0, 0, j), pipeline_mode=single),
            pl.BlockSpec((1, ct), lambda j, p: (0, j)),
            _const_spec(fc["a_fwd"].shape), _const_spec(fc["a_inv"].shape),
            _const_spec(fc["b_fwd"].shape), _const_spec(fc["b_inv"].shape),
            _const_spec((n, LANES)), _const_spec((n, LANES)),
        ],
        out_specs=pl.BlockSpec((2 * l, ct), lambda j, p: (p, j), pipeline_mode=single),
        out_shape=jax.ShapeDtypeStruct((nb * l, c), F32),
        scratch_shapes=[pltpu.VMEM((n, ct), F32) for _ in range(4)],
        compiler_params=_cp("arbitrary", "arbitrary"),
        name="hyena_lconv",
    )(a, gate, kf, skip.reshape(1, c), fc["a_fwd"], fc["a_inv"], fc["b_fwd"], fc["b_inv"], fc["twr"], fc["twi"])


def _hyfilt_kernel(zz_ref, tt_ref, w1_ref, b1_ref, w2_ref, b2_ref, sf_ref, w3f_ref, w3b_ref, b3f_ref, b3b_ref,
                   dl_ref, areal_ref, bfwd_ref, twr_ref, twi_ref, o_ref, h_scr, kern_scr, pr, pi, *, l, n1, n2):
    rc = _tile(l, 256)
    nch = l // rc

    @pl.when((pl.program_id(0) == 0) & (pl.program_id(1) == 0))
    def _():
        def mlp(ci, carry):
            r0 = pl.multiple_of(ci * rc, rc)
            h = jnp.dot(zz_ref[pl.ds(r0, rc), :], w1_ref[...], precision=HIGHEST, preferred_element_type=F32)
            h = jnp.sin(sf_ref[0:1, :] * (h + b1_ref[...]))
            h = jnp.dot(h, w2_ref[...], precision=HIGHEST, preferred_element_type=F32) + b2_ref[...]
            h_scr[pl.ds(r0, rc), :] = jnp.sin(sf_ref[1:2, :] * h)
            return carry

        lax.fori_loop(0, 2 * nch, mlp, 0)

    def filt(w3_ref, b3_ref, first_chunk):
        def body(ci, acc):
            r0 = pl.multiple_of((first_chunk + ci) * rc, rc)
            hk = jnp.dot(h_scr[pl.ds(r0, rc), :], w3_ref[...], precision=HIGHEST, preferred_element_type=F32)
            win = jnp.exp(-tt_ref[pl.ds(r0, rc), :] * dl_ref[...])
            row = r0 + lax.broadcasted_iota(jnp.int32, (rc, 1), 0)
            k = jnp.where(row == l, 0.0, (hk + b3_ref[...]) * win)
            kern_scr[pl.ds(r0, rc), :] = k
            return acc + jnp.sum(jnp.abs(k), axis=0, keepdims=True)
        return body

    zero = jnp.zeros((1, kern_scr.shape[1]), F32)
    total = lax.fori_loop(0, nch, filt(w3f_ref, b3f_ref, 0), zero)
    total = lax.fori_loop(0, nch, filt(w3b_ref, b3b_ref, nch), total)
    inv = 1.0 / total

    def stage_a(i2, carry):
        out = _mm3(areal_ref[...], _rows(kern_scr, i2, n1, n2) * inv)
        r0 = pl.multiple_of(i2 * n1, n1)
        pr[pl.ds(r0, n1), :] = out[:n1]
        pi[pl.ds(r0, n1), :] = out[n1:]
        return carry

    lax.fori_loop(0, n2, stage_a, 0, unroll=min(n2, FFT_UNROLL))

    if n2 == 1:
        o_ref[0] = pr[...]
        o_ref[1] = pi[...]
    else:
        def stage_b(k1, carry):
            xr, xi, _, _ = _stage_b_fwd(pr, pi, twr_ref, twi_ref, bfwd_ref, k1, n1, n2)
            r0 = pl.multiple_of(k1 * n2, n2)
            o_ref[0, pl.ds(r0, n2), :] = xr
            o_ref[1, pl.ds(r0, n2), :] = xi
            return carry

        lax.fori_loop(0, n1, stage_b, 0, unroll=FFT_UNROLL)


def _hyena_filters(l, fc, w1, b1, w2, b2, w3, b3, sin_freq, c):
    n1, n2 = fc["n1"], fc["n2"]
    n = 2 * l
    order = w3.shape[1] // (2 * c)
    hid = w1.shape[1]
    ct = LANES
    pos = np.concatenate([np.arange(l), [0], np.arange(l - 1, 0, -1)]).astype(np.float32)
    t01 = np.linspace(0.0, 1.0, l, dtype=np.float32)[pos.astype(np.int64)]
    bands = np.linspace(1e-4, HY_BANDS - 1, HY_BANDS, dtype=np.float32)
    ang = (np.float32(2.0 * math.pi) * pos / np.float32(l))[:, None] * bands
    emb = 1 + 2 * HY_BANDS
    feat = np.zeros((n, hid), np.float32)
    feat[:, :emb] = np.concatenate([t01[:, None], np.cos(ang), -np.sin(ang)], axis=-1)
    w1p = jnp.zeros((hid, hid), F32).at[:emb].set(w1)
    tt = jnp.asarray(np.broadcast_to(t01[:, None], (n, LANES)).copy())
    dmin = math.log(HY_TARGET) / HY_SLOW
    dmax = math.log(HY_TARGET) / HY_FAST
    deltas = jnp.abs(jnp.linspace(dmin, dmax, c, dtype=F32)).reshape(1, c)
    cb = c // ct
    kern = functools.partial(_hyfilt_kernel, l=l, n1=n1, n2=n2)
    return pl.pallas_call(
        kern,
        grid=(order, cb),
        in_specs=[
            _const_spec((n, hid)), _const_spec((n, LANES)),
            _const_spec((hid, hid)), _const_spec((1, hid)), _const_spec((hid, hid)), _const_spec((1, hid)),
            _const_spec((2, hid)),
            pl.BlockSpec((hid, ct), lambda o, j: (0, (2 * o) * cb + j)),
            pl.BlockSpec((hid, ct), lambda o, j: (0, (2 * o + 1) * cb + j)),
            pl.BlockSpec((1, ct), lambda o, j: (0, (2 * o) * cb + j)),
            pl.BlockSpec((1, ct), lambda o, j: (0, (2 * o + 1) * cb + j)),
            pl.BlockSpec((1, ct), lambda o, j: (0, j)),
            _const_spec(fc["a_real"].shape), _const_spec(fc["b_fwd"].shape),
            _const_spec((n, LANES)), _const_spec((n, LANES)),
        ],
        out_specs=pl.BlockSpec((None, 2, n, ct), lambda o, j: (o, 0, 0, j), pipeline_mode=pl.Buffered(1)),
        out_shape=jax.ShapeDtypeStruct((order, 2, n, c), F32),
        scratch_shapes=[pltpu.VMEM((n, hid), F32), pltpu.VMEM((n, ct), F32), pltpu.VMEM((n, ct), F32),
                        pltpu.VMEM((n, ct), F32)],
        compiler_params=_cp("arbitrary", "arbitrary"),
        name="hyena_filters",
    )(jnp.asarray(feat), tt, w1p, b1.reshape(1, hid), w2, b2.reshape(1, hid), sin_freq, w3, w3,
      b3.reshape(1, -1), b3.reshape(1, -1), deltas, fc["a_real"], fc["b_fwd"], fc["twr"], fc["twi"])


def _res_ln_epilogue(acc, x_ref, g_ref, lng_ref, lnb_ref, sc_ref, sh_ref, wr_ref, br_ref, xo_ref, h_ref, lg_ref, alpha):
    xn = _ln_rows(alpha * x_ref[...] + g_ref[...] * acc, lng_ref[...], lnb_ref[...])
    xo_ref[...] = xn
    h = xn * (1.0 + sc_ref[...]) + sh_ref[...]
    h_ref[...] = h
    lg_ref[...] = jnp.dot(h, wr_ref[...], precision=HIGHEST, preferred_element_type=F32) + br_ref[...]


def _out_ln_kernel(o_ref, y_ref, w1_ref, w2_ref, *rest, alpha):
    acc = jnp.dot(o_ref[...], w1_ref[...], preferred_element_type=F32)
    acc = acc + jnp.dot(y_ref[...].astype(BF16), w2_ref[...], preferred_element_type=F32)
    _res_ln_epilogue(acc, *rest, alpha)


def _pw2_ln_kernel(a_ref, cg_ref, cb_ref, w_ref, *rest, alpha):
    a = _silu(_ln_rows(a_ref[...], cg_ref[...], cb_ref[...])).astype(BF16)
    acc = jnp.dot(a, w_ref[...], preferred_element_type=F32)
    _res_ln_epilogue(acc, *rest, alpha)


def _res_ln_call(kern, name, lead_args, lead_specs, x, modl, grp, which_g, which_sc, which_sh, ln_g, ln_b, wr, br, tm):
    t, d = x.shape
    nr = wr.shape[1]
    row = lambda i: (i, 0)
    const = lambda i: (0, 0)
    specs = lead_specs + [
        pl.BlockSpec((tm, d), row),
        grp.mod_spec(which_g, tm, d, 1),
        pl.BlockSpec((1, d), const), pl.BlockSpec((1, d), const),
        grp.mod_spec(which_sc, tm, d, 1), grp.mod_spec(which_sh, tm, d, 1),
        pl.BlockSpec((d, nr), const, pipeline_mode=pl.Buffered(1)), pl.BlockSpec((1, nr), const),
    ]
    return pl.pallas_call(
        kern,
        grid=(t // tm,),
        in_specs=specs,
        out_specs=[pl.BlockSpec((tm, d), row), pl.BlockSpec((tm, d), row), pl.BlockSpec((tm, nr), row)],
        out_shape=[jax.ShapeDtypeStruct((t, d), F32), jax.ShapeDtypeStruct((t, d), F32),
                   jax.ShapeDtypeStruct((t, nr), F32)],
        compiler_params=_cp("arbitrary"),
        name=name,
    )(*lead_args, x, modl, ln_g.reshape(1, d), ln_b.reshape(1, d), modl, modl, wr, br)


def _out_ln(o_bf, y, w_out_bf, x, modl, grp, ln_g, ln_b, wr, br, alpha):
    t, d = x.shape
    ka = o_bf.shape[1]
    kb = y.shape[1]
    assert ka == kb
    tm = _tile(math.gcd(grp.tp, grp.ls), 512)
    single = pl.Buffered(1)
    lead_specs = [
        pl.BlockSpec((tm, ka), lambda i: (i, 0)),
        pl.BlockSpec((tm, kb), lambda i: (i, 0)),
        pl.BlockSpec((ka, d), lambda i: (0, 0), pipeline_mode=single),
        pl.BlockSpec((kb, d), lambda i: (1, 0), pipeline_mode=single),
    ]
    kern = functools.partial(_out_ln_kernel, alpha=alpha)
    return _res_ln_call(kern, "out_proj_ln", [o_bf, y, w_out_bf, w_out_bf], lead_specs, x, modl, grp, 2, 4, 3,
                        ln_g, ln_b, wr, br, tm)


def _pw2_ln(a, cv_g, cv_b, w_bf, x, modl, grp, ln_g, ln_b, wr, br, alpha):
    t, d = x.shape
    tm = _tile(math.gcd(grp.tp, grp.ls), 512)
    lead_specs = [
        pl.BlockSpec((tm, d), lambda i: (i, 0)),
        pl.BlockSpec((1, d), lambda i: (0, 0)), pl.BlockSpec((1, d), lambda i: (0, 0)),
        pl.BlockSpec((d, d), lambda i: (0, 0), pipeline_mode=pl.Buffered(1)),
    ]
    kern = functools.partial(_pw2_ln_kernel, alpha=alpha)
    return _res_ln_call(kern, "conv_proj_ln", [a, cv_g.reshape(1, d), cv_b.reshape(1, d), w_bf], lead_specs, x,
                        modl, grp, 2, 4, 3, ln_g, ln_b, wr, br, tm)


def _route_kernel(lg_ref, tri_ref, idx_ref, gate_ref, cnt_ref, carry, *, ng, epg):
    @pl.when(pl.program_id(0) == 0)
    def _():
        carry[...] = jnp.zeros_like(carry)

    lg = lg_ref[...]
    lane = lax.broadcasted_iota(jnp.int32, lg.shape, 1)
    neg = -jnp.inf
    big = lg.shape[1]
    gmask = lane < ng
    lgm = jnp.where(gmask, lg, neg)
    mg = jnp.max(lgm, axis=-1, keepdims=True)
    g = jnp.min(jnp.where(lgm == mg, lane, big), axis=-1, keepdims=True)
    gate_g = 1.0 / jnp.sum(jnp.exp(lgm - mg), axis=-1, keepdims=True)
    el = lane - ng
    lo = g * epg
    emask = (el >= lo) & (el < lo + epg)
    le = jnp.where(emask, lg, neg)
    m1 = jnp.max(le, axis=-1, keepdims=True)
    i1 = jnp.min(jnp.where(le == m1, lane, big), axis=-1, keepdims=True)
    le2 = jnp.where(lane == i1, neg, le)
    m2 = jnp.max(le2, axis=-1, keepdims=True)
    i2 = jnp.min(jnp.where(le2 == m2, lane, big), axis=-1, keepdims=True)
    r = jnp.exp(m2 - m1)
    den = 1.0 + r
    g1 = gate_g / den
    g2 = gate_g * r / den
    e1 = i1 - ng
    e2 = i2 - ng
    oh1 = lane == e1
    oh2 = lane == e2
    oh1f = jnp.where(oh1, 1.0, 0.0)
    oh2f = jnp.where(oh2, 1.0, 0.0)
    p1 = jnp.dot(tri_ref[...], oh1f.astype(BF16), preferred_element_type=F32)
    p2 = jnp.dot(tri_ref[...], oh2f.astype(BF16), preferred_element_type=F32)
    c = carry[0:1, :]
    tot1 = jnp.sum(oh1f, axis=0, keepdims=True)
    tot2 = jnp.sum(oh2f, axis=0, keepdims=True)
    r1 = jnp.sum(jnp.where(oh1, p1 + c, 0.0), axis=-1, keepdims=True)
    r2 = jnp.sum(jnp.where(oh2, p2 + c + tot1, 0.0), axis=-1, keepdims=True)
    newc = c + tot1 + tot2
    carry[...] = jnp.broadcast_to(newc, carry.shape)
    cnt_ref[...] = jnp.broadcast_to(newc, cnt_ref.shape)
    zi = jnp.zeros_like(lane)
    idx_ref[...] = jnp.where(lane == 0, e1, jnp.where(lane == 1, e2, jnp.where(
        lane == 2, r1.astype(jnp.int32), jnp.where(lane == 3, r2.astype(jnp.int32), zi))))
    gate_ref[...] = jnp.where(lane == 0, g1, jnp.where(lane == 1, g2, 0.0))


def _route(lg, ng, epg):
    t, nr = lg.shape
    tm = _tile(t, 512)
    tri = jnp.asarray(np.tril(np.ones((tm, tm), np.float32), -1)).astype(BF16)
    kern = functools.partial(_route_kernel, ng=ng, epg=epg)
    return pl.pallas_call(
        kern,
        grid=(t // tm,),
        in_specs=[pl.BlockSpec((tm, nr), lambda i: (i, 0)), pl.BlockSpec((tm, tm), lambda i: (0, 0))],
        out_specs=[pl.BlockSpec((tm, nr), lambda i: (i, 0)), pl.BlockSpec((tm, nr), lambda i: (i, 0)),
                   pl.BlockSpec((SUBLANES, nr), lambda i: (0, 0))],
        out_shape=[jax.ShapeDtypeStruct((t, nr), jnp.int32), jax.ShapeDtypeStruct((t, nr), F32),
                   jax.ShapeDtypeStruct((SUBLANES, nr), F32)],
        scratch_shapes=[pltpu.VMEM((SUBLANES, nr), F32)],
        compiler_params=_cp("arbitrary"),
        name="moe_route",
    )(lg, tri)


def _dispatch_kernel(dest_ref, h_ref, xs_in_ref, xs_ref, sem, *, tm):
    del xs_in_ref
    base = pl.program_id(0) * tm

    def issue(r, carry):
        t = base + r
        for k in range(TOP_K):
            d = dest_ref[TOP_K * t + k]
            pltpu.make_async_copy(h_ref.at[pl.ds(r, 1)], xs_ref.at[pl.ds(d, 1)], sem).start()
        return carry

    lax.fori_loop(0, tm, issue, 0)

    def drain(r, carry):
        for k in range(TOP_K):
            pltpu.make_async_copy(h_ref.at[pl.ds(0, 1)], xs_ref.at[pl.ds(0, 1)], sem).wait()
        return carry

    lax.fori_loop(0, tm, drain, 0)


def _dispatch(dest_flat, h, nslots):
    t, d = h.shape
    tm = _tile(t, 512)
    kern = functools.partial(_dispatch_kernel, tm=tm)
    return pl.pallas_call(
        kern,
        grid_spec=pltpu.PrefetchScalarGridSpec(
            num_scalar_prefetch=1,
            grid=(t // tm,),
            in_specs=[pl.BlockSpec((tm, d), lambda i, ds_: (i, 0)), pl.BlockSpec(memory_space=pl.ANY)],
            out_specs=pl.BlockSpec(memory_space=pl.ANY),
            scratch_shapes=[pltpu.SemaphoreType.DMA(())],
        ),
        out_shape=jax.ShapeDtypeStruct((nslots, d), F32),
        input_output_aliases={2: 0},
        compiler_params=_cp("arbitrary"),
        name="moe_dispatch",
    )(dest_flat, h, jnp.zeros((nslots, d), F32))


def _ffn_kernel(be_ref, nu_ref, x_ref, wg_ref, wu_ref, wd_ref, o_ref):
    b = pl.program_id(0)

    @pl.when(b < nu_ref[0])
    def _():
        x = x_ref[...].astype(BF16)
        gate = jnp.dot(x, wg_ref[...], preferred_element_type=F32)
        up = jnp.dot(x, wu_ref[...], preferred_element_type=F32)
        hid = (_silu(gate) * up).astype(BF16)
        o_ref[...] = jnp.dot(hid, wd_ref[...], preferred_element_type=F32)

    @pl.when(b >= nu_ref[0])
    def _():
        o_ref[...] = jnp.zeros_like(o_ref)


def _ffn(blk_exp, nused, xs, wg_bf, wu_bf, wd_bf):
    ns, d = xs.shape
    de = wg_bf.shape[2]
    nblk = ns // MOE_BLK
    return pl.pallas_call(
        _ffn_kernel,
        grid_spec=pltpu.PrefetchScalarGridSpec(
            num_scalar_prefetch=2,
            grid=(nblk,),
            in_specs=[
                pl.BlockSpec((MOE_BLK, d), lambda b, be, nu: (b, 0)),
                pl.BlockSpec((None, d, de), lambda b, be, nu: (be[b], 0, 0)),
                pl.BlockSpec((None, d, de), lambda b, be, nu: (be[b], 0, 0)),
                pl.BlockSpec((None, de, d), lambda b, be, nu: (be[b], 0, 0)),
            ],
            out_specs=pl.BlockSpec((MOE_BLK, d), lambda b, be, nu: (b, 0)),
        ),
        out_shape=jax.ShapeDtypeStruct((ns, d), F32),
        compiler_params=_cp("arbitrary"),
        name="moe_ffn",
    )(blk_exp, nused, xs, wg_bf, wu_bf, wd_bf)


def _combine_kernel(dest_ref, ys_ref, x_ref, gt_ref, g_ref, lng_ref, lnb_ref, o_ref, buf, sem, *, tm, alpha):
    base = pl.program_id(0) * tm

    def issue(r, carry):
        t = base + r
        for k in range(TOP_K):
            d = dest_ref[TOP_K * t + k]
            pltpu.make_async_copy(ys_ref.at[pl.ds(d, 1)], buf.at[k, pl.ds(r, 1)], sem).start()
        return carry

    lax.fori_loop(0, tm, issue, 0)

    def drain(r, carry):
        for k in range(TOP_K):
            pltpu.make_async_copy(ys_ref.at[pl.ds(0, 1)], buf.at[k, pl.ds(0, 1)], sem).wait()
        return carry

    lax.fori_loop(0, tm, drain, 0)
    gt = gt_ref[...]
    y = gt[:, 0:1] * buf[0] + gt[:, 1:2] * buf[1]
    o_ref[...] = _ln_rows(alpha * x_ref[...] + g_ref[...] * y, lng_ref[...], lnb_ref[...])


def _combine(dest_flat, ys, x, gates, modl, grp, which_g, ln_g, ln_b, alpha):
    t, d = x.shape
    nr = gates.shape[1]
    tm = _tile(math.gcd(grp.tp, grp.ls), 256)
    kern = functools.partial(_combine_kernel, tm=tm, alpha=alpha)
    gid = grp.gid
    nmod = grp.nmod
    return pl.pallas_call(
        kern,
        grid_spec=pltpu.PrefetchScalarGridSpec(
            num_scalar_prefetch=1,
            grid=(t // tm,),
            in_specs=[
                pl.BlockSpec(memory_space=pl.ANY),
                pl.BlockSpec((tm, d), lambda i, ds_: (i, 0)),
                pl.BlockSpec((tm, nr), lambda i, ds_: (i, 0)),
                pl.BlockSpec((None, 1, d), lambda i, ds_: (gid(i, tm) * nmod + which_g, 0, 0)),
                pl.BlockSpec((1, d), lambda i, ds_: (0, 0)),
                pl.BlockSpec((1, d), lambda i, ds_: (0, 0)),
            ],
            out_specs=pl.BlockSpec((tm, d), lambda i, ds_: (i, 0)),
            scratch_shapes=[pltpu.VMEM((TOP_K, tm, d), F32), pltpu.SemaphoreType.DMA(())],
        ),
        out_shape=jax.ShapeDtypeStruct((t, d), F32),
        compiler_params=_cp("arbitrary"),
        name="moe_combine_ln",
    )(dest_flat, ys, x, gates, modl, ln_g.reshape(1, d), ln_b.reshape(1, d))


def _moe(h, lg, x, modl, grp, ln_g, ln_b, wg_bf, wu_bf, wd_bf, ng, alpha):
    t, d = h.shape
    ne = wg_bf.shape[0]
    idx, gates, cnt = _route(lg, ng, ne // ng)
    experts = idx[:, 0:TOP_K]
    ranks = idx[:, TOP_K:2 * TOP_K]
    counts = cnt[0, :ne].astype(jnp.int32)
    padded = (counts + MOE_BLK - 1) // MOE_BLK * MOE_BLK
    pends = jnp.cumsum(padded)
    pstarts = pends - padded
    dest = (jnp.take(pstarts, experts) + ranks).reshape(-1).astype(jnp.int32)
    nblk = -(-(t * TOP_K) // MOE_BLK) + ne
    blk_start = jnp.arange(nblk, dtype=jnp.int32) * MOE_BLK
    blk_exp = jnp.minimum(jnp.sum(pends[None, :] <= blk_start[:, None], axis=1), ne - 1).astype(jnp.int32)
    nused = (pends[-1:] // MOE_BLK).astype(jnp.int32)
    xs = _dispatch(dest, h, nblk * MOE_BLK)
    ys = _ffn(blk_exp, nused, xs, wg_bf, wu_bf, wd_bf)
    return _combine(dest, ys, x, gates, modl, grp, 5, ln_g, ln_b, alpha)


def _rope_tables(tp, nb, l):
    rows = l // GRID_W
    row = np.repeat(np.arange(rows, dtype=np.float32), GRID_W)
    col = np.tile(np.arange(GRID_W, dtype=np.float32), rows)
    inv_freq = (np.float32(ROPE_THETA) ** (-np.arange(ROPE_PAIRS, dtype=np.float32) / np.float32(ROPE_PAIRS)))
    ar = row[:, None] * inv_freq
    ac = col[:, None] * inv_freq
    ang = np.concatenate([ar, ar, ac, ac], axis=-1).astype(np.float64)
    cos = np.cos(ang)
    sin = np.sin(ang)
    sign = np.tile(np.concatenate([-np.ones(ROPE_PAIRS), np.ones(ROPE_PAIRS)]), 2)
    cos_all = np.concatenate([np.ones((tp, HEAD_DIM)), np.tile(cos, (nb, 1))], axis=0)
    sin_all = np.concatenate([np.zeros((tp, HEAD_DIM)), np.tile(sin * sign, (nb, 1))], axis=0)
    return jnp.asarray(cos_all.astype(np.float32)), jnp.asarray(sin_all.astype(np.float32))


def kernel(x_prompt, x_sample, cache_k, cache_v, c, c_ctx, w_ada, b_ada, ln_g, ln_b, w_in_ab, q_norm_g, k_norm_g, hy_short_w, hy_short_b, hy_f_w1, hy_f_b1, hy_f_w2, hy_f_b2, hy_f_w3, hy_f_b3, hy_sin_freq, hy_skip, w_out_ab, cv_pw1, cv_dw_w, cv_dw_b, cv_ln_g, cv_ln_b, cv_pw2, moe_w_grp, moe_b_grp, moe_w_exp, moe_b_exp, moe_w_gate, moe_w_up, moe_w_down):
    bp, lp, d = x_prompt.shape
    bs, ls, _ = x_sample.shape
    depth = w_ada.shape[0]
    tp, ts = bp * lp, bs * ls
    t = tp + ts
    kv_w = cache_k.shape[2] * HEAD_DIM
    hy_w = hy_skip.shape[2]
    att_w = d - hy_w
    ng = moe_w_grp.shape[2]
    ne = moe_w_exp.shape[2]
    alpha = (2 * depth) ** 0.25
    nmod = 6
    ngrp = 1 + bs
    gpad = -(-ngrp // SUBLANES) * SUBLANES
    grp = _Groups(tp, ls, nmod)

    x = jnp.concatenate([x_prompt.reshape(tp, d), x_sample.reshape(ts, d)], axis=0)
    cond = jnp.zeros((gpad, d), F32).at[0].set(c_ctx).at[1:ngrp].set(c)
    mod = _ada(cond, w_ada, b_ada)
    nrt = LANES
    new_k, new_v = None, None
    for layer in range(depth):
        i = layer // 2
        modl = mod[layer].reshape(gpad * nmod, 1, d)
        wr = jnp.zeros((d, nrt), F32).at[:, :ng].set(moe_w_grp[layer]).at[:, ng:ng + ne].set(moe_w_exp[layer])
        br = jnp.zeros((1, nrt), F32).at[0, :ng].set(moe_b_grp[layer]).at[0, ng:ng + ne].set(moe_b_exp[layer])
        if layer % 2 == 0:
            zp = _mm_mod(x, modl, grp, 1, 0, w_in_ab[i].astype(BF16))
            cos_t, sin_t = _rope_tables(tp, bs, ls)
            q_bf, k_f, k_bf = _qkprep(zp, cos_t, sin_t, q_norm_g[i], k_norm_g[i], att_w, kv_w)
            o_p = _attention(q_bf, k_bf, zp, 0, bp, lp, att_w, kv_w)
            o_s = _attention(q_bf, k_bf, zp, tp, bs, ls, att_w, kv_w, cache_k, cache_v, i)
            o_bf = jnp.concatenate([o_p, o_s], axis=0)
            ucol = att_w + 2 * kv_w
            ys = []
            for row0, nb, l in ((0, bp, lp), (tp, bs, ls)):
                fc = _fft_consts(l)
                kf = _hyena_filters(l, fc, hy_f_w1[i], hy_f_b1[i], hy_f_w2[i], hy_f_b2[i], hy_f_w3[i], hy_f_b3[i],
                                    hy_sin_freq[i], hy_w)
                usc = _dwconv(zp, ucol, 3 * hy_w, hy_short_w[i], hy_short_b[i], row0, nb, l)
                z1 = _lconv(usc, 0, 0, usc, hy_w, 0, kf[0], hy_skip[i, 0], fc, nb, l, hy_w)
                ys.append(_lconv(z1, 0, 0, usc, 2 * hy_w, 0, kf[1], hy_skip[i, 1], fc, nb, l, hy_w))
            y_hy = jnp.concatenate(ys, axis=0)
            x, h, lg = _out_ln(o_bf, y_hy, w_out_ab[i].astype(BF16), x, modl, grp, ln_g[layer, 0], ln_b[layer, 0],
                               wr, br, alpha)
            nkv = kv_w // HEAD_DIM
            kk = k_f[:tp].reshape(bp, lp, nkv, HEAD_DIM).transpose(0, 2, 1, 3)
            vv = zp[:tp, att_w + kv_w:att_w + 2 * kv_w].reshape(bp, lp, nkv, HEAD_DIM).transpose(0, 2, 1, 3)
            new_k = kk if new_k is None else jnp.concatenate([new_k, kk], axis=1)
            new_v = vv if new_v is None else jnp.concatenate([new_v, vv], axis=1)
        else:
            a = _mm_mod(x, modl, grp, 1, 0, cv_pw1[i].astype(BF16), glu=True)
            a = jnp.concatenate([_dwconv(a, 0, d, cv_dw_w[i], cv_dw_b[i], 0, bp, lp),
                                 _dwconv(a, 0, d, cv_dw_w[i], cv_dw_b[i], tp, bs, ls)], axis=0)
            x, h, lg = _pw2_ln(a, cv_ln_g[i], cv_ln_b[i], cv_pw2[i].astype(BF16), x, modl, grp,
                               ln_g[layer, 0], ln_b[layer, 0], wr, br, alpha)
        x = _moe(h, lg, x, modl, grp, ln_g[layer, 1], ln_b[layer, 1], moe_w_gate[layer].astype(BF16),
                 moe_w_up[layer].astype(BF16), moe_w_down[layer].astype(BF16), ng, alpha)
    nkv = kv_w // HEAD_DIM
    n_attn = (depth + 1) // 2
    new_k = new_k.reshape(bp, n_attn, nkv, lp, HEAD_DIM)
    new_v = new_v.reshape(bp, n_attn, nkv, lp, HEAD_DIM)
    return (x[:tp].reshape(bp, lp, d), x[tp:].reshape(bs, ls, d), new_k, new_v)
```

```python
import functools
import math

import jax
import jax.numpy as jnp
import numpy as np
from jax import lax
from jax.experimental import pallas as pl
from jax.experimental.pallas import tpu as pltpu

F32 = jnp.float32
BF16 = jnp.bfloat16
HIGHEST = lax.Precision.HIGHEST

GRID_W = 64
HEAD_DIM = 128
ROPE_THETA = 10000.0
ROPE_PAIRS = HEAD_DIM // 4
HY_BANDS = 16
HY_TARGET = 1e-2
HY_FAST = 0.3
HY_SLOW = 1.5
TOP_K = 2
EPS = 1e-6

LANES = 128
SUBLANES = 8
VMEM_LIMIT = 56 * 1024 * 1024
MOE_BLK = 256
FFT_SPLIT = 64
FFT_DENSE_MAX = 1024
FFT_UNROLL = 4


def _cp(*sem):
    return pltpu.CompilerParams(dimension_semantics=sem, vmem_limit_bytes=VMEM_LIMIT)


def _tile(n, target, mult=SUBLANES):
    if n <= target:
        return n
    for t in range(target, 0, -1):
        if n % t == 0 and t % mult == 0:
            return t
    return n


def _silu(x):
    return x * jax.nn.sigmoid(x)


def _ln_rows(x, g, b):
    mu = jnp.mean(x, axis=-1, keepdims=True)
    xc = x - mu
    var = jnp.mean(xc * xc, axis=-1, keepdims=True)
    return xc * lax.rsqrt(var + EPS) * g + b


def _ada_kernel(c_ref, w_ref, b_ref, o_ref):
    s = _silu(c_ref[...])
    o_ref[...] = jnp.dot(s, w_ref[...], precision=HIGHEST, preferred_element_type=F32) + b_ref[...]


def _ada(cond, w_ada, b_ada):
    depth, d, n = w_ada.shape
    g = cond.shape[0]
    tn = _tile(n, 1024, LANES)
    return pl.pallas_call(
        _ada_kernel,
        grid=(depth, n // tn),
        in_specs=[
            pl.BlockSpec((g, d), lambda l, j: (0, 0)),
            pl.BlockSpec((None, d, tn), lambda l, j: (l, 0, j)),
            pl.BlockSpec((None, 1, tn), lambda l, j: (l, 0, j)),
        ],
        out_specs=pl.BlockSpec((None, g, tn), lambda l, j: (l, 0, j)),
        out_shape=jax.ShapeDtypeStruct((depth, g, n), F32),
        compiler_params=_cp("arbitrary", "arbitrary"),
        name="ada_mod",
    )(cond, w_ada, b_ada.reshape(depth, 1, n))


def _load_rows(refs, ntp):
    if len(refs) == 1:
        return refs[0][...]
    return jnp.where(pl.program_id(0) < ntp, refs[0][...], refs[1][...])


def _row_operand(arr, tm, cols, ntp, nargs):
    if isinstance(arr, tuple):
        if nargs == 2:
            maps = (lambda i, j: (jnp.minimum(i, ntp - 1), 0), lambda i, j: (jnp.maximum(i - ntp, 0), 0))
        else:
            maps = (lambda i: (jnp.minimum(i, ntp - 1), 0), lambda i: (jnp.maximum(i - ntp, 0), 0))
        return list(arr), [pl.BlockSpec((tm, cols), m) for m in maps]
    one = (lambda i, j: (i, 0)) if nargs == 2 else (lambda i: (i, 0))
    return [arr], [pl.BlockSpec((tm, cols), one)]


def _mm_mod_kernel(*refs, nx, ntp):
    sc_ref, sh_ref, w_ref, o_ref, h_scr = refs[nx:]

    @pl.when(pl.program_id(1) == 0)
    def _():
        h_scr[...] = (_load_rows(refs[:nx], ntp) * (1.0 + sc_ref[...]) + sh_ref[...]).astype(BF16)

    o_ref[...] = jnp.dot(h_scr[...], w_ref[...], preferred_element_type=F32)


def _mm_glu_kernel(*refs, nx, ntp):
    sc_ref, sh_ref, wa_ref, wg_ref, o_ref, h_scr = refs[nx:]

    @pl.when(pl.program_id(1) == 0)
    def _():
        h_scr[...] = (_load_rows(refs[:nx], ntp) * (1.0 + sc_ref[...]) + sh_ref[...]).astype(BF16)

    a = jnp.dot(h_scr[...], wa_ref[...], preferred_element_type=F32)
    g = jnp.dot(h_scr[...], wg_ref[...], preferred_element_type=F32)
    o_ref[...] = a * jax.nn.sigmoid(g)


class _Groups:
    def __init__(self, tp, ls, nmod):
        self.tp, self.ls, self.nmod = tp, ls, nmod

    def gid(self, i, tm):
        row = i * tm
        return jnp.where(row < self.tp, 0, 1 + (row - self.tp) // self.ls)

    def mod_spec(self, which, tm, d, nargs=2):
        if nargs == 2:
            return pl.BlockSpec((None, 1, d), lambda i, j: (self.gid(i, tm) * self.nmod + which, 0, 0))
        return pl.BlockSpec((None, 1, d), lambda i: (self.gid(i, tm) * self.nmod + which, 0, 0))


def _mm_mod(x, t, modl, grp, which_sc, which_sh, w_bf, glu=False):
    d = w_bf.shape[0]
    n = w_bf.shape[1] // (2 if glu else 1)
    tm = _tile(math.gcd(grp.tp, grp.ls), 1024)
    tn = _tile(n, 512, LANES)
    ntp = grp.tp // tm
    args, specs = _row_operand(x, tm, d, ntp, 2)
    nx = len(args)
    specs += [
        grp.mod_spec(which_sc, tm, d),
        grp.mod_spec(which_sh, tm, d),
        pl.BlockSpec((d, tn), lambda i, j: (0, j)),
    ]
    args += [modl, modl, w_bf]
    if glu:
        noff = n // tn
        specs.append(pl.BlockSpec((d, tn), lambda i, j: (0, j + noff)))
        args.append(w_bf)
    return pl.pallas_call(
        functools.partial(_mm_glu_kernel if glu else _mm_mod_kernel, nx=nx, ntp=ntp),
        grid=(t // tm, n // tn),
        in_specs=specs,
        out_specs=pl.BlockSpec((tm, tn), lambda i, j: (i, j)),
        out_shape=jax.ShapeDtypeStruct((t, n), F32),
        scratch_shapes=[pltpu.VMEM((tm, d), BF16)],
        compiler_params=_cp("arbitrary", "arbitrary"),
        name="mm_glu" if glu else "mm_mod",
    )(*args)


def _qkprep_kernel(q_ref, k_ref, c_ref, s_ref, qg_ref, kg_ref, qo_ref, ko_ref, kb_ref, *, nq, nk, scale):
    c = c_ref[...]
    s = s_ref[...]
    lane = lax.broadcasted_iota(jnp.int32, c.shape, 1)
    first = (lane % (2 * ROPE_PAIRS)) < ROPE_PAIRS

    def prep(xh, g):
        ms = jnp.mean(xh * xh, axis=-1, keepdims=True)
        xn = xh * lax.rsqrt(ms + EPS) * g
        rot = jnp.where(first, pltpu.roll(xn, HEAD_DIM - ROPE_PAIRS, 1), pltpu.roll(xn, ROPE_PAIRS, 1))
        return xn * c + rot * s

    for h in range(nq):
        sl = slice(h * HEAD_DIM, (h + 1) * HEAD_DIM)
        qo_ref[:, sl] = (prep(q_ref[:, sl], qg_ref[...]) * scale).astype(BF16)
    for h in range(nk):
        sl = slice(h * HEAD_DIM, (h + 1) * HEAD_DIM)
        kh = prep(k_ref[:, sl], kg_ref[...])
        ko_ref[:, sl] = kh
        kb_ref[:, sl] = kh.astype(BF16)


def _qkprep(zp, cos_t, sin_t, q_g, k_g, att_w, kv_w):
    t = zp.shape[0]
    tm = _tile(t, 512)
    nq, nk = att_w // HEAD_DIM, kv_w // HEAD_DIM
    assert att_w % kv_w == 0
    kern = functools.partial(_qkprep_kernel, nq=nq, nk=nk, scale=HEAD_DIM ** -0.5)
    return pl.pallas_call(
        kern,
        grid=(t // tm,),
        in_specs=[
            pl.BlockSpec((tm, att_w), lambda i: (i, 0)),
            pl.BlockSpec((tm, kv_w), lambda i: (i, att_w // kv_w)),
            pl.BlockSpec((tm, HEAD_DIM), lambda i: (i, 0)),
            pl.BlockSpec((tm, HEAD_DIM), lambda i: (i, 0)),
            pl.BlockSpec((1, HEAD_DIM), lambda i: (0, 0)),
            pl.BlockSpec((1, HEAD_DIM), lambda i: (0, 0)),
        ],
        out_specs=[
            pl.BlockSpec((tm, att_w), lambda i: (i, 0)),
            pl.BlockSpec((tm, kv_w), lambda i: (i, 0)),
            pl.BlockSpec((tm, kv_w), lambda i: (i, 0)),
        ],
        out_shape=[
            jax.ShapeDtypeStruct((t, att_w), BF16),
            jax.ShapeDtypeStruct((t, kv_w), F32),
            jax.ShapeDtypeStruct((t, kv_w), BF16),
        ],
        compiler_params=_cp("arbitrary"),
        name="qk_prep",
    )(zp, zp, cos_t, sin_t, q_g.reshape(1, HEAD_DIM), k_g.reshape(1, HEAD_DIM))


def _attn_kernel(*refs, has_cache, g, tq):
    if has_cache:
        q_ref, k_ref, v_ref, kc_ref, vc_ref, o_ref = refs
    else:
        q_ref, k_ref, v_ref, o_ref = refs
    qs = jnp.concatenate([q_ref[:, h * HEAD_DIM:(h + 1) * HEAD_DIM] for h in range(g)], axis=0)
    nt = (((1,), (1,)), ((), ()))
    s = lax.dot_general(qs, k_ref[...], nt, preferred_element_type=F32)
    m = jnp.max(s, axis=-1, keepdims=True)
    if has_cache:
        sc = lax.dot_general(qs, kc_ref[...].astype(BF16), nt, preferred_element_type=F32)
        m = jnp.maximum(m, jnp.max(sc, axis=-1, keepdims=True))
    p = jnp.exp(s - m)
    l = jnp.sum(p, axis=-1, keepdims=True)
    o = jnp.dot(p.astype(BF16), v_ref[...].astype(BF16), preferred_element_type=F32)
    if has_cache:
        pc = jnp.exp(sc - m)
        l = l + jnp.sum(pc, axis=-1, keepdims=True)
        o = o + jnp.dot(pc.astype(BF16), vc_ref[...].astype(BF16), preferred_element_type=F32)
    o = o / l
    for h in range(g):
        o_ref[:, h * HEAD_DIM:(h + 1) * HEAD_DIM] = o[h * tq:(h + 1) * tq].astype(BF16)


def _attention(q_bf, k_bf, zp, row0, nb, l, att_w, kv_w, cache_k=None, cache_v=None, layer=0):
    nkv = kv_w // HEAD_DIM
    g = att_w // kv_w
    tq = _tile(l, max(SUBLANES, 512 // g))
    rb = row0 // l
    assert row0 % l == 0
    vcol = (att_w + kv_w) // HEAD_DIM
    has_cache = cache_k is not None
    specs = [
        pl.BlockSpec((tq, g * HEAD_DIM), lambda b, h, i: ((row0 + b * l) // tq + i, h)),
        pl.BlockSpec((l, HEAD_DIM), lambda b, h, i: (rb + b, h)),
        pl.BlockSpec((l, HEAD_DIM), lambda b, h, i: (rb + b, vcol + h)),
    ]
    args = [q_bf, k_bf, zp]
    if has_cache:
        past = cache_k.shape[3]
        cspec = pl.BlockSpec((None, None, None, past, HEAD_DIM), lambda b, h, i: (b, layer, h, 0, 0))
        specs += [cspec, cspec]
        args += [cache_k, cache_v]
    kern = functools.partial(_attn_kernel, has_cache=has_cache, g=g, tq=tq)
    return pl.pallas_call(
        kern,
        grid=(nb, nkv, l // tq),
        in_specs=specs,
        out_specs=pl.BlockSpec((tq, g * HEAD_DIM), lambda b, h, i: (b * (l // tq) + i, h)),
        out_shape=jax.ShapeDtypeStruct((nb * l, att_w), BF16),
        compiler_params=_cp("arbitrary", "arbitrary", "arbitrary"),
        name="attn_cache" if has_cache else "attn",
    )(*args)


def _dwconv_kernel(x_ref, w_ref, b_ref, o_ref, pad, *, l, width, rc, off):
    ct = x_ref.shape[1]
    zeros = jnp.zeros((off, ct), F32)
    pad[0:off, :] = zeros
    pad[off + l:off + l + off, :] = zeros
    pad[off:off + l, :] = x_ref[...]
    bias = b_ref[...]
    shift = off - width // 2

    def chunk(ci, carry):
        base = pl.multiple_of(ci * rc, rc)
        win = pad[pl.ds(base, rc + 2 * off), :]
        acc = jnp.zeros((rc, ct), F32) + bias
        span = rc + 2 * off - SUBLANES
        for p in range(SUBLANES):
            taps = [k for k in range(width) if (k + shift) % SUBLANES == p]
            if taps:
                shifted = win[p:p + span, :]
                for k in taps:
                    o = k + shift - p
                    wk = jnp.concatenate([w_ref[SUBLANES * k:SUBLANES * (k + 1), :]] * (rc // SUBLANES), axis=0)
                    acc = acc + wk * shifted[o:o + rc, :]
        o_ref[pl.ds(base, rc), :] = acc
        return carry

    lax.fori_loop(0, l // rc, chunk, 0)


def _dwconv(x, col0, ncols, w, b, row0, nb, l):
    width = w.shape[0]
    off = 16
    assert width // 2 <= off
    ct = LANES
    rc = _tile(l, 64)
    assert rc % SUBLANES == 0
    wp = jnp.broadcast_to(w[:, None, :], (width, SUBLANES, ncols)).reshape(width * SUBLANES, ncols)
    assert col0 % ct == 0 and row0 % l == 0
    kern = functools.partial(_dwconv_kernel, l=l, width=width, rc=rc, off=off)
    return pl.pallas_call(
        kern,
        grid=(nb, ncols // ct),
        in_specs=[
            pl.BlockSpec((l, ct), lambda b_, j: (row0 // l + b_, col0 // ct + j)),
            pl.BlockSpec((width * SUBLANES, ct), lambda b_, j: (0, j)),
            pl.BlockSpec((1, ct), lambda b_, j: (0, j)),
        ],
        out_specs=pl.BlockSpec((l, ct), lambda b_, j: (b_, j)),
        out_shape=jax.ShapeDtypeStruct((nb * l, ncols), F32),
        scratch_shapes=[pltpu.VMEM((l + 2 * off, ct), F32)],
        compiler_params=_cp("arbitrary", "arbitrary"),
        name="dwconv%d" % width,
    )(x, wp, b.reshape(1, ncols))


def _fft_plan(l):
    n = 2 * l
    if n <= FFT_DENSE_MAX or n % FFT_SPLIT != 0:
        return n, 1
    return n // FFT_SPLIT, FFT_SPLIT


def _stack3_lhs(m):
    hi = m.astype(np.float32).astype(jnp.bfloat16)
    lo = (m - np.asarray(hi.astype(np.float32), np.float64)).astype(np.float32).astype(jnp.bfloat16)
    return jnp.asarray(np.concatenate([hi, lo, hi], axis=1))


def _split3(x):
    hi = x.astype(BF16)
    lo = (x - hi.astype(F32)).astype(BF16)
    return jnp.concatenate([hi, hi, lo], axis=0)


def _mm3(lhs3, x):
    return jnp.dot(lhs3, _split3(x), preferred_element_type=F32)


def _fft_consts(l):
    n1, n2 = _fft_plan(l)
    n = n1 * n2
    n1h = n1 // 2
    k1 = np.arange(n1, dtype=np.float64)[:, None]
    ang = 2.0 * np.pi * k1 * np.arange(n1h, dtype=np.float64)[None, :] / n1
    fr, fi = np.cos(ang), -np.sin(ang)
    a_fwd = np.block([[fr, -fi], [fi, fr]])
    ang = 2.0 * np.pi * k1 * np.arange(n1, dtype=np.float64)[None, :] / n1
    a_real = np.concatenate([np.cos(ang), -np.sin(ang)], axis=0)
    angi = 2.0 * np.pi * np.arange(n1h, dtype=np.float64)[:, None] * np.arange(n1, dtype=np.float64)[None, :] / n1
    cr, ci = np.cos(angi) / n, np.sin(angi) / n
    a_inv = np.block([[cr, -ci], [ci, cr]])
    k2 = np.arange(n2, dtype=np.float64)
    angb = 2.0 * np.pi * k2[:, None] * k2[None, :] / n2
    gr, gi = np.cos(angb), -np.sin(angb)
    b_fwd = np.block([[gr, -gi], [gi, gr]])
    b_inv = np.block([[gr, gi], [-gi, gr]])
    angt = 2.0 * np.pi * (np.arange(n1, dtype=np.float64)[:, None] * k2[None, :]).reshape(n, 1) / n
    twr = np.broadcast_to(np.cos(angt), (n, LANES)).astype(np.float32)
    twi = np.broadcast_to(-np.sin(angt), (n, LANES)).astype(np.float32)
    return dict(n1=n1, n2=n2, a_fwd=_stack3_lhs(a_fwd), a_real=_stack3_lhs(a_real), a_inv=_stack3_lhs(a_inv),
                b_fwd=_stack3_lhs(b_fwd), b_inv=_stack3_lhs(b_inv), twr=jnp.asarray(twr), twi=jnp.asarray(twi))


def _rows(ref, start, size, stride):
    if stride == 1:
        return ref[pl.ds(start, size), :]
    return ref[pl.ds(start, size, stride=stride), :]


def _stage_b_fwd(pr, pi, twr_ref, twi_ref, bfwd_ref, k1, n1, n2):
    xr = _rows(pr, k1, n2, n1)
    xi = _rows(pi, k1, n2, n1)
    r0 = pl.multiple_of(k1 * n2, n2)
    cr = twr_ref[pl.ds(r0, n2), :]
    ci = twi_ref[pl.ds(r0, n2), :]
    ar = xr * cr - xi * ci
    ai = xr * ci + xi * cr
    x = _mm3(bfwd_ref[...], jnp.concatenate([ar, ai], axis=0))
    return x[:n2], x[n2:], cr, ci


def _lconv_kernel(a_ref, g_ref, kf_ref, skip_ref, afwd_ref, ainv_ref, bfwd_ref, binv_ref, twr_ref, twi_ref,
                  o_ref, pr, pi, qr, qi, *, l, n1, n2):
    n1h = n1 // 2

    def stage_a(i2, carry):
        slab = jnp.concatenate([_rows(a_ref, i2, n1h, n2), _rows(a_ref, l + i2, n1h, n2)], axis=0)
        out = _mm3(afwd_ref[...], slab)
        r0 = pl.multiple_of(i2 * n1, n1)
        pr[pl.ds(r0, n1), :] = out[:n1]
        pi[pl.ds(r0, n1), :] = out[n1:]
        return carry

    lax.fori_loop(0, n2, stage_a, 0, unroll=min(n2, FFT_UNROLL))

    if n2 == 1:
        kr, ki = kf_ref[0], kf_ref[1]
        xr, xi = pr[...], pi[...]
        qr[...] = xr * kr - xi * ki
        qi[...] = xr * ki + xi * kr
    else:
        def stage_b(k1, carry):
            xr, xi, cr, ci = _stage_b_fwd(pr, pi, twr_ref, twi_ref, bfwd_ref, k1, n1, n2)
            r0 = pl.multiple_of(k1 * n2, n2)
            kr = kf_ref[0, pl.ds(r0, n2), :]
            ki = kf_ref[1, pl.ds(r0, n2), :]
            yr = xr * kr - xi * ki
            yi = xr * ki + xi * kr
            bv = _mm3(binv_ref[...], jnp.concatenate([yr, yi], axis=0))
            br, bi = bv[:n2], bv[n2:]
            qr[pl.ds(r0, n2), :] = br * cr + bi * ci
            qi[pl.ds(r0, n2), :] = bi * cr - br * ci
            return carry

        lax.fori_loop(0, n1, stage_b, 0, unroll=FFT_UNROLL)

    skip = skip_ref[...]

    def stage_c(i2, carry):
        slab = jnp.concatenate([_rows(qr, i2, n1, n2), _rows(qi, i2, n1, n2)], axis=0)
        y = _mm3(ainv_ref[...], slab)
        for half in range(2):
            start = half * l + i2
            a = _rows(a_ref, start, n1h, n2)
            g = _rows(g_ref, start, n1h, n2)
            val = g * (y[half * n1h:(half + 1) * n1h] + skip * a)
            if n2 == 1:
                o_ref[pl.ds(start, n1h), :] = val
            else:
                o_ref[pl.ds(start, n1h, stride=n2), :] = val
        return carry

    lax.fori_loop(0, n2, stage_c, 0, unroll=min(n2, FFT_UNROLL))


def _const_spec(shape):
    zeros = (0,) * len(shape)
    return pl.BlockSpec(shape, lambda a, b: zeros, pipeline_mode=pl.Buffered(1))


def _lconv(a, a_col0, a_row0, gate, g_col0, g_row0, kf, skip, fc, nb, l, c):
    n1, n2 = fc["n1"], fc["n2"]
    n = n1 * n2
    ct = LANES
    assert nb % 2 == 0 and a_row0 % (2 * l) == 0 and g_row0 % (2 * l) == 0
    single = pl.Buffered(1) if n2 > 1 else None
    kern = functools.partial(_lconv_kernel, l=l, n1=n1, n2=n2)
    return pl.pallas_call(
        kern,
        grid=(c // ct, nb // 2),
        in_specs=[
            pl.BlockSpec((2 * l, ct), lambda j, p: (a_row0 // (2 * l) + p, a_col0 // ct + j), pipeline_mode=single),
            pl.BlockSpec((2 * l, ct), lambda j, p: (g_row0 // (2 * l) + p, g_col0 // ct + j), pipeline_mode=single),
            pl.BlockSpec((2, n, ct), lambda j, p: (0, 0, j), pipeline_mode=single),
            pl.BlockSpec((1, ct), lambda j, p: (0, j)),
            _const_spec(fc["a_fwd"].shape), _const_spec(fc["a_inv"].shape),
            _const_spec(fc["b_fwd"].shape), _const_spec(fc["b_inv"].shape),
            _const_spec((n, LANES)), _const_spec((n, LANES)),
        ],
        out_specs=pl.BlockSpec((2 * l, ct), lambda j, p: (p, j), pipeline_mode=single),
        out_shape=jax.ShapeDtypeStruct((nb * l, c), F32),
        scratch_shapes=[pltpu.VMEM((n, ct), F32) for _ in range(4)],
        compiler_params=_cp("arbitrary", "arbitrary"),
        name="hyena_lconv",
    )(a, gate, kf, skip.reshape(1, c), fc["a_fwd"], fc["a_inv"], fc["b_fwd"], fc["b_inv"], fc["twr"], fc["twi"])


def _hyfilt_kernel(zz_ref, tt_ref, w1_ref, b1_ref, w2_ref, b2_ref, sf_ref, w3f_ref, w3b_ref, b3f_ref, b3b_ref,
                   dl_ref, areal_ref, bfwd_ref, twr_ref, twi_ref, o_ref, h_scr, kern_scr, pr, pi, *, l, n1, n2):
    rc = _tile(l, 256)
    nch = l // rc

    @pl.when((pl.program_id(0) == 0) & (pl.program_id(1) == 0))
    def _():
        def mlp(ci, carry):
            r0 = pl.multiple_of(ci * rc, rc)
            h = jnp.dot(zz_ref[pl.ds(r0, rc), :], w1_ref[...], precision=HIGHEST, preferred_element_type=F32)
            h = jnp.sin(sf_ref[0:1, :] * (h + b1_ref[...]))
            h = jnp.dot(h, w2_ref[...], precision=HIGHEST, preferred_element_type=F32) + b2_ref[...]
            h_scr[pl.ds(r0, rc), :] = jnp.sin(sf_ref[1:2, :] * h)
            return carry

        lax.fori_loop(0, 2 * nch, mlp, 0)

    def filt(w3_ref, b3_ref, first_chunk):
        def body(ci, acc):
            r0 = pl.multiple_of((first_chunk + ci) * rc, rc)
            hk = jnp.dot(h_scr[pl.ds(r0, rc), :], w3_ref[...], precision=HIGHEST, preferred_element_type=F32)
            win = jnp.exp(-tt_ref[pl.ds(r0, rc), :] * dl_ref[...])
            row = r0 + lax.broadcasted_iota(jnp.int32, (rc, 1), 0)
            k = jnp.where(row == l, 0.0, (hk + b3_ref[...]) * win)
            kern_scr[pl.ds(r0, rc), :] = k
            return acc + jnp.sum(jnp.abs(k), axis=0, keepdims=True)
        return body

    zero = jnp.zeros((1, kern_scr.shape[1]), F32)
    total = lax.fori_loop(0, nch, filt(w3f_ref, b3f_ref, 0), zero)
    total = lax.fori_loop(0, nch, filt(w3b_ref, b3b_ref, nch), total)
    inv = 1.0 / total

    def stage_a(i2, carry):
        out = _mm3(areal_ref[...], _rows(kern_scr, i2, n1, n2) * inv)
        r0 = pl.multiple_of(i2 * n1, n1)
        pr[pl.ds(r0, n1), :] = out[:n1]
        pi[pl.ds(r0, n1), :] = out[n1:]
        return carry

    lax.fori_loop(0, n2, stage_a, 0, unroll=min(n2, FFT_UNROLL))

    if n2 == 1:
        o_ref[0] = pr[...]
        o_ref[1] = pi[...]
    else:
        def stage_b(k1, carry):
            xr, xi, _, _ = _stage_b_fwd(pr, pi, twr_ref, twi_ref, bfwd_ref, k1, n1, n2)
            r0 = pl.multiple_of(k1 * n2, n2)
            o_ref[0, pl.ds(r0, n2), :] = xr
            o_ref[1, pl.ds(r0, n2), :] = xi
            return carry

        lax.fori_loop(0, n1, stage_b, 0, unroll=FFT_UNROLL)


def _hyena_filters(l, fc, w1, b1, w2, b2, w3, b3, sin_freq, c):
    n1, n2 = fc["n1"], fc["n2"]
    n = 2 * l
    order = w3.shape[1] // (2 * c)
    hid = w1.shape[1]
    ct = LANES
    pos = np.concatenate([np.arange(l), [0], np.arange(l - 1, 0, -1)]).astype(np.float32)
    t01 = np.linspace(0.0, 1.0, l, dtype=np.float32)[pos.astype(np.int64)]
    bands = np.linspace(1e-4, HY_BANDS - 1, HY_BANDS, dtype=np.float32)
    ang = (np.float32(2.0 * math.pi) * pos / np.float32(l))[:, None] * bands
    emb = 1 + 2 * HY_BANDS
    feat = np.zeros((n, hid), np.float32)
    feat[:, :emb] = np.concatenate([t01[:, None], np.cos(ang), -np.sin(ang)], axis=-1)
    w1p = jnp.zeros((hid, hid), F32).at[:emb].set(w1)
    tt = jnp.asarray(np.broadcast_to(t01[:, None], (n, LANES)).copy())
    dmin = math.log(HY_TARGET) / HY_SLOW
    dmax = math.log(HY_TARGET) / HY_FAST
    deltas = jnp.abs(jnp.linspace(dmin, dmax, c, dtype=F32)).reshape(1, c)
    cb = c // ct
    kern = functools.partial(_hyfilt_kernel, l=l, n1=n1, n2=n2)
    return pl.pallas_call(
        kern,
        grid=(order, cb),
        in_specs=[
            _const_spec((n, hid)), _const_spec((n, LANES)),
            _const_spec((hid, hid)), _const_spec((1, hid)), _const_spec((hid, hid)), _const_spec((1, hid)),
            _const_spec((2, hid)),
            pl.BlockSpec((hid, ct), lambda o, j: (0, (2 * o) * cb + j)),
            pl.BlockSpec((hid, ct), lambda o, j: (0, (2 * o + 1) * cb + j)),
            pl.BlockSpec((1, ct), lambda o, j: (0, (2 * o) * cb + j)),
            pl.BlockSpec((1, ct), lambda o, j: (0, (2 * o + 1) * cb + j)),
            pl.BlockSpec((1, ct), lambda o, j: (0, j)),
            _const_spec(fc["a_real"].shape), _const_spec(fc["b_fwd"].shape),
            _const_spec((n, LANES)), _const_spec((n, LANES)),
        ],
        out_specs=pl.BlockSpec((None, 2, n, ct), lambda o, j: (o, 0, 0, j), pipeline_mode=pl.Buffered(1)),
        out_shape=jax.ShapeDtypeStruct((order, 2, n, c), F32),
        scratch_shapes=[pltpu.VMEM((n, hid), F32), pltpu.VMEM((n, ct), F32), pltpu.VMEM((n, ct), F32),
                        pltpu.VMEM((n, ct), F32)],
        compiler_params=_cp("arbitrary", "arbitrary"),
        name="hyena_filters",
    )(jnp.asarray(feat), tt, w1p, b1.reshape(1, hid), w2, b2.reshape(1, hid), sin_freq, w3, w3,
      b3.reshape(1, -1), b3.reshape(1, -1), deltas, fc["a_real"], fc["b_fwd"], fc["twr"], fc["twi"])


def _res_ln_epilogue(acc, x, g_ref, lng_ref, lnb_ref, sc_ref, sh_ref, wr_ref, br_ref, xo_ref, h_ref, lg_ref, alpha):
    xn = _ln_rows(alpha * x + g_ref[...] * acc, lng_ref[...], lnb_ref[...])
    xo_ref[...] = xn
    h = xn * (1.0 + sc_ref[...]) + sh_ref[...]
    h_ref[...] = h
    lg_ref[...] = jnp.dot(h, wr_ref[...], precision=HIGHEST, preferred_element_type=F32) + br_ref[...]


def _out_ln_kernel(*refs, no, ny, nx, ntp, alpha):
    o = _load_rows(refs[:no], ntp)
    y = _load_rows(refs[no:no + ny], ntp)
    w1_ref, w2_ref = refs[no + ny:no + ny + 2]
    rest = refs[no + ny + 2:]
    acc = jnp.dot(o, w1_ref[...], preferred_element_type=F32)
    acc = acc + jnp.dot(y.astype(BF16), w2_ref[...], preferred_element_type=F32)
    _res_ln_epilogue(acc, _load_rows(rest[:nx], ntp), *rest[nx:], alpha)


def _pw2_ln_kernel(*refs, na, nx, ntp, alpha):
    cg_ref, cb_ref, w_ref = refs[na:na + 3]
    rest = refs[na + 3:]
    a = _silu(_ln_rows(_load_rows(refs[:na], ntp), cg_ref[...], cb_ref[...])).astype(BF16)
    acc = jnp.dot(a, w_ref[...], preferred_element_type=F32)
    _res_ln_epilogue(acc, _load_rows(rest[:nx], ntp), *rest[nx:], alpha)


def _res_ln_call(kern, name, lead_args, lead_specs, x, t, modl, grp, which_g, which_sc, which_sh, ln_g, ln_b, wr, br,
                 tm):
    d, nr = wr.shape
    row = lambda i: (i, 0)
    const = lambda i: (0, 0)
    xargs, xspecs = _row_operand(x, tm, d, grp.tp // tm, 1)
    specs = lead_specs + xspecs + [
        grp.mod_spec(which_g, tm, d, 1),
        pl.BlockSpec((1, d), const), pl.BlockSpec((1, d), const),
        grp.mod_spec(which_sc, tm, d, 1), grp.mod_spec(which_sh, tm, d, 1),
        pl.BlockSpec((d, nr), const, pipeline_mode=pl.Buffered(1)), pl.BlockSpec((1, nr), const),
    ]
    return pl.pallas_call(
        functools.partial(kern, nx=len(xargs)),
        grid=(t // tm,),
        in_specs=specs,
        out_specs=[pl.BlockSpec((tm, d), row), pl.BlockSpec((tm, d), row), pl.BlockSpec((tm, nr), row)],
        out_shape=[jax.ShapeDtypeStruct((t, d), F32), jax.ShapeDtypeStruct((t, d), F32),
                   jax.ShapeDtypeStruct((t, nr), F32)],
        compiler_params=_cp("arbitrary"),
        name=name,
    )(*lead_args, *xargs, modl, ln_g.reshape(1, d), ln_b.reshape(1, d), modl, modl, wr, br)


def _out_ln(o_bf, y, w_out_bf, x, t, modl, grp, ln_g, ln_b, wr, br, alpha):
    d = w_out_bf.shape[1]
    ka = d // 2
    tm = _tile(math.gcd(grp.tp, grp.ls), 512)
    ntp = grp.tp // tm
    single = pl.Buffered(1)
    oargs, ospecs = _row_operand(o_bf, tm, ka, ntp, 1)
    yargs, yspecs = _row_operand(y, tm, ka, ntp, 1)
    lead_specs = ospecs + yspecs + [
        pl.BlockSpec((ka, d), lambda i: (0, 0), pipeline_mode=single),
        pl.BlockSpec((ka, d), lambda i: (1, 0), pipeline_mode=single),
    ]
    kern = functools.partial(_out_ln_kernel, no=len(oargs), ny=len(yargs), ntp=ntp, alpha=alpha)
    return _res_ln_call(kern, "out_proj_ln", oargs + yargs + [w_out_bf, w_out_bf], lead_specs, x, t, modl, grp,
                        2, 4, 3, ln_g, ln_b, wr, br, tm)


def _pw2_ln(a, cv_g, cv_b, w_bf, x, t, modl, grp, ln_g, ln_b, wr, br, alpha):
    d = w_bf.shape[0]
    tm = _tile(math.gcd(grp.tp, grp.ls), 512)
    ntp = grp.tp // tm
    aargs, aspecs = _row_operand(a, tm, d, ntp, 1)
    lead_specs = aspecs + [
        pl.BlockSpec((1, d), lambda i: (0, 0)), pl.BlockSpec((1, d), lambda i: (0, 0)),
        pl.BlockSpec((d, d), lambda i: (0, 0), pipeline_mode=pl.Buffered(1)),
    ]
    kern = functools.partial(_pw2_ln_kernel, na=len(aargs), ntp=ntp, alpha=alpha)
    return _res_ln_call(kern, "conv_proj_ln", aargs + [cv_g.reshape(1, d), cv_b.reshape(1, d), w_bf], lead_specs,
                        x, t, modl, grp, 2, 4, 3, ln_g, ln_b, wr, br, tm)


def _route_kernel(lg_ref, tri_ref, idx_ref, gate_ref, cnt_ref, carry, *, ng, epg):
    @pl.when(pl.program_id(0) == 0)
    def _():
        carry[...] = jnp.zeros_like(carry)

    lg = lg_ref[...]
    lane = lax.broadcasted_iota(jnp.int32, lg.shape, 1)
    neg = -jnp.inf
    big = lg.shape[1]
    gmask = lane < ng
    lgm = jnp.where(gmask, lg, neg)
    mg = jnp.max(lgm, axis=-1, keepdims=True)
    g = jnp.min(jnp.where(lgm == mg, lane, big), axis=-1, keepdims=True)
    gate_g = 1.0 / jnp.sum(jnp.exp(lgm - mg), axis=-1, keepdims=True)
    el = lane - ng
    lo = g * epg
    emask = (el >= lo) & (el < lo + epg)
    le = jnp.where(emask, lg, neg)
    m1 = jnp.max(le, axis=-1, keepdims=True)
    i1 = jnp.min(jnp.where(le == m1, lane, big), axis=-1, keepdims=True)
    le2 = jnp.where(lane == i1, neg, le)
    m2 = jnp.max(le2, axis=-1, keepdims=True)
    i2 = jnp.min(jnp.where(le2 == m2, lane, big), axis=-1, keepdims=True)
    r = jnp.exp(m2 - m1)
    den = 1.0 + r
    g1 = gate_g / den
    g2 = gate_g * r / den
    e1 = i1 - ng
    e2 = i2 - ng
    oh1 = lane == e1
    oh2 = lane == e2
    oh1f = jnp.where(oh1, 1.0, 0.0)
    oh2f = jnp.where(oh2, 1.0, 0.0)
    p1 = jnp.dot(tri_ref[...], oh1f.astype(BF16), preferred_element_type=F32)
    p2 = jnp.dot(tri_ref[...], oh2f.astype(BF16), preferred_element_type=F32)
    c = carry[0:1, :]
    tot1 = jnp.sum(oh1f, axis=0, keepdims=True)
    tot2 = jnp.sum(oh2f, axis=0, keepdims=True)
    r1 = jnp.sum(jnp.where(oh1, p1 + c, 0.0), axis=-1, keepdims=True)
    r2 = jnp.sum(jnp.where(oh2, p2 + c + tot1, 0.0), axis=-1, keepdims=True)
    newc = c + tot1 + tot2
    carry[...] = jnp.broadcast_to(newc, carry.shape)
    cnt_ref[...] = jnp.broadcast_to(newc, cnt_ref.shape)
    zi = jnp.zeros_like(lane)
    idx_ref[...] = jnp.where(lane == 0, e1, jnp.where(lane == 1, e2, jnp.where(
        lane == 2, r1.astype(jnp.int32), jnp.where(lane == 3, r2.astype(jnp.int32), zi))))
    gate_ref[...] = jnp.where(lane == 0, g1, jnp.where(lane == 1, g2, 0.0))


def _route(lg, ng, epg):
    t, nr = lg.shape
    tm = _tile(t, 512)
    tri = jnp.asarray(np.tril(np.ones((tm, tm), np.float32), -1)).astype(BF16)
    kern = functools.partial(_route_kernel, ng=ng, epg=epg)
    return pl.pallas_call(
        kern,
        grid=(t // tm,),
        in_specs=[pl.BlockSpec((tm, nr), lambda i: (i, 0)), pl.BlockSpec((tm, tm), lambda i: (0, 0))],
        out_specs=[pl.BlockSpec((tm, nr), lambda i: (i, 0)), pl.BlockSpec((tm, nr), lambda i: (i, 0)),
                   pl.BlockSpec((SUBLANES, nr), lambda i: (0, 0))],
        out_shape=[jax.ShapeDtypeStruct((t, nr), jnp.int32), jax.ShapeDtypeStruct((t, nr), F32),
                   jax.ShapeDtypeStruct((SUBLANES, nr), F32)],
        scratch_shapes=[pltpu.VMEM((SUBLANES, nr), F32)],
        compiler_params=_cp("arbitrary"),
        name="moe_route",
    )(lg, tri)


def _ffn_kernel(be_ref, tok_ref, dst_ref, h_ref, wg_ref, wu_ref, wd_ref, o_ref, xbuf, ybuf, gsem, ssem,
                *, blk, nblk, nchunk):
    del be_ref
    b = pl.program_id(0)
    slot = b % 2
    other = 1 - slot

    def gather_copy(src_row, s, r):
        return pltpu.make_async_copy(h_ref.at[pl.ds(src_row, 1)], xbuf.at[s, pl.ds(r, 1)], gsem)

    def scatter_copy(s, r, dst_row):
        return pltpu.make_async_copy(ybuf.at[s, pl.ds(r, 1)], o_ref.at[pl.ds(dst_row, 1)], ssem)

    def wait_rows(make_copy):
        def body(r, carry):
            make_copy(r).wait()
            return carry

        lax.fori_loop(0, blk, body, 0, unroll=8)

    @pl.when(b == 0)
    def _():
        ybuf[...] = jnp.zeros_like(ybuf)

        def body(r, carry):
            gather_copy(tok_ref[r], 0, r).start()
            return carry

        lax.fori_loop(0, blk, body, 0)

    wait_rows(lambda r: gather_copy(0, slot, r))

    @pl.when(b >= 1)
    def _():
        wait_rows(lambda r: scatter_copy(slot, r, 0))

    nxt = jnp.minimum(b + 1, nblk - 1) * blk
    prv = b * blk
    x = xbuf[slot].astype(BF16)
    de = wg_ref.shape[1]
    hc = de // nchunk
    rows = blk // nchunk
    for c in range(nchunk):
        cols = slice(c * hc, (c + 1) * hc)
        gate = jnp.dot(x, wg_ref[:, cols], preferred_element_type=F32)
        up = jnp.dot(x, wu_ref[:, cols], preferred_element_type=F32)
        hid = (_silu(gate) * up).astype(BF16)
        part = jnp.dot(hid, wd_ref[cols, :], preferred_element_type=F32)
        if c == 0:
            ybuf[slot] = part
        else:
            ybuf[slot] = ybuf[slot] + part
        for r in range(c * rows, (c + 1) * rows):
            gather_copy(tok_ref[nxt + r], other, r).start()
            scatter_copy(other, r, dst_ref[prv + r]).start()

    @pl.when(b == nblk)
    def _():
        wait_rows(lambda r: gather_copy(0, other, r))
        wait_rows(lambda r: scatter_copy(other, r, 0))


def _ffn(blk_exp, tok, dstp, h, wg_bf, wu_bf, wd_bf, nrows_out):
    d = h.shape[1]
    de = wg_bf.shape[2]
    nblk = tok.shape[0] // MOE_BLK
    kern = functools.partial(_ffn_kernel, blk=MOE_BLK, nblk=nblk, nchunk=4)
    wmap = lambda b, be, tk, ds_: (be[jnp.minimum(b, nblk - 1)], 0, 0)
    return pl.pallas_call(
        kern,
        grid_spec=pltpu.PrefetchScalarGridSpec(
            num_scalar_prefetch=3,
            grid=(nblk + 1,),
            in_specs=[
                pl.BlockSpec(memory_space=pl.ANY),
                pl.BlockSpec((None, d, de), wmap),
                pl.BlockSpec((None, d, de), wmap),
                pl.BlockSpec((None, de, d), wmap),
            ],
            out_specs=pl.BlockSpec(memory_space=pl.ANY),
            scratch_shapes=[pltpu.VMEM((2, MOE_BLK, d), F32), pltpu.VMEM((2, MOE_BLK, d), F32),
                            pltpu.SemaphoreType.DMA(()), pltpu.SemaphoreType.DMA(())],
        ),
        out_shape=jax.ShapeDtypeStruct((nrows_out, d), F32),
        compiler_params=_cp("arbitrary"),
        name="moe_ffn",
    )(blk_exp, tok, dstp, h, wg_bf, wu_bf, wd_bf)


def _combine_kernel(y2_ref, x_ref, gt_ref, g_ref, lng_ref, lnb_ref, *o_refs, alpha, d, ntp):
    gt = gt_ref[...]
    y = gt[:, 0:1] * y2_ref[:, 0:d] + gt[:, 1:2] * y2_ref[:, d:2 * d]
    val = _ln_rows(alpha * x_ref[...] + g_ref[...] * y, lng_ref[...], lnb_ref[...])
    if len(o_refs) == 1:
        o_refs[0][...] = val
    else:
        i = pl.program_id(0)

        @pl.when(i < ntp)
        def _():
            o_refs[0][...] = val

        @pl.when(i >= ntp)
        def _():
            o_refs[1][...] = val


def _combine(y2, x, gates, modl, grp, which_g, ln_g, ln_b, alpha, split):
    t, d = x.shape
    nr = gates.shape[1]
    tm = _tile(math.gcd(grp.tp, grp.ls), 256)
    ntp = grp.tp // tm
    kern = functools.partial(_combine_kernel, alpha=alpha, d=d, ntp=ntp)
    if split:
        out_specs = [pl.BlockSpec((tm, d), lambda i: (jnp.minimum(i, ntp - 1), 0)),
                     pl.BlockSpec((tm, d), lambda i: (jnp.maximum(i - ntp, 0), 0))]
        out_shape = [jax.ShapeDtypeStruct((grp.tp, d), F32), jax.ShapeDtypeStruct((t - grp.tp, d), F32)]
    else:
        out_specs = pl.BlockSpec((tm, d), lambda i: (i, 0))
        out_shape = jax.ShapeDtypeStruct((t, d), F32)
    return pl.pallas_call(
        kern,
        grid=(t // tm,),
        in_specs=[
            pl.BlockSpec((tm, TOP_K * d), lambda i: (i, 0)),
            pl.BlockSpec((tm, d), lambda i: (i, 0)),
            pl.BlockSpec((tm, nr), lambda i: (i, 0)),
            grp.mod_spec(which_g, tm, d, 1),
            pl.BlockSpec((1, d), lambda i: (0, 0)),
            pl.BlockSpec((1, d), lambda i: (0, 0)),
        ],
        out_specs=out_specs,
        out_shape=out_shape,
        compiler_params=_cp("arbitrary"),
        name="moe_combine_ln",
    )(y2, x, gates, modl, ln_g.reshape(1, d), ln_b.reshape(1, d))


def _moe(h, lg, x, modl, grp, ln_g, ln_b, wg_bf, wu_bf, wd_bf, ng, alpha, split):
    t, d = h.shape
    ne = wg_bf.shape[0]
    na = t * TOP_K
    idx, gates, cnt = _route(lg, ng, ne // ng)
    experts = idx[:, 0:TOP_K]
    ranks = idx[:, TOP_K:2 * TOP_K]
    counts = cnt[0, :ne].astype(jnp.int32)
    padded = (counts + MOE_BLK - 1) // MOE_BLK * MOE_BLK
    pends = jnp.cumsum(padded)
    pstarts = pends - padded
    dest = (jnp.take(pstarts, experts) + ranks).reshape(-1).astype(jnp.int32)
    nblk = -(-na // MOE_BLK) + ne
    ns = nblk * MOE_BLK
    blk_start = jnp.arange(nblk, dtype=jnp.int32) * MOE_BLK
    blk_exp = jnp.minimum(jnp.sum(pends[None, :] <= blk_start[:, None], axis=1), ne - 1).astype(jnp.int32)
    code = jnp.full((ns,), -1, jnp.int32).at[dest].set(jnp.arange(na, dtype=jnp.int32), unique_indices=True)
    spare = na + jnp.arange(ns, dtype=jnp.int32) % MOE_BLK
    tok = jnp.where(code >= 0, code // TOP_K, 0)
    dstp = jnp.concatenate([na + jnp.arange(MOE_BLK, dtype=jnp.int32), jnp.where(code >= 0, code, spare)])
    y2 = _ffn(blk_exp, tok, dstp, h, wg_bf, wu_bf, wd_bf, na + MOE_BLK)
    y2 = y2.reshape((na + MOE_BLK) // TOP_K, TOP_K * d)
    return _combine(y2, x, gates, modl, grp, 5, ln_g, ln_b, alpha, split)


def _rope_tables(tp, nb, l):
    rows = l // GRID_W
    row = np.repeat(np.arange(rows, dtype=np.float32), GRID_W)
    col = np.tile(np.arange(GRID_W, dtype=np.float32), rows)
    inv_freq = (np.float32(ROPE_THETA) ** (-np.arange(ROPE_PAIRS, dtype=np.float32) / np.float32(ROPE_PAIRS)))
    ar = row[:, None] * inv_freq
    ac = col[:, None] * inv_freq
    ang = np.concatenate([ar, ar, ac, ac], axis=-1).astype(np.float64)
    cos = np.cos(ang)
    sin = np.sin(ang)
    sign = np.tile(np.concatenate([-np.ones(ROPE_PAIRS), np.ones(ROPE_PAIRS)]), 2)
    cos_all = np.concatenate([np.ones((tp, HEAD_DIM)), np.tile(cos, (nb, 1))], axis=0)
    sin_all = np.concatenate([np.zeros((tp, HEAD_DIM)), np.tile(sin * sign, (nb, 1))], axis=0)
    return jnp.asarray(cos_all.astype(np.float32)), jnp.asarray(sin_all.astype(np.float32))


def kernel(x_prompt, x_sample, cache_k, cache_v, c, c_ctx, w_ada, b_ada, ln_g, ln_b, w_in_ab, q_norm_g, k_norm_g, hy_short_w, hy_short_b, hy_f_w1, hy_f_b1, hy_f_w2, hy_f_b2, hy_f_w3, hy_f_b3, hy_sin_freq, hy_skip, w_out_ab, cv_pw1, cv_dw_w, cv_dw_b, cv_ln_g, cv_ln_b, cv_pw2, moe_w_grp, moe_b_grp, moe_w_exp, moe_b_exp, moe_w_gate, moe_w_up, moe_w_down):
    bp, lp, d = x_prompt.shape
    bs, ls, _ = x_sample.shape
    depth = w_ada.shape[0]
    tp, ts = bp * lp, bs * ls
    t = tp + ts
    kv_w = cache_k.shape[2] * HEAD_DIM
    hy_w = hy_skip.shape[2]
    att_w = d - hy_w
    ng = moe_w_grp.shape[2]
    ne = moe_w_exp.shape[2]
    alpha = (2 * depth) ** 0.25
    nmod = 6
    ngrp = 1 + bs
    gpad = -(-ngrp // SUBLANES) * SUBLANES
    grp = _Groups(tp, ls, nmod)

    x = jnp.concatenate([x_prompt.reshape(tp, d), x_sample.reshape(ts, d)], axis=0)
    cond = jnp.zeros((gpad, d), F32).at[0].set(c_ctx).at[1:ngrp].set(c)
    mod = _ada(cond, w_ada, b_ada)
    nrt = LANES
    new_k, new_v = None, None
    for layer in range(depth):
        i = layer // 2
        modl = mod[layer].reshape(gpad * nmod, 1, d)
        wr = jnp.zeros((d, nrt), F32).at[:, :ng].set(moe_w_grp[layer]).at[:, ng:ng + ne].set(moe_w_exp[layer])
        br = jnp.zeros((1, nrt), F32).at[0, :ng].set(moe_b_grp[layer]).at[0, ng:ng + ne].set(moe_b_exp[layer])
        if layer % 2 == 0:
            zp = _mm_mod(x, t, modl, grp, 1, 0, w_in_ab[i].astype(BF16))
            cos_t, sin_t = _rope_tables(tp, bs, ls)
            q_bf, k_f, k_bf = _qkprep(zp, cos_t, sin_t, q_norm_g[i], k_norm_g[i], att_w, kv_w)
            o_p = _attention(q_bf, k_bf, zp, 0, bp, lp, att_w, kv_w)
            o_s = _attention(q_bf, k_bf, zp, tp, bs, ls, att_w, kv_w, cache_k, cache_v, i)
            ucol = att_w + 2 * kv_w
            ys = []
            for row0, nb, l in ((0, bp, lp), (tp, bs, ls)):
                fc = _fft_consts(l)
                kf = _hyena_filters(l, fc, hy_f_w1[i], hy_f_b1[i], hy_f_w2[i], hy_f_b2[i], hy_f_w3[i], hy_f_b3[i],
                                    hy_sin_freq[i], hy_w)
                usc = _dwconv(zp, ucol, 3 * hy_w, hy_short_w[i], hy_short_b[i], row0, nb, l)
                z1 = _lconv(usc, 0, 0, usc, hy_w, 0, kf[0], hy_skip[i, 0], fc, nb, l, hy_w)
                ys.append(_lconv(z1, 0, 0, usc, 2 * hy_w, 0, kf[1], hy_skip[i, 1], fc, nb, l, hy_w))
            x, h, lg = _out_ln((o_p, o_s), tuple(ys), w_out_ab[i].astype(BF16), x, t, modl, grp,
                               ln_g[layer, 0], ln_b[layer, 0], wr, br, alpha)
            nkv = kv_w // HEAD_DIM
            kk = k_f[:tp].reshape(bp, lp, nkv, HEAD_DIM).transpose(0, 2, 1, 3)
            vv = zp[:tp, att_w + kv_w:att_w + 2 * kv_w].reshape(bp, lp, nkv, HEAD_DIM).transpose(0, 2, 1, 3)
            new_k = kk if new_k is None else jnp.concatenate([new_k, kk], axis=1)
            new_v = vv if new_v is None else jnp.concatenate([new_v, vv], axis=1)
        else:
            a = _mm_mod(x, t, modl, grp, 1, 0, cv_pw1[i].astype(BF16), glu=True)
            a = (_dwconv(a, 0, d, cv_dw_w[i], cv_dw_b[i], 0, bp, lp),
                 _dwconv(a, 0, d, cv_dw_w[i], cv_dw_b[i], tp, bs, ls))
            x, h, lg = _pw2_ln(a, cv_ln_g[i], cv_ln_b[i], cv_pw2[i].astype(BF16), x, t, modl, grp,
                               ln_g[layer, 0], ln_b[layer, 0], wr, br, alpha)
        x = _moe(h, lg, x, modl, grp, ln_g[layer, 1], ln_b[layer, 1], moe_w_gate[layer].astype(BF16),
                 moe_w_up[layer].astype(BF16), moe_w_down[layer].astype(BF16), ng, alpha, layer == depth - 1)
    nkv = kv_w // HEAD_DIM
    n_attn = (depth + 1) // 2
    new_k = new_k.reshape(bp, n_attn, nkv, lp, HEAD_DIM)
    new_v = new_v.reshape(bp, n_attn, nkv, lp, HEAD_DIM)
    return (x[0].reshape(bp, lp, d), x[1].reshape(bs, ls, d), new_k, new_v)
```

```python
import functools
import math

import jax
import jax.numpy as jnp
import numpy as np
from jax import lax
from jax.experimental import pallas as pl
from jax.experimental.pallas import tpu as pltpu

F32 = jnp.float32
BF16 = jnp.bfloat16
HIGHEST = lax.Precision.HIGHEST

GRID_W = 64
HEAD_DIM = 128
ROPE_THETA = 10000.0
ROPE_PAIRS = HEAD_DIM // 4
HY_BANDS = 16
HY_TARGET = 1e-2
HY_FAST = 0.3
HY_SLOW = 1.5
TOP_K = 2
EPS = 1e-6

LANES = 128
SUBLANES = 8
VMEM_LIMIT = 56 * 1024 * 1024
MOE_BLK = 256
FFT_SPLIT = 64
FFT_DENSE_MAX = 1024
FFT_UNROLL = 4
ATTN_PARTS = 2


def _cp(*sem):
    return pltpu.CompilerParams(dimension_semantics=sem, vmem_limit_bytes=VMEM_LIMIT)


def _tile(n, target, mult=SUBLANES):
    if n <= target:
        return n
    for t in range(target, 0, -1):
        if n % t == 0 and t % mult == 0:
            return t
    return n


def _silu(x):
    return x * jax.nn.sigmoid(x)


def _ln_rows(x, g, b):
    mu = jnp.mean(x, axis=-1, keepdims=True)
    xc = x - mu
    var = jnp.mean(xc * xc, axis=-1, keepdims=True)
    return xc * lax.rsqrt(var + EPS) * g + b


def _ada_kernel(c_ref, w_ref, b_ref, o_ref):
    s = _silu(c_ref[...])
    o_ref[...] = jnp.dot(s, w_ref[...], precision=HIGHEST, preferred_element_type=F32) + b_ref[...]


def _ada(cond, w_ada, b_ada):
    depth, d, n = w_ada.shape
    g = cond.shape[0]
    tn = _tile(n, 1024, LANES)
    return pl.pallas_call(
        _ada_kernel,
        grid=(depth, n // tn),
        in_specs=[
            pl.BlockSpec((g, d), lambda l, j: (0, 0)),
            pl.BlockSpec((None, d, tn), lambda l, j: (l, 0, j)),
            pl.BlockSpec((None, 1, tn), lambda l, j: (l, 0, j)),
        ],
        out_specs=pl.BlockSpec((None, g, tn), lambda l, j: (l, 0, j)),
        out_shape=jax.ShapeDtypeStruct((depth, g, n), F32),
        compiler_params=_cp("arbitrary", "arbitrary"),
        name="ada_mod",
    )(cond, w_ada, b_ada.reshape(depth, 1, n))


def _load_rows(refs, ntp):
    if len(refs) == 1:
        return refs[0][...]
    return jnp.where(pl.program_id(0) < ntp, refs[0][...], refs[1][...])


def _row_operand(arr, tm, cols, ntp, nargs):
    if isinstance(arr, tuple):
        if nargs == 2:
            maps = (lambda i, j: (jnp.minimum(i, ntp - 1), 0), lambda i, j: (jnp.maximum(i - ntp, 0), 0))
        else:
            maps = (lambda i: (jnp.minimum(i, ntp - 1), 0), lambda i: (jnp.maximum(i - ntp, 0), 0))
        return list(arr), [pl.BlockSpec((tm, cols), m) for m in maps]
    one = (lambda i, j: (i, 0)) if nargs == 2 else (lambda i: (i, 0))
    return [arr], [pl.BlockSpec((tm, cols), one)]


def _mm_mod_kernel(*refs, nx, ntp):
    sc_ref, sh_ref, w_ref, o_ref, h_scr = refs[nx:]

    @pl.when(pl.program_id(1) == 0)
    def _():
        h_scr[...] = (_load_rows(refs[:nx], ntp) * (1.0 + sc_ref[...]) + sh_ref[...]).astype(BF16)

    o_ref[...] = jnp.dot(h_scr[...], w_ref[...], preferred_element_type=F32)


def _mm_glu_kernel(*refs, nx, ntp):
    sc_ref, sh_ref, wa_ref, wg_ref, o_ref, h_scr = refs[nx:]

    @pl.when(pl.program_id(1) == 0)
    def _():
        h_scr[...] = (_load_rows(refs[:nx], ntp) * (1.0 + sc_ref[...]) + sh_ref[...]).astype(BF16)

    a = jnp.dot(h_scr[...], wa_ref[...], preferred_element_type=F32)
    g = jnp.dot(h_scr[...], wg_ref[...], preferred_element_type=F32)
    o_ref[...] = a * jax.nn.sigmoid(g)


class _Groups:
    def __init__(self, tp, ls, nmod):
        self.tp, self.ls, self.nmod = tp, ls, nmod

    def gid(self, i, tm):
        row = i * tm
        return jnp.where(row < self.tp, 0, 1 + (row - self.tp) // self.ls)

    def mod_spec(self, which, tm, d, nargs=2):
        if nargs == 2:
            return pl.BlockSpec((None, 1, d), lambda i, j: (self.gid(i, tm) * self.nmod + which, 0, 0))
        return pl.BlockSpec((None, 1, d), lambda i: (self.gid(i, tm) * self.nmod + which, 0, 0))


def _mm_mod(x, t, modl, grp, which_sc, which_sh, w_bf, glu=False):
    d = w_bf.shape[0]
    n = w_bf.shape[1] // (2 if glu else 1)
    tm = _tile(math.gcd(grp.tp, grp.ls), 1024)
    tn = _tile(n, 512, LANES)
    ntp = grp.tp // tm
    args, specs = _row_operand(x, tm, d, ntp, 2)
    nx = len(args)
    specs += [
        grp.mod_spec(which_sc, tm, d),
        grp.mod_spec(which_sh, tm, d),
        pl.BlockSpec((d, tn), lambda i, j: (0, j)),
    ]
    args += [modl, modl, w_bf]
    if glu:
        noff = n // tn
        specs.append(pl.BlockSpec((d, tn), lambda i, j: (0, j + noff)))
        args.append(w_bf)
    return pl.pallas_call(
        functools.partial(_mm_glu_kernel if glu else _mm_mod_kernel, nx=nx, ntp=ntp),
        grid=(t // tm, n // tn),
        in_specs=specs,
        out_specs=pl.BlockSpec((tm, tn), lambda i, j: (i, j)),
        out_shape=jax.ShapeDtypeStruct((t, n), F32),
        scratch_shapes=[pltpu.VMEM((tm, d), BF16)],
        compiler_params=_cp("arbitrary", "arbitrary"),
        name="mm_glu" if glu else "mm_mod",
    )(*args)


def _qkprep_kernel(q_ref, k_ref, c_ref, s_ref, qg_ref, kg_ref, qo_ref, ko_ref, kb_ref, *, nq, nk, scale):
    c = c_ref[...]
    s = s_ref[...]
    lane = lax.broadcasted_iota(jnp.int32, c.shape, 1)
    first = (lane % (2 * ROPE_PAIRS)) < ROPE_PAIRS

    def prep(xh, g):
        ms = jnp.mean(xh * xh, axis=-1, keepdims=True)
        xn = xh * lax.rsqrt(ms + EPS) * g
        rot = jnp.where(first, pltpu.roll(xn, HEAD_DIM - ROPE_PAIRS, 1), pltpu.roll(xn, ROPE_PAIRS, 1))
        return xn * c + rot * s

    for h in range(nq):
        sl = slice(h * HEAD_DIM, (h + 1) * HEAD_DIM)
        qo_ref[:, sl] = (prep(q_ref[:, sl], qg_ref[...]) * scale).astype(BF16)
    for h in range(nk):
        sl = slice(h * HEAD_DIM, (h + 1) * HEAD_DIM)
        kh = prep(k_ref[:, sl], kg_ref[...])
        ko_ref[:, sl] = kh
        kb_ref[:, sl] = kh.astype(BF16)


def _qkprep(zp, cos_t, sin_t, q_g, k_g, att_w, kv_w):
    t = zp.shape[0]
    tm = _tile(t, 512)
    nq, nk = att_w // HEAD_DIM, kv_w // HEAD_DIM
    assert att_w % kv_w == 0
    kern = functools.partial(_qkprep_kernel, nq=nq, nk=nk, scale=HEAD_DIM ** -0.5)
    return pl.pallas_call(
        kern,
        grid=(t // tm,),
        in_specs=[
            pl.BlockSpec((tm, att_w), lambda i: (i, 0)),
            pl.BlockSpec((tm, kv_w), lambda i: (i, att_w // kv_w)),
            pl.BlockSpec((tm, HEAD_DIM), lambda i: (i, 0)),
            pl.BlockSpec((tm, HEAD_DIM), lambda i: (i, 0)),
            pl.BlockSpec((1, HEAD_DIM), lambda i: (0, 0)),
            pl.BlockSpec((1, HEAD_DIM), lambda i: (0, 0)),
        ],
        out_specs=[
            pl.BlockSpec((tm, att_w), lambda i: (i, 0)),
            pl.BlockSpec((tm, kv_w), lambda i: (i, 0)),
            pl.BlockSpec((tm, kv_w), lambda i: (i, 0)),
        ],
        out_shape=[
            jax.ShapeDtypeStruct((t, att_w), BF16),
            jax.ShapeDtypeStruct((t, kv_w), F32),
            jax.ShapeDtypeStruct((t, kv_w), BF16),
        ],
        compiler_params=_cp("arbitrary"),
        name="qk_prep",
    )(zp, zp, cos_t, sin_t, q_g.reshape(1, HEAD_DIM), k_g.reshape(1, HEAD_DIM))


def _attn_kernel(*refs, has_cache, g, tq):
    if has_cache:
        q_ref, k_ref, v_ref, kc_ref, vc_ref, o_ref = refs
    else:
        q_ref, k_ref, v_ref, o_ref = refs
    nt = (((1,), (1,)), ((), ()))
    nparts = ATTN_PARTS if g % ATTN_PARTS == 0 else 1
    hp = g // nparts
    for part in range(nparts):
        heads = range(part * hp, (part + 1) * hp)
        qs = jnp.concatenate([q_ref[:, h * HEAD_DIM:(h + 1) * HEAD_DIM] for h in heads], axis=0)
        s = lax.dot_general(qs, k_ref[...], nt, preferred_element_type=F32)
        m = jnp.max(s, axis=-1, keepdims=True)
        if has_cache:
            sc = lax.dot_general(qs, kc_ref[...].astype(BF16), nt, preferred_element_type=F32)
            m = jnp.maximum(m, jnp.max(sc, axis=-1, keepdims=True))
        p = jnp.exp(s - m)
        l = jnp.sum(p, axis=-1, keepdims=True)
        o = jnp.dot(p.astype(BF16), v_ref[...].astype(BF16), preferred_element_type=F32)
        if has_cache:
            pc = jnp.exp(sc - m)
            l = l + jnp.sum(pc, axis=-1, keepdims=True)
            o = o + jnp.dot(pc.astype(BF16), vc_ref[...].astype(BF16), preferred_element_type=F32)
        o = o / l
        for i, h in enumerate(heads):
            o_ref[:, h * HEAD_DIM:(h + 1) * HEAD_DIM] = o[i * tq:(i + 1) * tq].astype(BF16)


def _attention(q_bf, k_bf, zp, row0, nb, l, att_w, kv_w, cache_k=None, cache_v=None, layer=0):
    nkv = kv_w // HEAD_DIM
    g = att_w // kv_w
    tq = _tile(l, max(SUBLANES, 512 // g))
    rb = row0 // l
    assert row0 % l == 0
    vcol = (att_w + kv_w) // HEAD_DIM
    has_cache = cache_k is not None
    specs = [
        pl.BlockSpec((tq, g * HEAD_DIM), lambda b, h, i: ((row0 + b * l) // tq + i, h)),
        pl.BlockSpec((l, HEAD_DIM), lambda b, h, i: (rb + b, h)),
        pl.BlockSpec((l, HEAD_DIM), lambda b, h, i: (rb + b, vcol + h)),
    ]
    args = [q_bf, k_bf, zp]
    if has_cache:
        past = cache_k.shape[3]
        cspec = pl.BlockSpec((None, None, None, past, HEAD_DIM), lambda b, h, i: (b, layer, h, 0, 0))
        specs += [cspec, cspec]
        args += [cache_k, cache_v]
    kern = functools.partial(_attn_kernel, has_cache=has_cache, g=g, tq=tq)
    return pl.pallas_call(
        kern,
        grid=(nb, nkv, l // tq),
        in_specs=specs,
        out_specs=pl.BlockSpec((tq, g * HEAD_DIM), lambda b, h, i: (b * (l // tq) + i, h)),
        out_shape=jax.ShapeDtypeStruct((nb * l, att_w), BF16),
        compiler_params=_cp("arbitrary", "arbitrary", "arbitrary"),
        name="attn_cache" if has_cache else "attn",
    )(*args)


def _dwconv_kernel(x_ref, w_ref, b_ref, o_ref, pad, *, l, width, rc, off):
    ct = x_ref.shape[1]
    zeros = jnp.zeros((off, ct), F32)
    pad[0:off, :] = zeros
    pad[off + l:off + l + off, :] = zeros
    pad[off:off + l, :] = x_ref[...]
    bias = b_ref[...]
    shift = off - width // 2

    def chunk(ci, carry):
        base = pl.multiple_of(ci * rc, rc)
        win = pad[pl.ds(base, rc + 2 * off), :]
        acc = jnp.zeros((rc, ct), F32) + bias
        span = rc + 2 * off - SUBLANES
        for p in range(SUBLANES):
            taps = [k for k in range(width) if (k + shift) % SUBLANES == p]
            if taps:
                shifted = win[p:p + span, :]
                for k in taps:
                    o = k + shift - p
                    wk = jnp.concatenate([w_ref[SUBLANES * k:SUBLANES * (k + 1), :]] * (rc // SUBLANES), axis=0)
                    acc = acc + wk * shifted[o:o + rc, :]
        o_ref[pl.ds(base, rc), :] = acc
        return carry

    lax.fori_loop(0, l // rc, chunk, 0)


def _dwconv(x, col0, ncols, w, b, row0, nb, l):
    width = w.shape[0]
    off = 16
    assert width // 2 <= off
    ct = LANES if width > SUBLANES else _tile(ncols, 2 * LANES, LANES)
    rc = _tile(l, 64)
    assert rc % SUBLANES == 0
    wp = jnp.broadcast_to(w[:, None, :], (width, SUBLANES, ncols)).reshape(width * SUBLANES, ncols)
    assert col0 % ct == 0 and row0 % l == 0
    kern = functools.partial(_dwconv_kernel, l=l, width=width, rc=rc, off=off)
    return pl.pallas_call(
        kern,
        grid=(nb, ncols // ct),
        in_specs=[
            pl.BlockSpec((l, ct), lambda b_, j: (row0 // l + b_, col0 // ct + j)),
            pl.BlockSpec((width * SUBLANES, ct), lambda b_, j: (0, j)),
            pl.BlockSpec((1, ct), lambda b_, j: (0, j)),
        ],
        out_specs=pl.BlockSpec((l, ct), lambda b_, j: (b_, j)),
        out_shape=jax.ShapeDtypeStruct((nb * l, ncols), F32),
        scratch_shapes=[pltpu.VMEM((l + 2 * off, ct), F32)],
        compiler_params=_cp("arbitrary", "arbitrary"),
        name="dwconv%d" % width,
    )(x, wp, b.reshape(1, ncols))


def _fft_plan(l):
    n = 2 * l
    if n <= FFT_DENSE_MAX or n % FFT_SPLIT != 0:
        return n, 1
    return n // FFT_SPLIT, FFT_SPLIT


def _stack3_lhs(m):
    hi = m.astype(np.float32).astype(jnp.bfloat16)
    lo = (m - np.asarray(hi.astype(np.float32), np.float64)).astype(np.float32).astype(jnp.bfloat16)
    return jnp.asarray(np.concatenate([hi, lo, hi], axis=1))


def _split3(x):
    hi = x.astype(BF16)
    lo = (x - hi.astype(F32)).astype(BF16)
    return jnp.concatenate([hi, hi, lo], axis=0)


def _mm3(lhs3, x):
    return jnp.dot(lhs3, _split3(x), preferred_element_type=F32)


def _fft_consts(l):
    n1, n2 = _fft_plan(l)
    n = n1 * n2
    n1h = n1 // 2
    k1 = np.arange(n1, dtype=np.float64)[:, None]
    ang = 2.0 * np.pi * k1 * np.arange(n1h, dtype=np.float64)[None, :] / n1
    fr, fi = np.cos(ang), -np.sin(ang)
    a_fwd = np.block([[fr, -fi], [fi, fr]])
    ang = 2.0 * np.pi * k1 * np.arange(n1, dtype=np.float64)[None, :] / n1
    a_real = np.concatenate([np.cos(ang), -np.sin(ang)], axis=0)
    angi = 2.0 * np.pi * np.arange(n1h, dtype=np.float64)[:, None] * np.arange(n1, dtype=np.float64)[None, :] / n1
    cr, ci = np.cos(angi) / n, np.sin(angi) / n
    a_inv = np.block([[cr, -ci], [ci, cr]])
    k2 = np.arange(n2, dtype=np.float64)
    angb = 2.0 * np.pi * k2[:, None] * k2[None, :] / n2
    gr, gi = np.cos(angb), -np.sin(angb)
    b_fwd = np.block([[gr, -gi], [gi, gr]])
    b_inv = np.block([[gr, gi], [-gi, gr]])
    angt = 2.0 * np.pi * (np.arange(n1, dtype=np.float64)[:, None] * k2[None, :]).reshape(n, 1) / n
    twr = np.broadcast_to(np.cos(angt), (n, LANES)).astype(np.float32)
    twi = np.broadcast_to(-np.sin(angt), (n, LANES)).astype(np.float32)
    return dict(n1=n1, n2=n2, a_fwd=_stack3_lhs(a_fwd), a_real=_stack3_lhs(a_real), a_inv=_stack3_lhs(a_inv),
                b_fwd=_stack3_lhs(b_fwd), b_inv=_stack3_lhs(b_inv), twr=jnp.asarray(twr), twi=jnp.asarray(twi))


def _rows(ref, start, size, stride):
    if stride == 1:
        return ref[pl.ds(start, size), :]
    return ref[pl.ds(start, size, stride=stride), :]


def _stage_b_fwd(pr, pi, twr_ref, twi_ref, bfwd_ref, k1, n1, n2):
    xr = _rows(pr, k1, n2, n1)
    xi = _rows(pi, k1, n2, n1)
    r0 = pl.multiple_of(k1 * n2, n2)
    cr = twr_ref[pl.ds(r0, n2), :]
    ci = twi_ref[pl.ds(r0, n2), :]
    ar = xr * cr - xi * ci
    ai = xr * ci + xi * cr
    x = _mm3(bfwd_ref[...], jnp.concatenate([ar, ai], axis=0))
    return x[:n2], x[n2:], cr, ci


def _lconv_kernel(a_ref, g_ref, kf_ref, skip_ref, afwd_ref, ainv_ref, bfwd_ref, binv_ref, twr_ref, twi_ref,
                  o_ref, pr, pi, qr, qi, *, l, n1, n2):
    n1h = n1 // 2

    def stage_a(i2, carry):
        slab = jnp.concatenate([_rows(a_ref, i2, n1h, n2), _rows(a_ref, l + i2, n1h, n2)], axis=0)
        out = _mm3(afwd_ref[...], slab)
        r0 = pl.multiple_of(i2 * n1, n1)
        pr[pl.ds(r0, n1), :] = out[:n1]
        pi[pl.ds(r0, n1), :] = out[n1:]
        return carry

    lax.fori_loop(0, n2, stage_a, 0, unroll=min(n2, FFT_UNROLL))

    if n2 == 1:
        kr, ki = kf_ref[0], kf_ref[1]
        xr, xi = pr[...], pi[...]
        qr[...] = xr * kr - xi * ki
        qi[...] = xr * ki + xi * kr
    else:
        def stage_b(k1, carry):
            xr, xi, cr, ci = _stage_b_fwd(pr, pi, twr_ref, twi_ref, bfwd_ref, k1, n1, n2)
            r0 = pl.multiple_of(k1 * n2, n2)
            kr = kf_ref[0, pl.ds(r0, n2), :]
            ki = kf_ref[1, pl.ds(r0, n2), :]
            yr = xr * kr - xi * ki
            yi = xr * ki + xi * kr
            bv = _mm3(binv_ref[...], jnp.concatenate([yr, yi], axis=0))
            br, bi = bv[:n2], bv[n2:]
            qr[pl.ds(r0, n2), :] = br * cr + bi * ci
            qi[pl.ds(r0, n2), :] = bi * cr - br * ci
            return carry

        lax.fori_loop(0, n1, stage_b, 0, unroll=FFT_UNROLL)

    skip = skip_ref[...]

    def stage_c(i2, carry):
        slab = jnp.concatenate([_rows(qr, i2, n1, n2), _rows(qi, i2, n1, n2)], axis=0)
        y = _mm3(ainv_ref[...], slab)
        for half in range(2):
            start = half * l + i2
            a = _rows(a_ref, start, n1h, n2)
            g = _rows(g_ref, start, n1h, n2)
            val = g * (y[half * n1h:(half + 1) * n1h] + skip * a)
            if n2 == 1:
                o_ref[pl.ds(start, n1h), :] = val
            else:
                o_ref[pl.ds(start, n1h, stride=n2), :] = val
        return carry

    lax.fori_loop(0, n2, stage_c, 0, unroll=min(n2, FFT_UNROLL))


def _const_spec(shape):
    zeros = (0,) * len(shape)
    return pl.BlockSpec(shape, lambda a, b: zeros, pipeline_mode=pl.Buffered(1))


def _lconv(a, a_col0, a_row0, gate, g_col0, g_row0, kf, order, skip, fc, nb, l, c):
    n1, n2 = fc["n1"], fc["n2"]
    n = n1 * n2
    ct = LANES
    assert nb % 2 == 0 and a_row0 % (2 * l) == 0 and g_row0 % (2 * l) == 0
    single = pl.Buffered(1) if n2 > 1 else None
    kern = functools.partial(_lconv_kernel, l=l, n1=n1, n2=n2)
    return pl.pallas_call(
        kern,
        grid=(c // ct, nb // 2),
        in_specs=[
            pl.BlockSpec((2 * l, ct), lambda j, p: (a_row0 // (2 * l) + p, a_col0 // ct + j), pipeline_mode=single),
            pl.BlockSpec((2 * l, ct), lambda j, p: (g_row0 // (2 * l) + p, g_col0 // ct + j), pipeline_mode=single),
            pl.BlockSpec((None, 2, n, ct), lambda j, p: (order, 0, 0, j), pipeline_mode=single),
            pl.BlockSpec((1, ct), lambda j, p: (0, j)),
            _const_spec(fc["a_fwd"].shape), _const_spec(fc["a_inv"].shape),
            _const_spec(fc["b_fwd"].shape), _const_spec(fc["b_inv"].shape),
            _const_spec((n, LANES)), _const_spec((n, LANES)),
        ],
        out_specs=pl.BlockSpec((2 * l, ct), lambda j, p: (p, j), pipeline_mode=single),
        out_shape=jax.ShapeDtypeStruct((nb * l, c), F32),
        scratch_shapes=[pltpu.VMEM((n, ct), F32) for _ in range(4)],
        compiler_params=_cp("arbitrary", "arbitrary"),
        name="hyena_lconv",
    )(a, gate, kf, skip.reshape(1, c), fc["a_fwd"], fc["a_inv"], fc["b_fwd"], fc["b_inv"], fc["twr"], fc["twi"])


def _hyfilt_kernel(zz_ref, tt_ref, w1_ref, b1_ref, w2_ref, b2_ref, sf_ref, w3f_ref, w3b_ref, b3f_ref, b3b_ref,
                   dl_ref, areal_ref, bfwd_ref, twr_ref, twi_ref, o_ref, h_scr, kern_scr, pr, pi, *, l, n1, n2):
    rc = _tile(l, 256)
    nch = l // rc

    @pl.when((pl.program_id(0) == 0) & (pl.program_id(1) == 0))
    def _():
        def mlp(ci, carry):
            r0 = pl.multiple_of(ci * rc, rc)
            h = jnp.dot(zz_ref[pl.ds(r0, rc), :], w1_ref[...], precision=HIGHEST, preferred_element_type=F32)
            h = jnp.sin(sf_ref[0:1, :] * (h + b1_ref[...]))
            h = jnp.dot(h, w2_ref[...], precision=HIGHEST, preferred_element_type=F32) + b2_ref[...]
            h_scr[pl.ds(r0, rc), :] = jnp.sin(sf_ref[1:2, :] * h)
            return carry

        lax.fori_loop(0, 2 * nch, mlp, 0)

    def filt(w3_ref, b3_ref, first_chunk):
        def body(ci, acc):
            r0 = pl.multiple_of((first_chunk + ci) * rc, rc)
            hk = jnp.dot(h_scr[pl.ds(r0, rc), :], w3_ref[...], precision=HIGHEST, preferred_element_type=F32)
            win = jnp.exp(-tt_ref[pl.ds(r0, rc), :] * dl_ref[...])
            row = r0 + lax.broadcasted_iota(jnp.int32, (rc, 1), 0)
            k = jnp.where(row == l, 0.0, (hk + b3_ref[...]) * win)
            kern_scr[pl.ds(r0, rc), :] = k
            return acc + jnp.sum(jnp.abs(k), axis=0, keepdims=True)
        return body

    zero = jnp.zeros((1, kern_scr.shape[1]), F32)
    total = lax.fori_loop(0, nch, filt(w3f_ref, b3f_ref, 0), zero)
    total = lax.fori_loop(0, nch, filt(w3b_ref, b3b_ref, nch), total)
    inv = 1.0 / total

    def stage_a(i2, carry):
        out = _mm3(areal_ref[...], _rows(kern_scr, i2, n1, n2) * inv)
        r0 = pl.multiple_of(i2 * n1, n1)
        pr[pl.ds(r0, n1), :] = out[:n1]
        pi[pl.ds(r0, n1), :] = out[n1:]
        return carry

    lax.fori_loop(0, n2, stage_a, 0, unroll=min(n2, FFT_UNROLL))

    if n2 == 1:
        o_ref[0] = pr[...]
        o_ref[1] = pi[...]
    else:
        def stage_b(k1, carry):
            xr, xi, _, _ = _stage_b_fwd(pr, pi, twr_ref, twi_ref, bfwd_ref, k1, n1, n2)
            r0 = pl.multiple_of(k1 * n2, n2)
            o_ref[0, pl.ds(r0, n2), :] = xr
            o_ref[1, pl.ds(r0, n2), :] = xi
            return carry

        lax.fori_loop(0, n1, stage_b, 0, unroll=FFT_UNROLL)


def _hyena_filters(l, fc, w1, b1, w2, b2, w3, b3, sin_freq, c):
    n1, n2 = fc["n1"], fc["n2"]
    n = 2 * l
    order = w3.shape[1] // (2 * c)
    hid = w1.shape[1]
    ct = LANES
    pos = np.concatenate([np.arange(l), [0], np.arange(l - 1, 0, -1)]).astype(np.float32)
    t01 = np.linspace(0.0, 1.0, l, dtype=np.float32)[pos.astype(np.int64)]
    bands = np.linspace(1e-4, HY_BANDS - 1, HY_BANDS, dtype=np.float32)
    ang = (np.float32(2.0 * math.pi) * pos / np.float32(l))[:, None] * bands
    emb = 1 + 2 * HY_BANDS
    feat = np.zeros((n, hid), np.float32)
    feat[:, :emb] = np.concatenate([t01[:, None], np.cos(ang), -np.sin(ang)], axis=-1)
    w1p = jnp.zeros((hid, hid), F32).at[:emb].set(w1)
    tt = jnp.asarray(np.broadcast_to(t01[:, None], (n, LANES)).copy())
    dmin = math.log(HY_TARGET) / HY_SLOW
    dmax = math.log(HY_TARGET) / HY_FAST
    deltas = jnp.abs(jnp.linspace(dmin, dmax, c, dtype=F32)).reshape(1, c)
    cb = c // ct
    kern = functools.partial(_hyfilt_kernel, l=l, n1=n1, n2=n2)
    return pl.pallas_call(
        kern,
        grid=(order, cb),
        in_specs=[
            _const_spec((n, hid)), _const_spec((n, LANES)),
            _const_spec((hid, hid)), _const_spec((1, hid)), _const_spec((hid, hid)), _const_spec((1, hid)),
            _const_spec((2, hid)),
            pl.BlockSpec((hid, ct), lambda o, j: (0, (2 * o) * cb + j)),
            pl.BlockSpec((hid, ct), lambda o, j: (0, (2 * o + 1) * cb + j)),
            pl.BlockSpec((1, ct), lambda o, j: (0, (2 * o) * cb + j)),
            pl.BlockSpec((1, ct), lambda o, j: (0, (2 * o + 1) * cb + j)),
            pl.BlockSpec((1, ct), lambda o, j: (0, j)),
            _const_spec(fc["a_real"].shape), _const_spec(fc["b_fwd"].shape),
            _const_spec((n, LANES)), _const_spec((n, LANES)),
        ],
        out_specs=pl.BlockSpec((None, 2, n, ct), lambda o, j: (o, 0, 0, j), pipeline_mode=pl.Buffered(1)),
        out_shape=jax.ShapeDtypeStruct((order, 2, n, c), F32),
        scratch_shapes=[pltpu.VMEM((n, hid), F32), pltpu.VMEM((n, ct), F32), pltpu.VMEM((n, ct), F32),
                        pltpu.VMEM((n, ct), F32)],
        compiler_params=_cp("arbitrary", "arbitrary"),
        name="hyena_filters",
    )(jnp.asarray(feat), tt, w1p, b1.reshape(1, hid), w2, b2.reshape(1, hid), sin_freq, w3, w3,
      b3.reshape(1, -1), b3.reshape(1, -1), deltas, fc["a_real"], fc["b_fwd"], fc["twr"], fc["twi"])


def _res_ln_epilogue(acc, x, g_ref, lng_ref, lnb_ref, sc_ref, sh_ref, wr_ref, br_ref, xo_ref, h_ref, lg_ref, alpha):
    xn = _ln_rows(alpha * x + g_ref[...] * acc, lng_ref[...], lnb_ref[...])
    xo_ref[...] = xn
    h = xn * (1.0 + sc_ref[...]) + sh_ref[...]
    h_ref[...] = h
    d = h.shape[1]
    hh = h.astype(BF16)
    hl = (h - hh.astype(F32)).astype(BF16)
    lg = jnp.dot(hh, wr_ref[0:d, :], preferred_element_type=F32)
    lg = lg + jnp.dot(hh, wr_ref[d:2 * d, :], preferred_element_type=F32)
    lg = lg + jnp.dot(hl, wr_ref[0:d, :], preferred_element_type=F32)
    lg_ref[...] = lg + br_ref[...]


def _out_ln_kernel(*refs, no, ny, nx, ntp, alpha):
    o = _load_rows(refs[:no], ntp)
    y = _load_rows(refs[no:no + ny], ntp)
    w1_ref, w2_ref = refs[no + ny:no + ny + 2]
    rest = refs[no + ny + 2:]
    acc = jnp.dot(o, w1_ref[...], preferred_element_type=F32)
    acc = acc + jnp.dot(y.astype(BF16), w2_ref[...], preferred_element_type=F32)
    _res_ln_epilogue(acc, _load_rows(rest[:nx], ntp), *rest[nx:], alpha)


def _pw2_ln_kernel(*refs, na, nx, ntp, alpha):
    cg_ref, cb_ref, w_ref = refs[na:na + 3]
    rest = refs[na + 3:]
    a = _silu(_ln_rows(_load_rows(refs[:na], ntp), cg_ref[...], cb_ref[...])).astype(BF16)
    acc = jnp.dot(a, w_ref[...], preferred_element_type=F32)
    _res_ln_epilogue(acc, _load_rows(rest[:nx], ntp), *rest[nx:], alpha)


def _res_ln_call(kern, name, lead_args, lead_specs, x, t, modl, grp, which_g, which_sc, which_sh, ln_g, ln_b, wr, br,
                 tm):
    d, nr = wr.shape[0] // 2, wr.shape[1]
    row = lambda i: (i, 0)
    const = lambda i: (0, 0)
    xargs, xspecs = _row_operand(x, tm, d, grp.tp // tm, 1)
    specs = lead_specs + xspecs + [
        grp.mod_spec(which_g, tm, d, 1),
        pl.BlockSpec((1, d), const), pl.BlockSpec((1, d), const),
        grp.mod_spec(which_sc, tm, d, 1), grp.mod_spec(which_sh, tm, d, 1),
        pl.BlockSpec((2 * d, nr), const, pipeline_mode=pl.Buffered(1)), pl.BlockSpec((1, nr), const),
    ]
    return pl.pallas_call(
        functools.partial(kern, nx=len(xargs)),
        grid=(t // tm,),
        in_specs=specs,
        out_specs=[pl.BlockSpec((tm, d), row), pl.BlockSpec((tm, d), row), pl.BlockSpec((tm, nr), row)],
        out_shape=[jax.ShapeDtypeStruct((t, d), F32), jax.ShapeDtypeStruct((t, d), F32),
                   jax.ShapeDtypeStruct((t, nr), F32)],
        compiler_params=_cp("arbitrary"),
        name=name,
    )(*lead_args, *xargs, modl, ln_g.reshape(1, d), ln_b.reshape(1, d), modl, modl, wr, br)


def _out_ln(o_bf, y, w_out_bf, x, t, modl, grp, ln_g, ln_b, wr, br, alpha):
    d = w_out_bf.shape[1]
    ka = d // 2
    tm = _tile(math.gcd(grp.tp, grp.ls), 512)
    ntp = grp.tp // tm
    single = pl.Buffered(1)
    oargs, ospecs = _row_operand(o_bf, tm, ka, ntp, 1)
    yargs, yspecs = _row_operand(y, tm, ka, ntp, 1)
    lead_specs = ospecs + yspecs + [
        pl.BlockSpec((ka, d), lambda i: (0, 0), pipeline_mode=single),
        pl.BlockSpec((ka, d), lambda i: (1, 0), pipeline_mode=single),
    ]
    kern = functools.partial(_out_ln_kernel, no=len(oargs), ny=len(yargs), ntp=ntp, alpha=alpha)
    return _res_ln_call(kern, "out_proj_ln", oargs + yargs + [w_out_bf, w_out_bf], lead_specs, x, t, modl, grp,
                        2, 4, 3, ln_g, ln_b, wr, br, tm)


def _pw2_ln(a, cv_g, cv_b, w_bf, x, t, modl, grp, ln_g, ln_b, wr, br, alpha):
    d = w_bf.shape[0]
    tm = _tile(math.gcd(grp.tp, grp.ls), 512)
    ntp = grp.tp // tm
    aargs, aspecs = _row_operand(a, tm, d, ntp, 1)
    lead_specs = aspecs + [
        pl.BlockSpec((1, d), lambda i: (0, 0)), pl.BlockSpec((1, d), lambda i: (0, 0)),
        pl.BlockSpec((d, d), lambda i: (0, 0), pipeline_mode=pl.Buffered(1)),
    ]
    kern = functools.partial(_pw2_ln_kernel, na=len(aargs), ntp=ntp, alpha=alpha)
    return _res_ln_call(kern, "conv_proj_ln", aargs + [cv_g.reshape(1, d), cv_b.reshape(1, d), w_bf], lead_specs,
                        x, t, modl, grp, 2, 4, 3, ln_g, ln_b, wr, br, tm)


def _route_kernel(lg_ref, tri_ref, idx_ref, gate_ref, cnt_ref, carry, *, ng, epg):
    @pl.when(pl.program_id(0) == 0)
    def _():
        carry[...] = jnp.zeros_like(carry)

    lg = lg_ref[...]
    lane = lax.broadcasted_iota(jnp.int32, lg.shape, 1)
    neg = -jnp.inf
    big = lg.shape[1]
    gmask = lane < ng
    lgm = jnp.where(gmask, lg, neg)
    mg = jnp.max(lgm, axis=-1, keepdims=True)
    g = jnp.min(jnp.where(lgm == mg, lane, big), axis=-1, keepdims=True)
    gate_g = 1.0 / jnp.sum(jnp.exp(lgm - mg), axis=-1, keepdims=True)
    el = lane - ng
    lo = g * epg
    emask = (el >= lo) & (el < lo + epg)
    le = jnp.where(emask, lg, neg)
    m1 = jnp.max(le, axis=-1, keepdims=True)
    i1 = jnp.min(jnp.where(le == m1, lane, big), axis=-1, keepdims=True)
    le2 = jnp.where(lane == i1, neg, le)
    m2 = jnp.max(le2, axis=-1, keepdims=True)
    i2 = jnp.min(jnp.where(le2 == m2, lane, big), axis=-1, keepdims=True)
    r = jnp.exp(m2 - m1)
    den = 1.0 + r
    g1 = gate_g / den
    g2 = gate_g * r / den
    e1 = i1 - ng
    e2 = i2 - ng
    oh1 = lane == e1
    oh2 = lane == e2
    oh1f = jnp.where(oh1, 1.0, 0.0)
    oh2f = jnp.where(oh2, 1.0, 0.0)
    p1 = jnp.dot(tri_ref[...], oh1f.astype(BF16), preferred_element_type=F32)
    p2 = jnp.dot(tri_ref[...], oh2f.astype(BF16), preferred_element_type=F32)
    c = carry[0:1, :]
    tot1 = jnp.sum(oh1f, axis=0, keepdims=True)
    tot2 = jnp.sum(oh2f, axis=0, keepdims=True)
    r1 = jnp.sum(jnp.where(oh1, p1 + c, 0.0), axis=-1, keepdims=True)
    r2 = jnp.sum(jnp.where(oh2, p2 + c + tot1, 0.0), axis=-1, keepdims=True)
    newc = c + tot1 + tot2
    carry[...] = jnp.broadcast_to(newc, carry.shape)
    cnt_ref[...] = jnp.broadcast_to(newc, cnt_ref.shape)
    zi = jnp.zeros_like(lane)
    idx_ref[...] = jnp.where(lane == 0, e1, jnp.where(lane == 1, e2, jnp.where(
        lane == 2, r1.astype(jnp.int32), jnp.where(lane == 3, r2.astype(jnp.int32), zi))))
    gate_ref[...] = jnp.where(lane == 0, g1, jnp.where(lane == 1, g2, 0.0))


def _route(lg, ng, epg):
    t, nr = lg.shape
    tm = _tile(t, 512)
    tri = jnp.asarray(np.tril(np.ones((tm, tm), np.float32), -1)).astype(BF16)
    kern = functools.partial(_route_kernel, ng=ng, epg=epg)
    return pl.pallas_call(
        kern,
        grid=(t // tm,),
        in_specs=[pl.BlockSpec((tm, nr), lambda i: (i, 0)), pl.BlockSpec((tm, tm), lambda i: (0, 0))],
        out_specs=[pl.BlockSpec((tm, nr), lambda i: (i, 0)), pl.BlockSpec((tm, nr), lambda i: (i, 0)),
                   pl.BlockSpec((SUBLANES, nr), lambda i: (0, 0))],
        out_shape=[jax.ShapeDtypeStruct((t, nr), jnp.int32), jax.ShapeDtypeStruct((t, nr), F32),
                   jax.ShapeDtypeStruct((SUBLANES, nr), F32)],
        scratch_shapes=[pltpu.VMEM((SUBLANES, nr), F32)],
        compiler_params=_cp("arbitrary"),
        name="moe_route",
    )(lg, tri)


def _ffn_kernel(be_ref, tok_ref, dst_ref, h_ref, wg_ref, wu_ref, wd_ref, o_ref, xb0, xb1, yb0, yb1, gsem, ssem,
                *, blk, nblk, nchunk):
    del be_ref
    b = pl.program_id(0)
    xbufs = (xb0, xb1)
    ybufs = (yb0, yb1)

    def gather_copy(src_row, s, r):
        return pltpu.make_async_copy(h_ref.at[pl.ds(src_row, 1)], xbufs[s].at[pl.ds(r, 1)], gsem)

    def scatter_copy(s, r, dst_row):
        return pltpu.make_async_copy(ybufs[s].at[pl.ds(r, 1)], o_ref.at[pl.ds(dst_row, 1)], ssem)

    def wait_rows(make_copy):
        def body(r, carry):
            make_copy(r).wait()
            return carry

        lax.fori_loop(0, blk, body, 0, unroll=8)

    @pl.when(b == 0)
    def _():
        yb0[...] = jnp.zeros_like(yb0)
        yb1[...] = jnp.zeros_like(yb1)

        def body(r, carry):
            gather_copy(tok_ref[r], 0, r).start()
            return carry

        lax.fori_loop(0, blk, body, 0)

    wait_rows(lambda r: gather_copy(0, 0, r))

    @pl.when(b >= 1)
    def _():
        wait_rows(lambda r: scatter_copy(0, r, 0))

    nxt = jnp.minimum(b + 1, nblk - 1) * blk
    prv = b * blk
    de = wg_ref.shape[1]
    hc = de // nchunk
    per = -(-blk // ((nchunk - 1) * SUBLANES)) * SUBLANES
    bounds = [min(c * per, blk) for c in range(nchunk)] + [blk] * 2

    def step(slot):
        other = 1 - slot
        x = xbufs[slot][...].astype(BF16)
        for c in range(nchunk):
            for r in range(bounds[c], bounds[c + 1] if c < nchunk - 1 else bounds[c]):
                gather_copy(tok_ref[nxt + r], other, r).start()
                scatter_copy(other, r, dst_ref[prv + r]).start()
            cols = slice(c * hc, (c + 1) * hc)
            gate = jnp.dot(x, wg_ref[:, cols], preferred_element_type=F32)
            up = jnp.dot(x, wu_ref[:, cols], preferred_element_type=F32)
            hid = (_silu(gate) * up).astype(BF16)
            part = jnp.dot(hid, wd_ref[cols, :], preferred_element_type=F32)
            if c == 0:
                ybufs[slot][...] = part
            else:
                ybufs[slot][...] = ybufs[slot][...] + part

    @pl.when(b % 2 == 0)
    def _():
        step(0)

    @pl.when(b % 2 == 1)
    def _():
        step(1)

    @pl.when(b == nblk)
    def _():
        wait_rows(lambda r: gather_copy(0, 0, r))
        wait_rows(lambda r: scatter_copy(0, r, 0))


def _ffn(blk_exp, tok, dstp, h, wg_bf, wu_bf, wd_bf, layer, nrows_out):
    d = h.shape[1]
    de = wg_bf.shape[3]
    nblk = tok.shape[0] // MOE_BLK
    kern = functools.partial(_ffn_kernel, blk=MOE_BLK, nblk=nblk, nchunk=4)
    wmap = lambda b, be, tk, ds_: (layer, be[jnp.minimum(b, nblk - 1)], 0, 0)
    buf = pltpu.VMEM((MOE_BLK, d), F32)
    return pl.pallas_call(
        kern,
        grid_spec=pltpu.PrefetchScalarGridSpec(
            num_scalar_prefetch=3,
            grid=(nblk + 1,),
            in_specs=[
                pl.BlockSpec(memory_space=pl.ANY),
                pl.BlockSpec((None, None, d, de), wmap),
                pl.BlockSpec((None, None, d, de), wmap),
                pl.BlockSpec((None, None, de, d), wmap),
            ],
            out_specs=pl.BlockSpec(memory_space=pl.ANY),
            scratch_shapes=[buf, buf, buf, buf, pltpu.SemaphoreType.DMA(()), pltpu.SemaphoreType.DMA(())],
        ),
        out_shape=jax.ShapeDtypeStruct((nrows_out, d), F32),
        compiler_params=_cp("arbitrary"),
        name="moe_ffn",
    )(blk_exp, tok, dstp, h, wg_bf, wu_bf, wd_bf)


def _combine_kernel(ya_ref, yb_ref, x_ref, gt_ref, g_ref, lng_ref, lnb_ref, *o_refs, alpha, ntp):
    gt = gt_ref[...]
    y = gt[:, 0:1] * ya_ref[...] + gt[:, 1:2] * yb_ref[...]
    val = _ln_rows(alpha * x_ref[...] + g_ref[...] * y, lng_ref[...], lnb_ref[...])
    if len(o_refs) == 1:
        o_refs[0][...] = val
    else:
        i = pl.program_id(0)

        @pl.when(i < ntp)
        def _():
            o_refs[0][...] = val

        @pl.when(i >= ntp)
        def _():
            o_refs[1][...] = val


def _combine(y2, x, gates, modl, grp, which_g, ln_g, ln_b, alpha, split):
    t, d = x.shape
    nr = gates.shape[1]
    tm = _tile(math.gcd(grp.tp, grp.ls), 256)
    ntp = grp.tp // tm
    kern = functools.partial(_combine_kernel, alpha=alpha, ntp=ntp)
    assert TOP_K == 2
    nt = t // tm
    if split:
        out_specs = [pl.BlockSpec((tm, d), lambda i: (jnp.minimum(i, ntp - 1), 0)),
                     pl.BlockSpec((tm, d), lambda i: (jnp.maximum(i - ntp, 0), 0))]
        out_shape = [jax.ShapeDtypeStruct((grp.tp, d), F32), jax.ShapeDtypeStruct((t - grp.tp, d), F32)]
    else:
        out_specs = pl.BlockSpec((tm, d), lambda i: (i, 0))
        out_shape = jax.ShapeDtypeStruct((t, d), F32)
    return pl.pallas_call(
        kern,
        grid=(t // tm,),
        in_specs=[
            pl.BlockSpec((tm, d), lambda i: (i, 0)),
            pl.BlockSpec((tm, d), lambda i: (nt + i, 0)),
            pl.BlockSpec((tm, d), lambda i: (i, 0)),
            pl.BlockSpec((tm, nr), lambda i: (i, 0)),
            grp.mod_spec(which_g, tm, d, 1),
            pl.BlockSpec((1, d), lambda i: (0, 0)),
            pl.BlockSpec((1, d), lambda i: (0, 0)),
        ],
        out_specs=out_specs,
        out_shape=out_shape,
        compiler_params=_cp("arbitrary"),
        name="moe_combine_ln",
    )(y2, y2, x, gates, modl, ln_g.reshape(1, d), ln_b.reshape(1, d))


def _moe(h, lg, x, modl, grp, ln_g, ln_b, wg_bf, wu_bf, wd_bf, layer, ng, alpha, split):
    t, d = h.shape
    ne = wg_bf.shape[1]
    na = t * TOP_K
    idx, gates, cnt = _route(lg, ng, ne // ng)
    experts = idx[:, 0:TOP_K]
    ranks = idx[:, TOP_K:2 * TOP_K]
    counts = cnt[0, :ne].astype(jnp.int32)
    padded = (counts + MOE_BLK - 1) // MOE_BLK * MOE_BLK
    pends = jnp.cumsum(padded)
    pstarts = pends - padded
    dest = (jnp.take(pstarts, experts) + ranks).reshape(-1).astype(jnp.int32)
    nblk = -(-na // MOE_BLK) + ne
    ns = nblk * MOE_BLK
    blk_start = jnp.arange(nblk, dtype=jnp.int32) * MOE_BLK
    blk_exp = jnp.minimum(jnp.sum(pends[None, :] <= blk_start[:, None], axis=1), ne - 1).astype(jnp.int32)
    code = jnp.full((ns,), -1, jnp.int32).at[dest].set(jnp.arange(na, dtype=jnp.int32), unique_indices=True)
    spare = na + jnp.arange(ns, dtype=jnp.int32) % MOE_BLK
    tok = jnp.where(code >= 0, code // TOP_K, 0)
    dst = jnp.where(code >= 0, (code % TOP_K) * t + code // TOP_K, spare)
    dstp = jnp.concatenate([na + jnp.arange(MOE_BLK, dtype=jnp.int32), dst])
    y2 = _ffn(blk_exp, tok, dstp, h, wg_bf, wu_bf, wd_bf, layer, na + MOE_BLK)
    return _combine(y2, x, gates, modl, grp, 5, ln_g, ln_b, alpha, split)


def _rope_tables(tp, nb, l):
    rows = l // GRID_W
    row = np.repeat(np.arange(rows, dtype=np.float32), GRID_W)
    col = np.tile(np.arange(GRID_W, dtype=np.float32), rows)
    inv_freq = (np.float32(ROPE_THETA) ** (-np.arange(ROPE_PAIRS, dtype=np.float32) / np.float32(ROPE_PAIRS)))
    ar = row[:, None] * inv_freq
    ac = col[:, None] * inv_freq
    ang = np.concatenate([ar, ar, ac, ac], axis=-1).astype(np.float64)
    cos = np.cos(ang)
    sin = np.sin(ang)
    sign = np.tile(np.concatenate([-np.ones(ROPE_PAIRS), np.ones(ROPE_PAIRS)]), 2)
    cos_all = np.concatenate([np.ones((tp, HEAD_DIM)), np.tile(cos, (nb, 1))], axis=0)
    sin_all = np.concatenate([np.zeros((tp, HEAD_DIM)), np.tile(sin * sign, (nb, 1))], axis=0)
    return jnp.asarray(cos_all.astype(np.float32)), jnp.asarray(sin_all.astype(np.float32))


def kernel(x_prompt, x_sample, cache_k, cache_v, c, c_ctx, w_ada, b_ada, ln_g, ln_b, w_in_ab, q_norm_g, k_norm_g, hy_short_w, hy_short_b, hy_f_w1, hy_f_b1, hy_f_w2, hy_f_b2, hy_f_w3, hy_f_b3, hy_sin_freq, hy_skip, w_out_ab, cv_pw1, cv_dw_w, cv_dw_b, cv_ln_g, cv_ln_b, cv_pw2, moe_w_grp, moe_b_grp, moe_w_exp, moe_b_exp, moe_w_gate, moe_w_up, moe_w_down):
    bp, lp, d = x_prompt.shape
    bs, ls, _ = x_sample.shape
    depth = w_ada.shape[0]
    tp, ts = bp * lp, bs * ls
    t = tp + ts
    kv_w = cache_k.shape[2] * HEAD_DIM
    hy_w = hy_skip.shape[2]
    att_w = d - hy_w
    ng = moe_w_grp.shape[2]
    ne = moe_w_exp.shape[2]
    alpha = (2 * depth) ** 0.25
    nmod = 6
    ngrp = 1 + bs
    gpad = -(-ngrp // SUBLANES) * SUBLANES
    grp = _Groups(tp, ls, nmod)

    x = jnp.concatenate([x_prompt.reshape(tp, d), x_sample.reshape(ts, d)], axis=0)
    cond = jnp.zeros((gpad, d), F32).at[0].set(c_ctx).at[1:ngrp].set(c)
    mod = _ada(cond, w_ada, b_ada)
    nrt = LANES
    wg_bf, wu_bf, wd_bf = moe_w_gate.astype(BF16), moe_w_up.astype(BF16), moe_w_down.astype(BF16)
    new_k, new_v = None, None
    for layer in range(depth):
        i = layer // 2
        modl = mod[layer].reshape(gpad * nmod, 1, d)
        wr = jnp.zeros((d, nrt), F32).at[:, :ng].set(moe_w_grp[layer]).at[:, ng:ng + ne].set(moe_w_exp[layer])
        wr_hi = wr.astype(BF16)
        wr = jnp.concatenate([wr_hi, (wr - wr_hi.astype(F32)).astype(BF16)], axis=0)
        br = jnp.zeros((1, nrt), F32).at[0, :ng].set(moe_b_grp[layer]).at[0, ng:ng + ne].set(moe_b_exp[layer])
        if layer % 2 == 0:
            zp = _mm_mod(x, t, modl, grp, 1, 0, w_in_ab[i].astype(BF16))
            cos_t, sin_t = _rope_tables(tp, bs, ls)
            q_bf, k_f, k_bf = _qkprep(zp, cos_t, sin_t, q_norm_g[i], k_norm_g[i], att_w, kv_w)
            o_p = _attention(q_bf, k_bf, zp, 0, bp, lp, att_w, kv_w)
            o_s = _attention(q_bf, k_bf, zp, tp, bs, ls, att_w, kv_w, cache_k, cache_v, i)
            ucol = att_w + 2 * kv_w
            ys = []
            for row0, nb, l in ((0, bp, lp), (tp, bs, ls)):
                fc = _fft_consts(l)
                kf = _hyena_filters(l, fc, hy_f_w1[i], hy_f_b1[i], hy_f_w2[i], hy_f_b2[i], hy_f_w3[i], hy_f_b3[i],
                                    hy_sin_freq[i], hy_w)
                usc = _dwconv(zp, ucol, 3 * hy_w, hy_short_w[i], hy_short_b[i], row0, nb, l)
                z1 = _lconv(usc, 0, 0, usc, hy_w, 0, kf, 0, hy_skip[i, 0], fc, nb, l, hy_w)
                ys.append(_lconv(z1, 0, 0, usc, 2 * hy_w, 0, kf, 1, hy_skip[i, 1], fc, nb, l, hy_w))
            x, h, lg = _out_ln((o_p, o_s), tuple(ys), w_out_ab[i].astype(BF16), x, t, modl, grp,
                               ln_g[layer, 0], ln_b[layer, 0], wr, br, alpha)
            nkv = kv_w // HEAD_DIM
            kk = k_f[:tp].reshape(bp, lp, nkv, HEAD_DIM).transpose(0, 2, 1, 3)
            vv = zp[:tp, att_w + kv_w:att_w + 2 * kv_w].reshape(bp, lp, nkv, HEAD_DIM).transpose(0, 2, 1, 3)
            new_k = kk if new_k is None else jnp.concatenate([new_k, kk], axis=1)
            new_v = vv if new_v is None else jnp.concatenate([new_v, vv], axis=1)
        else:
            a = _mm_mod(x, t, modl, grp, 1, 0, cv_pw1[i].astype(BF16), glu=True)
            a = (_dwconv(a, 0, d, cv_dw_w[i], cv_dw_b[i], 0, bp, lp),
                 _dwconv(a, 0, d, cv_dw_w[i], cv_dw_b[i], tp, bs, ls))
            x, h, lg = _pw2_ln(a, cv_ln_g[i], cv_ln_b[i], cv_pw2[i].astype(BF16), x, t, modl, grp,
                               ln_g[layer, 0], ln_b[layer, 0], wr, br, alpha)
        x = _moe(h, lg, x, modl, grp, ln_g[layer, 1], ln_b[layer, 1], wg_bf, wu_bf, wd_bf, layer, ng, alpha,
                 layer == depth - 1)
    nkv = kv_w // HEAD_DIM
    n_attn = (depth + 1) // 2
    new_k = new_k.reshape(bp, n_attn, nkv, lp, HEAD_DIM)
    new_v = new_v.reshape(bp, n_attn, nkv, lp, HEAD_DIM)
    return (x[0].reshape(bp, lp, d), x[1].reshape(bs, ls, d), new_k, new_v)
```

```python
import functools
import math

import jax
import jax.numpy as jnp
import numpy as np
from jax import lax
from jax.experimental import pallas as pl
from jax.experimental.pallas import tpu as pltpu

F32 = jnp.float32
BF16 = jnp.bfloat16
HIGHEST = lax.Precision.HIGHEST

GRID_W = 64
HEAD_DIM = 128
ROPE_THETA = 10000.0
ROPE_PAIRS = HEAD_DIM // 4
HY_BANDS = 16
HY_TARGET = 1e-2
HY_FAST = 0.3
HY_SLOW = 1.5
TOP_K = 2
EPS = 1e-6

LANES = 128
SUBLANES = 8
VMEM_LIMIT = 56 * 1024 * 1024
MOE_BLK = 256
FFT_SPLIT = 64
FFT_DENSE_MAX = 1024
FFT_UNROLL = 8
ATTN_PARTS = 2


def _cp(*sem):
    return pltpu.CompilerParams(dimension_semantics=sem, vmem_limit_bytes=VMEM_LIMIT)


def _tile(n, target, mult=SUBLANES):
    if n <= target:
        return n
    for t in range(target, 0, -1):
        if n % t == 0 and t % mult == 0:
            return t
    return n


def _silu(x):
    return x * jax.nn.sigmoid(x)


def _ln_rows(x, g, b):
    mu = jnp.mean(x, axis=-1, keepdims=True)
    xc = x - mu
    var = jnp.mean(xc * xc, axis=-1, keepdims=True)
    return xc * lax.rsqrt(var + EPS) * g + b


def _ada_kernel(c_ref, w_ref, b_ref, o_ref):
    s = _silu(c_ref[...])
    o_ref[...] = jnp.dot(s, w_ref[...], precision=HIGHEST, preferred_element_type=F32) + b_ref[...]


def _ada(cond, w_ada, b_ada):
    depth, d, n = w_ada.shape
    g = cond.shape[0]
    tn = _tile(n, 1024, LANES)
    return pl.pallas_call(
        _ada_kernel,
        grid=(depth, n // tn),
        in_specs=[
            pl.BlockSpec((g, d), lambda l, j: (0, 0)),
            pl.BlockSpec((None, d, tn), lambda l, j: (l, 0, j)),
            pl.BlockSpec((None, 1, tn), lambda l, j: (l, 0, j)),
        ],
        out_specs=pl.BlockSpec((None, g, tn), lambda l, j: (l, 0, j)),
        out_shape=jax.ShapeDtypeStruct((depth, g, n), F32),
        compiler_params=_cp("arbitrary", "arbitrary"),
        name="ada_mod",
    )(cond, w_ada, b_ada.reshape(depth, 1, n))


def _load_rows(refs, ntp):
    if len(refs) == 1:
        return refs[0][...]
    return jnp.where(pl.program_id(0) < ntp, refs[0][...], refs[1][...])


def _row_operand(arr, tm, cols, ntp, nargs):
    if isinstance(arr, tuple):
        if nargs == 2:
            maps = (lambda i, j: (jnp.minimum(i, ntp - 1), 0), lambda i, j: (jnp.maximum(i - ntp, 0), 0))
        else:
            maps = (lambda i: (jnp.minimum(i, ntp - 1), 0), lambda i: (jnp.maximum(i - ntp, 0), 0))
        return list(arr), [pl.BlockSpec((tm, cols), m) for m in maps]
    one = (lambda i, j: (i, 0)) if nargs == 2 else (lambda i: (i, 0))
    return [arr], [pl.BlockSpec((tm, cols), one)]


def _mm_mod_kernel(*refs, nx, ntp):
    sc_ref, sh_ref, w_ref, o_ref, h_scr = refs[nx:]

    @pl.when(pl.program_id(1) == 0)
    def _():
        h_scr[...] = (_load_rows(refs[:nx], ntp) * (1.0 + sc_ref[...]) + sh_ref[...]).astype(BF16)

    o_ref[...] = jnp.dot(h_scr[...], w_ref[...], preferred_element_type=F32)


def _mm_glu_kernel(*refs, nx, ntp):
    sc_ref, sh_ref, wa_ref, wg_ref, o_ref, h_scr = refs[nx:]

    @pl.when(pl.program_id(1) == 0)
    def _():
        h_scr[...] = (_load_rows(refs[:nx], ntp) * (1.0 + sc_ref[...]) + sh_ref[...]).astype(BF16)

    a = jnp.dot(h_scr[...], wa_ref[...], preferred_element_type=F32)
    g = jnp.dot(h_scr[...], wg_ref[...], preferred_element_type=F32)
    o_ref[...] = a * jax.nn.sigmoid(g)


class _Groups:
    def __init__(self, tp, ls, nmod):
        self.tp, self.ls, self.nmod = tp, ls, nmod

    def gid(self, i, tm):
        row = i * tm
        return jnp.where(row < self.tp, 0, 1 + (row - self.tp) // self.ls)

    def mod_spec(self, which, tm, d, nargs=2):
        if nargs == 2:
            return pl.BlockSpec((None, 1, d), lambda i, j: (self.gid(i, tm) * self.nmod + which, 0, 0))
        return pl.BlockSpec((None, 1, d), lambda i: (self.gid(i, tm) * self.nmod + which, 0, 0))


def _mm_mod(x, t, modl, grp, which_sc, which_sh, w_bf, glu=False):
    d = w_bf.shape[0]
    n = w_bf.shape[1] // (2 if glu else 1)
    tm = _tile(math.gcd(grp.tp, grp.ls), 1024)
    tn = _tile(n, 512, LANES)
    ntp = grp.tp // tm
    args, specs = _row_operand(x, tm, d, ntp, 2)
    nx = len(args)
    specs += [
        grp.mod_spec(which_sc, tm, d),
        grp.mod_spec(which_sh, tm, d),
        pl.BlockSpec((d, tn), lambda i, j: (0, j)),
    ]
    args += [modl, modl, w_bf]
    if glu:
        noff = n // tn
        specs.append(pl.BlockSpec((d, tn), lambda i, j: (0, j + noff)))
        args.append(w_bf)
    return pl.pallas_call(
        functools.partial(_mm_glu_kernel if glu else _mm_mod_kernel, nx=nx, ntp=ntp),
        grid=(t // tm, n // tn),
        in_specs=specs,
        out_specs=pl.BlockSpec((tm, tn), lambda i, j: (i, j)),
        out_shape=jax.ShapeDtypeStruct((t, n), F32),
        scratch_shapes=[pltpu.VMEM((tm, d), BF16)],
        compiler_params=_cp("arbitrary", "arbitrary"),
        name="mm_glu" if glu else "mm_mod",
    )(*args)


def _qkprep_kernel(q_ref, k_ref, c_ref, s_ref, qg_ref, kg_ref, qo_ref, ko_ref, kb_ref, *, nq, nk, scale):
    c = c_ref[...]
    s = s_ref[...]
    lane = lax.broadcasted_iota(jnp.int32, c.shape, 1)
    first = (lane % (2 * ROPE_PAIRS)) < ROPE_PAIRS

    def prep(xh, g):
        ms = jnp.mean(xh * xh, axis=-1, keepdims=True)
        xn = xh * lax.rsqrt(ms + EPS) * g
        rot = jnp.where(first, pltpu.roll(xn, HEAD_DIM - ROPE_PAIRS, 1), pltpu.roll(xn, ROPE_PAIRS, 1))
        return xn * c + rot * s

    for h in range(nq):
        sl = slice(h * HEAD_DIM, (h + 1) * HEAD_DIM)
        qo_ref[:, sl] = (prep(q_ref[:, sl], qg_ref[...]) * scale).astype(BF16)
    for h in range(nk):
        sl = slice(h * HEAD_DIM, (h + 1) * HEAD_DIM)
        kh = prep(k_ref[:, sl], kg_ref[...])
        ko_ref[:, sl] = kh
        kb_ref[:, sl] = kh.astype(BF16)


def _qkprep(zp, cos_t, sin_t, q_g, k_g, att_w, kv_w):
    t = zp.shape[0]
    tm = _tile(t, 512)
    nq, nk = att_w // HEAD_DIM, kv_w // HEAD_DIM
    assert att_w % kv_w == 0
    kern = functools.partial(_qkprep_kernel, nq=nq, nk=nk, scale=HEAD_DIM ** -0.5)
    return pl.pallas_call(
        kern,
        grid=(t // tm,),
        in_specs=[
            pl.BlockSpec((tm, att_w), lambda i: (i, 0)),
            pl.BlockSpec((tm, kv_w), lambda i: (i, att_w // kv_w)),
            pl.BlockSpec((tm, HEAD_DIM), lambda i: (i, 0)),
            pl.BlockSpec((tm, HEAD_DIM), lambda i: (i, 0)),
            pl.BlockSpec((1, HEAD_DIM), lambda i: (0, 0)),
            pl.BlockSpec((1, HEAD_DIM), lambda i: (0, 0)),
        ],
        out_specs=[
            pl.BlockSpec((tm, att_w), lambda i: (i, 0)),
            pl.BlockSpec((tm, kv_w), lambda i: (i, 0)),
            pl.BlockSpec((tm, kv_w), lambda i: (i, 0)),
        ],
        out_shape=[
            jax.ShapeDtypeStruct((t, att_w), BF16),
            jax.ShapeDtypeStruct((t, kv_w), F32),
            jax.ShapeDtypeStruct((t, kv_w), BF16),
        ],
        compiler_params=_cp("arbitrary"),
        name="qk_prep",
    )(zp, zp, cos_t, sin_t, q_g.reshape(1, HEAD_DIM), k_g.reshape(1, HEAD_DIM))


def _attn_kernel(*refs, has_cache, g, tq):
    if has_cache:
        q_ref, k_ref, v_ref, kc_ref, vc_ref, o_ref = refs
    else:
        q_ref, k_ref, v_ref, o_ref = refs
    nt = (((1,), (1,)), ((), ()))
    nparts = ATTN_PARTS if g % ATTN_PARTS == 0 else 1
    hp = g // nparts
    for part in range(nparts):
        heads = range(part * hp, (part + 1) * hp)
        qs = jnp.concatenate([q_ref[:, h * HEAD_DIM:(h + 1) * HEAD_DIM] for h in heads], axis=0)
        s = lax.dot_general(qs, k_ref[...], nt, preferred_element_type=F32)
        m = jnp.max(s, axis=-1, keepdims=True)
        if has_cache:
            sc = lax.dot_general(qs, kc_ref[...].astype(BF16), nt, preferred_element_type=F32)
            m = jnp.maximum(m, jnp.max(sc, axis=-1, keepdims=True))
        p = jnp.exp(s - m)
        l = jnp.sum(p, axis=-1, keepdims=True)
        o = jnp.dot(p.astype(BF16), v_ref[...].astype(BF16), preferred_element_type=F32)
        if has_cache:
            pc = jnp.exp(sc - m)
            l = l + jnp.sum(pc, axis=-1, keepdims=True)
            o = o + jnp.dot(pc.astype(BF16), vc_ref[...].astype(BF16), preferred_element_type=F32)
        o = o / l
        for i, h in enumerate(heads):
            o_ref[:, h * HEAD_DIM:(h + 1) * HEAD_DIM] = o[i * tq:(i + 1) * tq].astype(BF16)


def _attention(q_bf, k_bf, zp, row0, nb, l, att_w, kv_w, cache_k=None, cache_v=None, layer=0):
    nkv = kv_w // HEAD_DIM
    g = att_w // kv_w
    tq = _tile(l, max(SUBLANES, 512 // g))
    rb = row0 // l
    assert row0 % l == 0
    vcol = (att_w + kv_w) // HEAD_DIM
    has_cache = cache_k is not None
    specs = [
        pl.BlockSpec((tq, g * HEAD_DIM), lambda b, h, i: ((row0 + b * l) // tq + i, h)),
        pl.BlockSpec((l, HEAD_DIM), lambda b, h, i: (rb + b, h)),
        pl.BlockSpec((l, HEAD_DIM), lambda b, h, i: (rb + b, vcol + h)),
    ]
    args = [q_bf, k_bf, zp]
    if has_cache:
        past = cache_k.shape[3]
        cspec = pl.BlockSpec((None, None, None, past, HEAD_DIM), lambda b, h, i: (b, layer, h, 0, 0))
        specs += [cspec, cspec]
        args += [cache_k, cache_v]
    kern = functools.partial(_attn_kernel, has_cache=has_cache, g=g, tq=tq)
    return pl.pallas_call(
        kern,
        grid=(nb, nkv, l // tq),
        in_specs=specs,
        out_specs=pl.BlockSpec((tq, g * HEAD_DIM), lambda b, h, i: (b * (l // tq) + i, h)),
        out_shape=jax.ShapeDtypeStruct((nb * l, att_w), BF16),
        compiler_params=_cp("arbitrary", "arbitrary", "arbitrary"),
        name="attn_cache" if has_cache else "attn",
    )(*args)


def _dwconv_kernel(x_ref, w_ref, b_ref, o_ref, pad, *, l, width, rc, off):
    ct = x_ref.shape[1]
    zeros = jnp.zeros((off, ct), F32)
    pad[0:off, :] = zeros
    pad[off + l:off + l + off, :] = zeros
    pad[off:off + l, :] = x_ref[...]
    bias = b_ref[...]
    shift = off - width // 2

    def chunk(ci, carry):
        base = pl.multiple_of(ci * rc, rc)
        win = pad[pl.ds(base, rc + 2 * off), :]
        acc = jnp.zeros((rc, ct), F32) + bias
        span = rc + 2 * off - SUBLANES
        for p in range(SUBLANES):
            taps = [k for k in range(width) if (k + shift) % SUBLANES == p]
            if taps:
                shifted = win[p:p + span, :]
                for k in taps:
                    o = k + shift - p
                    wk = jnp.concatenate([w_ref[SUBLANES * k:SUBLANES * (k + 1), :]] * (rc // SUBLANES), axis=0)
                    acc = acc + wk * shifted[o:o + rc, :]
        o_ref[pl.ds(base, rc), :] = acc
        return carry

    lax.fori_loop(0, l // rc, chunk, 0)


def _dwconv(x, col0, ncols, w, b, row0, nb, l):
    width = w.shape[0]
    off = 16
    assert width // 2 <= off
    ct = LANES if width > SUBLANES else _tile(ncols, 2 * LANES, LANES)
    rc = _tile(l, 64)
    assert rc % SUBLANES == 0
    wp = jnp.broadcast_to(w[:, None, :], (width, SUBLANES, ncols)).reshape(width * SUBLANES, ncols)
    assert col0 % ct == 0 and row0 % l == 0
    kern = functools.partial(_dwconv_kernel, l=l, width=width, rc=rc, off=off)
    return pl.pallas_call(
        kern,
        grid=(nb, ncols // ct),
        in_specs=[
            pl.BlockSpec((l, ct), lambda b_, j: (row0 // l + b_, col0 // ct + j)),
            pl.BlockSpec((width * SUBLANES, ct), lambda b_, j: (0, j)),
            pl.BlockSpec((1, ct), lambda b_, j: (0, j)),
        ],
        out_specs=pl.BlockSpec((l, ct), lambda b_, j: (b_, j)),
        out_shape=jax.ShapeDtypeStruct((nb * l, ncols), F32),
        scratch_shapes=[pltpu.VMEM((l + 2 * off, ct), F32)],
        compiler_params=_cp("arbitrary", "arbitrary"),
        name="dwconv%d" % width,
    )(x, wp, b.reshape(1, ncols))


def _fft_plan(l):
    n = 2 * l
    if n <= FFT_DENSE_MAX or n % FFT_SPLIT != 0:
        return n, 1
    return n // FFT_SPLIT, FFT_SPLIT


def _stack3_lhs(m):
    hi = m.astype(np.float32).astype(jnp.bfloat16)
    lo = (m - np.asarray(hi.astype(np.float32), np.float64)).astype(np.float32).astype(jnp.bfloat16)
    return jnp.asarray(np.concatenate([hi, lo, hi], axis=1))


def _split3(x):
    hi = x.astype(BF16)
    lo = (x - hi.astype(F32)).astype(BF16)
    return jnp.concatenate([hi, hi, lo], axis=0)


def _mm3(lhs3, x):
    return jnp.dot(lhs3, _split3(x), preferred_element_type=F32)


def _fft_consts(l):
    n1, n2 = _fft_plan(l)
    n = n1 * n2
    n1h = n1 // 2
    k1 = np.arange(n1, dtype=np.float64)[:, None]
    ang = 2.0 * np.pi * k1 * np.arange(n1h, dtype=np.float64)[None, :] / n1
    fr, fi = np.cos(ang), -np.sin(ang)
    a_fwd = np.block([[fr, -fi], [fi, fr]])
    ang = 2.0 * np.pi * k1 * np.arange(n1, dtype=np.float64)[None, :] / n1
    a_real = np.concatenate([np.cos(ang), -np.sin(ang)], axis=0)
    angi = 2.0 * np.pi * np.arange(n1h, dtype=np.float64)[:, None] * np.arange(n1, dtype=np.float64)[None, :] / n1
    cr, ci = np.cos(angi) / n, np.sin(angi) / n
    a_inv = np.block([[cr, -ci], [ci, cr]])
    k2 = np.arange(n2, dtype=np.float64)
    angb = 2.0 * np.pi * k2[:, None] * k2[None, :] / n2
    gr, gi = np.cos(angb), -np.sin(angb)
    b_fwd = np.block([[gr, -gi], [gi, gr]])
    b_inv = np.block([[gr, gi], [-gi, gr]])
    angt = 2.0 * np.pi * (np.arange(n1, dtype=np.float64)[:, None] * k2[None, :]).reshape(n, 1) / n
    twr = np.broadcast_to(np.cos(angt), (n, LANES)).astype(np.float32)
    twi = np.broadcast_to(-np.sin(angt), (n, LANES)).astype(np.float32)
    return dict(n1=n1, n2=n2, a_fwd=_stack3_lhs(a_fwd), a_real=_stack3_lhs(a_real), a_inv=_stack3_lhs(a_inv),
                b_fwd=_stack3_lhs(b_fwd), b_inv=_stack3_lhs(b_inv), twr=jnp.asarray(twr), twi=jnp.asarray(twi))


def _rows(ref, start, size, stride):
    if stride == 1:
        return ref[pl.ds(start, size), :]
    return ref[pl.ds(start, size, stride=stride), :]


def _stage_b_fwd(pr, pi, twr_ref, twi_ref, bfwd_ref, k1, n1, n2):
    xr = _rows(pr, k1, n2, n1)
    xi = _rows(pi, k1, n2, n1)
    r0 = pl.multiple_of(k1 * n2, n2)
    cr = twr_ref[pl.ds(r0, n2), :]
    ci = twi_ref[pl.ds(r0, n2), :]
    ar = xr * cr - xi * ci
    ai = xr * ci + xi * cr
    x = _mm3(bfwd_ref[...], jnp.concatenate([ar, ai], axis=0))
    return x[:n2], x[n2:], cr, ci


def _lconv_kernel(a_ref, g_ref, kf_ref, skip_ref, afwd_ref, ainv_ref, bfwd_ref, binv_ref, twr_ref, twi_ref,
                  o_ref, pr, pi, qr, qi, *, l, n1, n2):
    n1h = n1 // 2

    def stage_a(i2, carry):
        slab = jnp.concatenate([_rows(a_ref, i2, n1h, n2), _rows(a_ref, l + i2, n1h, n2)], axis=0)
        out = _mm3(afwd_ref[...], slab)
        r0 = pl.multiple_of(i2 * n1, n1)
        pr[pl.ds(r0, n1), :] = out[:n1]
        pi[pl.ds(r0, n1), :] = out[n1:]
        return carry

    lax.fori_loop(0, n2, stage_a, 0, unroll=min(n2, FFT_UNROLL))

    if n2 == 1:
        kr, ki = kf_ref[0], kf_ref[1]
        xr, xi = pr[...], pi[...]
        qr[...] = xr * kr - xi * ki
        qi[...] = xr * ki + xi * kr
    else:
        def stage_b(k1, carry):
            xr, xi, cr, ci = _stage_b_fwd(pr, pi, twr_ref, twi_ref, bfwd_ref, k1, n1, n2)
            r0 = pl.multiple_of(k1 * n2, n2)
            kr = kf_ref[0, pl.ds(r0, n2), :]
            ki = kf_ref[1, pl.ds(r0, n2), :]
            yr = xr * kr - xi * ki
            yi = xr * ki + xi * kr
            bv = _mm3(binv_ref[...], jnp.concatenate([yr, yi], axis=0))
            br, bi = bv[:n2], bv[n2:]
            qr[pl.ds(r0, n2), :] = br * cr + bi * ci
            qi[pl.ds(r0, n2), :] = bi * cr - br * ci
            return carry

        lax.fori_loop(0, n1, stage_b, 0, unroll=FFT_UNROLL)

    skip = skip_ref[...]

    def stage_c(i2, carry):
        slab = jnp.concatenate([_rows(qr, i2, n1, n2), _rows(qi, i2, n1, n2)], axis=0)
        y = _mm3(ainv_ref[...], slab)
        for half in range(2):
            start = half * l + i2
            a = _rows(a_ref, start, n1h, n2)
            g = _rows(g_ref, start, n1h, n2)
            val = g * (y[half * n1h:(half + 1) * n1h] + skip * a)
            if n2 == 1:
                o_ref[pl.ds(start, n1h), :] = val
            else:
                o_ref[pl.ds(start, n1h, stride=n2), :] = val
        return carry

    lax.fori_loop(0, n2, stage_c, 0, unroll=min(n2, FFT_UNROLL))


def _const_spec(shape):
    zeros = (0,) * len(shape)
    return pl.BlockSpec(shape, lambda a, b: zeros, pipeline_mode=pl.Buffered(1))


def _lconv(a, a_col0, a_row0, gate, g_col0, g_row0, kf, order, skip, fc, nb, l, c):
    n1, n2 = fc["n1"], fc["n2"]
    n = n1 * n2
    ct = LANES
    assert nb % 2 == 0 and a_row0 % (2 * l) == 0 and g_row0 % (2 * l) == 0
    single = pl.Buffered(1) if n2 > 1 else None
    kern = functools.partial(_lconv_kernel, l=l, n1=n1, n2=n2)
    return pl.pallas_call(
        kern,
        grid=(c // ct, nb // 2),
        in_specs=[
            pl.BlockSpec((2 * l, ct), lambda j, p: (a_row0 // (2 * l) + p, a_col0 // ct + j), pipeline_mode=single),
            pl.BlockSpec((2 * l, ct), lambda j, p: (g_row0 // (2 * l) + p, g_col0 // ct + j), pipeline_mode=single),
            pl.BlockSpec((None, 2, n, ct), lambda j, p: (order, 0, 0, j), pipeline_mode=single),
            pl.BlockSpec((1, ct), lambda j, p: (0, j)),
            _const_spec(fc["a_fwd"].shape), _const_spec(fc["a_inv"].shape),
            _const_spec(fc["b_fwd"].shape), _const_spec(fc["b_inv"].shape),
            _const_spec((n, LANES)), _const_spec((n, LANES)),
        ],
        out_specs=pl.BlockSpec((2 * l, ct), lambda j, p: (p, j), pipeline_mode=single),
        out_shape=jax.ShapeDtypeStruct((nb * l, c), F32),
        scratch_shapes=[pltpu.VMEM((n, ct), F32) for _ in range(4)],
        compiler_params=_cp("arbitrary", "arbitrary"),
        name="hyena_lconv",
    )(a, gate, kf, skip.reshape(1, c), fc["a_fwd"], fc["a_inv"], fc["b_fwd"], fc["b_inv"], fc["twr"], fc["twi"])


def _hyfilt_kernel(zz_ref, tt_ref, w1_ref, b1_ref, w2_ref, b2_ref, sf_ref, w3f_ref, w3b_ref, b3f_ref, b3b_ref,
                   dl_ref, areal_ref, bfwd_ref, twr_ref, twi_ref, o_ref, h_scr, kern_scr, pr, pi, *, l, n1, n2):
    rc = _tile(l, 256)
    nch = l // rc

    @pl.when((pl.program_id(0) == 0) & (pl.program_id(1) == 0))
    def _():
        def mlp(ci, carry):
            r0 = pl.multiple_of(ci * rc, rc)
            h = jnp.dot(zz_ref[pl.ds(r0, rc), :], w1_ref[...], precision=HIGHEST, preferred_element_type=F32)
            h = jnp.sin(sf_ref[0:1, :] * (h + b1_ref[...]))
            h = jnp.dot(h, w2_ref[...], precision=HIGHEST, preferred_element_type=F32) + b2_ref[...]
            h_scr[pl.ds(r0, rc), :] = jnp.sin(sf_ref[1:2, :] * h)
            return carry

        lax.fori_loop(0, 2 * nch, mlp, 0)

    def filt(w3_ref, b3_ref, first_chunk):
        def body(ci, acc):
            r0 = pl.multiple_of((first_chunk + ci) * rc, rc)
            hk = jnp.dot(h_scr[pl.ds(r0, rc), :], w3_ref[...], precision=HIGHEST, preferred_element_type=F32)
            win = jnp.exp(-tt_ref[pl.ds(r0, rc), :] * dl_ref[...])
            row = r0 + lax.broadcasted_iota(jnp.int32, (rc, 1), 0)
            k = jnp.where(row == l, 0.0, (hk + b3_ref[...]) * win)
            kern_scr[pl.ds(r0, rc), :] = k
            return acc + jnp.sum(jnp.abs(k), axis=0, keepdims=True)
        return body

    zero = jnp.zeros((1, kern_scr.shape[1]), F32)
    total = lax.fori_loop(0, nch, filt(w3f_ref, b3f_ref, 0), zero)
    total = lax.fori_loop(0, nch, filt(w3b_ref, b3b_ref, nch), total)
    inv = 1.0 / total

    def stage_a(i2, carry):
        out = _mm3(areal_ref[...], _rows(kern_scr, i2, n1, n2) * inv)
        r0 = pl.multiple_of(i2 * n1, n1)
        pr[pl.ds(r0, n1), :] = out[:n1]
        pi[pl.ds(r0, n1), :] = out[n1:]
        return carry

    lax.fori_loop(0, n2, stage_a, 0, unroll=min(n2, FFT_UNROLL))

    if n2 == 1:
        o_ref[0] = pr[...]
        o_ref[1] = pi[...]
    else:
        def stage_b(k1, carry):
            xr, xi, _, _ = _stage_b_fwd(pr, pi, twr_ref, twi_ref, bfwd_ref, k1, n1, n2)
            r0 = pl.multiple_of(k1 * n2, n2)
            o_ref[0, pl.ds(r0, n2), :] = xr
            o_ref[1, pl.ds(r0, n2), :] = xi
            return carry

        lax.fori_loop(0, n1, stage_b, 0, unroll=FFT_UNROLL)


def _hyena_filters(l, fc, w1, b1, w2, b2, w3, b3, sin_freq, c):
    n1, n2 = fc["n1"], fc["n2"]
    n = 2 * l
    order = w3.shape[1] // (2 * c)
    hid = w1.shape[1]
    ct = LANES
    pos = np.concatenate([np.arange(l), [0], np.arange(l - 1, 0, -1)]).astype(np.float32)
    t01 = np.linspace(0.0, 1.0, l, dtype=np.float32)[pos.astype(np.int64)]
    bands = np.linspace(1e-4, HY_BANDS - 1, HY_BANDS, dtype=np.float32)
    ang = (np.float32(2.0 * math.pi) * pos / np.float32(l))[:, None] * bands
    emb = 1 + 2 * HY_BANDS
    feat = np.zeros((n, hid), np.float32)
    feat[:, :emb] = np.concatenate([t01[:, None], np.cos(ang), -np.sin(ang)], axis=-1)
    w1p = jnp.zeros((hid, hid), F32).at[:emb].set(w1)
    tt = jnp.asarray(np.broadcast_to(t01[:, None], (n, LANES)).copy())
    dmin = math.log(HY_TARGET) / HY_SLOW
    dmax = math.log(HY_TARGET) / HY_FAST
    deltas = jnp.abs(jnp.linspace(dmin, dmax, c, dtype=F32)).reshape(1, c)
    cb = c // ct
    kern = functools.partial(_hyfilt_kernel, l=l, n1=n1, n2=n2)
    return pl.pallas_call(
        kern,
        grid=(order, cb),
        in_specs=[
            _const_spec((n, hid)), _const_spec((n, LANES)),
            _const_spec((hid, hid)), _const_spec((1, hid)), _const_spec((hid, hid)), _const_spec((1, hid)),
            _const_spec((2, hid)),
            pl.BlockSpec((hid, ct), lambda o, j: (0, (2 * o) * cb + j)),
            pl.BlockSpec((hid, ct), lambda o, j: (0, (2 * o + 1) * cb + j)),
            pl.BlockSpec((1, ct), lambda o, j: (0, (2 * o) * cb + j)),
            pl.BlockSpec((1, ct), lambda o, j: (0, (2 * o + 1) * cb + j)),
            pl.BlockSpec((1, ct), lambda o, j: (0, j)),
            _const_spec(fc["a_real"].shape), _const_spec(fc["b_fwd"].shape),
            _const_spec((n, LANES)), _const_spec((n, LANES)),
        ],
        out_specs=pl.BlockSpec((None, 2, n, ct), lambda o, j: (o, 0, 0, j), pipeline_mode=pl.Buffered(1)),
        out_shape=jax.ShapeDtypeStruct((order, 2, n, c), F32),
        scratch_shapes=[pltpu.VMEM((n, hid), F32), pltpu.VMEM((n, ct), F32), pltpu.VMEM((n, ct), F32),
                        pltpu.VMEM((n, ct), F32)],
        compiler_params=_cp("arbitrary", "arbitrary"),
        name="hyena_filters",
    )(jnp.asarray(feat), tt, w1p, b1.reshape(1, hid), w2, b2.reshape(1, hid), sin_freq, w3, w3,
      b3.reshape(1, -1), b3.reshape(1, -1), deltas, fc["a_real"], fc["b_fwd"], fc["twr"], fc["twi"])


def _res_ln_epilogue(acc, x, g_ref, lng_ref, lnb_ref, sc_ref, sh_ref, wr_ref, br_ref, xo_ref, h_ref, lg_ref, alpha):
    xn = _ln_rows(alpha * x + g_ref[...] * acc, lng_ref[...], lnb_ref[...])
    xo_ref[...] = xn
    h = xn * (1.0 + sc_ref[...]) + sh_ref[...]
    h_ref[...] = h
    d = h.shape[1]
    hh = h.astype(BF16)
    hl = (h - hh.astype(F32)).astype(BF16)
    lg = jnp.dot(hh, wr_ref[0:d, :], preferred_element_type=F32)
    lg = lg + jnp.dot(hh, wr_ref[d:2 * d, :], preferred_element_type=F32)
    lg = lg + jnp.dot(hl, wr_ref[0:d, :], preferred_element_type=F32)
    lg_ref[...] = lg + br_ref[...]


def _out_ln_kernel(*refs, no, ny, nx, ntp, alpha):
    o = _load_rows(refs[:no], ntp)
    y = _load_rows(refs[no:no + ny], ntp)
    w1_ref, w2_ref = refs[no + ny:no + ny + 2]
    rest = refs[no + ny + 2:]
    acc = jnp.dot(o, w1_ref[...], preferred_element_type=F32)
    acc = acc + jnp.dot(y.astype(BF16), w2_ref[...], preferred_element_type=F32)
    _res_ln_epilogue(acc, _load_rows(rest[:nx], ntp), *rest[nx:], alpha)


def _pw2_ln_kernel(*refs, na, nx, ntp, alpha):
    cg_ref, cb_ref, w_ref = refs[na:na + 3]
    rest = refs[na + 3:]
    a = _silu(_ln_rows(_load_rows(refs[:na], ntp), cg_ref[...], cb_ref[...])).astype(BF16)
    acc = jnp.dot(a, w_ref[...], preferred_element_type=F32)
    _res_ln_epilogue(acc, _load_rows(rest[:nx], ntp), *rest[nx:], alpha)


def _res_ln_call(kern, name, lead_args, lead_specs, x, t, modl, grp, which_g, which_sc, which_sh, ln_g, ln_b, wr, br,
                 tm):
    d, nr = wr.shape[0] // 2, wr.shape[1]
    row = lambda i: (i, 0)
    const = lambda i: (0, 0)
    xargs, xspecs = _row_operand(x, tm, d, grp.tp // tm, 1)
    specs = lead_specs + xspecs + [
        grp.mod_spec(which_g, tm, d, 1),
        pl.BlockSpec((1, d), const), pl.BlockSpec((1, d), const),
        grp.mod_spec(which_sc, tm, d, 1), grp.mod_spec(which_sh, tm, d, 1),
        pl.BlockSpec((2 * d, nr), const, pipeline_mode=pl.Buffered(1)), pl.BlockSpec((1, nr), const),
    ]
    return pl.pallas_call(
        functools.partial(kern, nx=len(xargs)),
        grid=(t // tm,),
        in_specs=specs,
        out_specs=[pl.BlockSpec((tm, d), row), pl.BlockSpec((tm, d), row), pl.BlockSpec((tm, nr), row)],
        out_shape=[jax.ShapeDtypeStruct((t, d), F32), jax.ShapeDtypeStruct((t, d), F32),
                   jax.ShapeDtypeStruct((t, nr), F32)],
        compiler_params=_cp("arbitrary"),
        name=name,
    )(*lead_args, *xargs, modl, ln_g.reshape(1, d), ln_b.reshape(1, d), modl, modl, wr, br)


def _out_ln(o_bf, y, w_out_bf, x, t, modl, grp, ln_g, ln_b, wr, br, alpha):
    d = w_out_bf.shape[1]
    ka = d // 2
    tm = _tile(math.gcd(grp.tp, grp.ls), 512)
    ntp = grp.tp // tm
    single = pl.Buffered(1)
    oargs, ospecs = _row_operand(o_bf, tm, ka, ntp, 1)
    yargs, yspecs = _row_operand(y, tm, ka, ntp, 1)
    lead_specs = ospecs + yspecs + [
        pl.BlockSpec((ka, d), lambda i: (0, 0), pipeline_mode=single),
        pl.BlockSpec((ka, d), lambda i: (1, 0), pipeline_mode=single),
    ]
    kern = functools.partial(_out_ln_kernel, no=len(oargs), ny=len(yargs), ntp=ntp, alpha=alpha)
    return _res_ln_call(kern, "out_proj_ln", oargs + yargs + [w_out_bf, w_out_bf], lead_specs, x, t, modl, grp,
                        2, 4, 3, ln_g, ln_b, wr, br, tm)


def _pw2_ln(a, cv_g, cv_b, w_bf, x, t, modl, grp, ln_g, ln_b, wr, br, alpha):
    d = w_bf.shape[0]
    tm = _tile(math.gcd(grp.tp, grp.ls), 512)
    ntp = grp.tp // tm
    aargs, aspecs = _row_operand(a, tm, d, ntp, 1)
    lead_specs = aspecs + [
        pl.BlockSpec((1, d), lambda i: (0, 0)), pl.BlockSpec((1, d), lambda i: (0, 0)),
        pl.BlockSpec((d, d), lambda i: (0, 0), pipeline_mode=pl.Buffered(1)),
    ]
    kern = functools.partial(_pw2_ln_kernel, na=len(aargs), ntp=ntp, alpha=alpha)
    return _res_ln_call(kern, "conv_proj_ln", aargs + [cv_g.reshape(1, d), cv_b.reshape(1, d), w_bf], lead_specs,
                        x, t, modl, grp, 2, 4, 3, ln_g, ln_b, wr, br, tm)


def _route_kernel(lg_ref, tri_ref, idx_ref, gate_ref, cnt_ref, carry, *, ng, epg):
    @pl.when(pl.program_id(0) == 0)
    def _():
        carry[...] = jnp.zeros_like(carry)

    lg = lg_ref[...]
    lane = lax.broadcasted_iota(jnp.int32, lg.shape, 1)
    neg = -jnp.inf
    big = lg.shape[1]
    gmask = lane < ng
    lgm = jnp.where(gmask, lg, neg)
    mg = jnp.max(lgm, axis=-1, keepdims=True)
    g = jnp.min(jnp.where(lgm == mg, lane, big), axis=-1, keepdims=True)
    gate_g = 1.0 / jnp.sum(jnp.exp(lgm - mg), axis=-1, keepdims=True)
    el = lane - ng
    lo = g * epg
    emask = (el >= lo) & (el < lo + epg)
    le = jnp.where(emask, lg, neg)
    m1 = jnp.max(le, axis=-1, keepdims=True)
    i1 = jnp.min(jnp.where(le == m1, lane, big), axis=-1, keepdims=True)
    le2 = jnp.where(lane == i1, neg, le)
    m2 = jnp.max(le2, axis=-1, keepdims=True)
    i2 = jnp.min(jnp.where(le2 == m2, lane, big), axis=-1, keepdims=True)
    r = jnp.exp(m2 - m1)
    den = 1.0 + r
    g1 = gate_g / den
    g2 = gate_g * r / den
    e1 = i1 - ng
    e2 = i2 - ng
    oh1 = lane == e1
    oh2 = lane == e2
    oh1f = jnp.where(oh1, 1.0, 0.0)
    oh2f = jnp.where(oh2, 1.0, 0.0)
    p1 = jnp.dot(tri_ref[...], oh1f.astype(BF16), preferred_element_type=F32)
    p2 = jnp.dot(tri_ref[...], oh2f.astype(BF16), preferred_element_type=F32)
    c = carry[0:1, :]
    tot1 = jnp.sum(oh1f, axis=0, keepdims=True)
    tot2 = jnp.sum(oh2f, axis=0, keepdims=True)
    r1 = jnp.sum(jnp.where(oh1, p1 + c, 0.0), axis=-1, keepdims=True)
    r2 = jnp.sum(jnp.where(oh2, p2 + c + tot1, 0.0), axis=-1, keepdims=True)
    newc = c + tot1 + tot2
    carry[...] = jnp.broadcast_to(newc, carry.shape)
    cnt_ref[...] = jnp.broadcast_to(newc, cnt_ref.shape)
    zi = jnp.zeros_like(lane)
    idx_ref[...] = jnp.where(lane == 0, e1, jnp.where(lane == 1, e2, jnp.where(
        lane == 2, r1.astype(jnp.int32), jnp.where(lane == 3, r2.astype(jnp.int32), zi))))
    gate_ref[...] = jnp.where(lane == 0, g1, jnp.where(lane == 1, g2, 0.0))


def _route(lg, ng, epg):
    t, nr = lg.shape
    tm = _tile(t, 512)
    tri = jnp.asarray(np.tril(np.ones((tm, tm), np.float32), -1)).astype(BF16)
    kern = functools.partial(_route_kernel, ng=ng, epg=epg)
    return pl.pallas_call(
        kern,
        grid=(t // tm,),
        in_specs=[pl.BlockSpec((tm, nr), lambda i: (i, 0)), pl.BlockSpec((tm, tm), lambda i: (0, 0))],
        out_specs=[pl.BlockSpec((tm, nr), lambda i: (i, 0)), pl.BlockSpec((tm, nr), lambda i: (i, 0)),
                   pl.BlockSpec((SUBLANES, nr), lambda i: (0, 0))],
        out_shape=[jax.ShapeDtypeStruct((t, nr), jnp.int32), jax.ShapeDtypeStruct((t, nr), F32),
                   jax.ShapeDtypeStruct((SUBLANES, nr), F32)],
        scratch_shapes=[pltpu.VMEM((SUBLANES, nr), F32)],
        compiler_params=_cp("arbitrary"),
        name="moe_route",
    )(lg, tri)


def _ffn_kernel(be_ref, tok_ref, dst_ref, h_ref, wg_ref, wu_ref, wd_ref, o_ref, xb0, xb1, yb0, yb1, gsem, ssem,
                *, blk, nblk, nchunk):
    del be_ref
    b = pl.program_id(0)
    xbufs = (xb0, xb1)
    ybufs = (yb0, yb1)

    def gather_copy(src_row, s, r):
        return pltpu.make_async_copy(h_ref.at[pl.ds(src_row, 1)], xbufs[s].at[pl.ds(r, 1)], gsem)

    def scatter_copy(s, r, dst_row):
        return pltpu.make_async_copy(ybufs[s].at[pl.ds(r, 1)], o_ref.at[pl.ds(dst_row, 1)], ssem)

    def wait_rows(make_copy):
        def body(r, carry):
            make_copy(r).wait()
            return carry

        lax.fori_loop(0, blk, body, 0, unroll=8)

    @pl.when(b == 0)
    def _():
        yb0[...] = jnp.zeros_like(yb0)
        yb1[...] = jnp.zeros_like(yb1)

        def body(r, carry):
            gather_copy(tok_ref[r], 0, r).start()
            return carry

        lax.fori_loop(0, blk, body, 0)

    wait_rows(lambda r: gather_copy(0, 0, r))

    @pl.when(b >= 1)
    def _():
        wait_rows(lambda r: scatter_copy(0, r, 0))

    nxt = jnp.minimum(b + 1, nblk - 1) * blk
    prv = b * blk
    de = wg_ref.shape[1]
    hc = de // nchunk
    per = -(-blk // ((nchunk - 1) * SUBLANES)) * SUBLANES
    bounds = [min(c * per, blk) for c in range(nchunk)] + [blk] * 2

    def step(slot):
        other = 1 - slot
        x = xbufs[slot][...].astype(BF16)
        for c in range(nchunk):
            for r in range(bounds[c], bounds[c + 1] if c < nchunk - 1 else bounds[c]):
                gather_copy(tok_ref[nxt + r], other, r).start(priority=r % 2)
                scatter_copy(other, r, dst_ref[prv + r]).start(priority=r % 2)
            cols = slice(c * hc, (c + 1) * hc)
            gate = jnp.dot(x, wg_ref[:, cols], preferred_element_type=F32)
            up = jnp.dot(x, wu_ref[:, cols], preferred_element_type=F32)
            hid = (_silu(gate) * up).astype(BF16)
            part = jnp.dot(hid, wd_ref[cols, :], preferred_element_type=F32)
            if c == 0:
                ybufs[slot][...] = part
            else:
                ybufs[slot][...] = ybufs[slot][...] + part

    @pl.when(b % 2 == 0)
    def _():
        step(0)

    @pl.when(b % 2 == 1)
    def _():
        step(1)

    @pl.when(b == nblk)
    def _():
        wait_rows(lambda r: gather_copy(0, 0, r))
        wait_rows(lambda r: scatter_copy(0, r, 0))


def _ffn(blk_exp, tok, dstp, h, wg_bf, wu_bf, wd_bf, layer, nrows_out):
    d = h.shape[1]
    de = wg_bf.shape[3]
    nblk = tok.shape[0] // MOE_BLK
    kern = functools.partial(_ffn_kernel, blk=MOE_BLK, nblk=nblk, nchunk=2)
    wmap = lambda b, be, tk, ds_: (layer, be[jnp.minimum(b, nblk - 1)], 0, 0)
    buf = pltpu.VMEM((MOE_BLK, d), F32)
    return pl.pallas_call(
        kern,
        grid_spec=pltpu.PrefetchScalarGridSpec(
            num_scalar_prefetch=3,
            grid=(nblk + 1,),
            in_specs=[
                pl.BlockSpec(memory_space=pl.ANY),
                pl.BlockSpec((None, None, d, de), wmap),
                pl.BlockSpec((None, None, d, de), wmap),
                pl.BlockSpec((None, None, de, d), wmap),
            ],
            out_specs=pl.BlockSpec(memory_space=pl.ANY),
            scratch_shapes=[buf, buf, buf, buf, pltpu.SemaphoreType.DMA(()), pltpu.SemaphoreType.DMA(())],
        ),
        out_shape=jax.ShapeDtypeStruct((nrows_out, d), F32),
        compiler_params=_cp("arbitrary"),
        name="moe_ffn",
    )(blk_exp, tok, dstp, h, wg_bf, wu_bf, wd_bf)


def _combine_kernel(ya_ref, yb_ref, x_ref, gt_ref, g_ref, lng_ref, lnb_ref, *o_refs, alpha, ntp):
    gt = gt_ref[...]
    y = gt[:, 0:1] * ya_ref[...] + gt[:, 1:2] * yb_ref[...]
    val = _ln_rows(alpha * x_ref[...] + g_ref[...] * y, lng_ref[...], lnb_ref[...])
    if len(o_refs) == 1:
        o_refs[0][...] = val
    else:
        i = pl.program_id(0)

        @pl.when(i < ntp)
        def _():
            o_refs[0][...] = val

        @pl.when(i >= ntp)
        def _():
            o_refs[1][...] = val


def _combine(y2, x, gates, modl, grp, which_g, ln_g, ln_b, alpha, split):
    t, d = x.shape
    nr = gates.shape[1]
    tm = _tile(math.gcd(grp.tp, grp.ls), 256)
    ntp = grp.tp // tm
    kern = functools.partial(_combine_kernel, alpha=alpha, ntp=ntp)
    assert TOP_K == 2
    nt = t // tm
    if split:
        out_specs = [pl.BlockSpec((tm, d), lambda i: (jnp.minimum(i, ntp - 1), 0)),
                     pl.BlockSpec((tm, d), lambda i: (jnp.maximum(i - ntp, 0), 0))]
        out_shape = [jax.ShapeDtypeStruct((grp.tp, d), F32), jax.ShapeDtypeStruct((t - grp.tp, d), F32)]
    else:
        out_specs = pl.BlockSpec((tm, d), lambda i: (i, 0))
        out_shape = jax.ShapeDtypeStruct((t, d), F32)
    return pl.pallas_call(
        kern,
        grid=(t // tm,),
        in_specs=[
            pl.BlockSpec((tm, d), lambda i: (i, 0)),
            pl.BlockSpec((tm, d), lambda i: (nt + i, 0)),
            pl.BlockSpec((tm, d), lambda i: (i, 0)),
            pl.BlockSpec((tm, nr), lambda i: (i, 0)),
            grp.mod_spec(which_g, tm, d, 1),
            pl.BlockSpec((1, d), lambda i: (0, 0)),
            pl.BlockSpec((1, d), lambda i: (0, 0)),
        ],
        out_specs=out_specs,
        out_shape=out_shape,
        compiler_params=_cp("arbitrary"),
        name="moe_combine_ln",
    )(y2, y2, x, gates, modl, ln_g.reshape(1, d), ln_b.reshape(1, d))


def _moe(h, lg, x, modl, grp, ln_g, ln_b, wg_bf, wu_bf, wd_bf, layer, ng, alpha, split):
    t, d = h.shape
    ne = wg_bf.shape[1]
    na = t * TOP_K
    idx, gates, cnt = _route(lg, ng, ne // ng)
    experts = idx[:, 0:TOP_K]
    ranks = idx[:, TOP_K:2 * TOP_K]
    counts = cnt[0, :ne].astype(jnp.int32)
    padded = (counts + MOE_BLK - 1) // MOE_BLK * MOE_BLK
    pends = jnp.cumsum(padded)
    pstarts = pends - padded
    dest = (jnp.take(pstarts, experts) + ranks).reshape(-1).astype(jnp.int32)
    nblk = -(-na // MOE_BLK) + ne
    ns = nblk * MOE_BLK
    blk_start = jnp.arange(nblk, dtype=jnp.int32) * MOE_BLK
    blk_exp = jnp.minimum(jnp.sum(pends[None, :] <= blk_start[:, None], axis=1), ne - 1).astype(jnp.int32)
    code = jnp.full((ns,), -1, jnp.int32).at[dest].set(jnp.arange(na, dtype=jnp.int32), unique_indices=True)
    spare = na + jnp.arange(ns, dtype=jnp.int32) % MOE_BLK
    tok = jnp.where(code >= 0, code // TOP_K, 0)
    dst = jnp.where(code >= 0, (code % TOP_K) * t + code // TOP_K, spare)
    dstp = jnp.concatenate([na + jnp.arange(MOE_BLK, dtype=jnp.int32), dst])
    y2 = _ffn(blk_exp, tok, dstp, h, wg_bf, wu_bf, wd_bf, layer, na + MOE_BLK)
    return _combine(y2, x, gates, modl, grp, 5, ln_g, ln_b, alpha, split)


def _rope_tables(tp, nb, l):
    rows = l // GRID_W
    row = np.repeat(np.arange(rows, dtype=np.float32), GRID_W)
    col = np.tile(np.arange(GRID_W, dtype=np.float32), rows)
    inv_freq = (np.float32(ROPE_THETA) ** (-np.arange(ROPE_PAIRS, dtype=np.float32) / np.float32(ROPE_PAIRS)))
    ar = row[:, None] * inv_freq
    ac = col[:, None] * inv_freq
    ang = np.concatenate([ar, ar, ac, ac], axis=-1).astype(np.float64)
    cos = np.cos(ang)
    sin = np.sin(ang)
    sign = np.tile(np.concatenate([-np.ones(ROPE_PAIRS), np.ones(ROPE_PAIRS)]), 2)
    cos_all = np.concatenate([np.ones((tp, HEAD_DIM)), np.tile(cos, (nb, 1))], axis=0)
    sin_all = np.concatenate([np.zeros((tp, HEAD_DIM)), np.tile(sin * sign, (nb, 1))], axis=0)
    return jnp.asarray(cos_all.astype(np.float32)), jnp.asarray(sin_all.astype(np.float32))


def kernel(x_prompt, x_sample, cache_k, cache_v, c, c_ctx, w_ada, b_ada, ln_g, ln_b, w_in_ab, q_norm_g, k_norm_g, hy_short_w, hy_short_b, hy_f_w1, hy_f_b1, hy_f_w2, hy_f_b2, hy_f_w3, hy_f_b3, hy_sin_freq, hy_skip, w_out_ab, cv_pw1, cv_dw_w, cv_dw_b, cv_ln_g, cv_ln_b, cv_pw2, moe_w_grp, moe_b_grp, moe_w_exp, moe_b_exp, moe_w_gate, moe_w_up, moe_w_down):
    bp, lp, d = x_prompt.shape
    bs, ls, _ = x_sample.shape
    depth = w_ada.shape[0]
    tp, ts = bp * lp, bs * ls
    t = tp + ts
    kv_w = cache_k.shape[2] * HEAD_DIM
    hy_w = hy_skip.shape[2]
    att_w = d - hy_w
    ng = moe_w_grp.shape[2]
    ne = moe_w_exp.shape[2]
    alpha = (2 * depth) ** 0.25
    nmod = 6
    ngrp = 1 + bs
    gpad = -(-ngrp // SUBLANES) * SUBLANES
    grp = _Groups(tp, ls, nmod)

    x = jnp.concatenate([x_prompt.reshape(tp, d), x_sample.reshape(ts, d)], axis=0)
    cond = jnp.zeros((gpad, d), F32).at[0].set(c_ctx).at[1:ngrp].set(c)
    mod = _ada(cond, w_ada, b_ada)
    nrt = LANES
    wg_bf, wu_bf, wd_bf = moe_w_gate.astype(BF16), moe_w_up.astype(BF16), moe_w_down.astype(BF16)
    new_k, new_v = None, None
    for layer in range(depth):
        i = layer // 2
        modl = mod[layer].reshape(gpad * nmod, 1, d)
        wr = jnp.zeros((d, nrt), F32).at[:, :ng].set(moe_w_grp[layer]).at[:, ng:ng + ne].set(moe_w_exp[layer])
        wr_hi = wr.astype(BF16)
        wr = jnp.concatenate([wr_hi, (wr - wr_hi.astype(F32)).astype(BF16)], axis=0)
        br = jnp.zeros((1, nrt), F32).at[0, :ng].set(moe_b_grp[layer]).at[0, ng:ng + ne].set(moe_b_exp[layer])
        if layer % 2 == 0:
            zp = _mm_mod(x, t, modl, grp, 1, 0, w_in_ab[i].astype(BF16))
            cos_t, sin_t = _rope_tables(tp, bs, ls)
            q_bf, k_f, k_bf = _qkprep(zp, cos_t, sin_t, q_norm_g[i], k_norm_g[i], att_w, kv_w)
            o_p = _attention(q_bf, k_bf, zp, 0, bp, lp, att_w, kv_w)
            o_s = _attention(q_bf, k_bf, zp, tp, bs, ls, att_w, kv_w, cache_k, cache_v, i)
            ucol = att_w + 2 * kv_w
            ys = []
            for row0, nb, l in ((0, bp, lp), (tp, bs, ls)):
                fc = _fft_consts(l)
                kf = _hyena_filters(l, fc, hy_f_w1[i], hy_f_b1[i], hy_f_w2[i], hy_f_b2[i], hy_f_w3[i], hy_f_b3[i],
                                    hy_sin_freq[i], hy_w)
                usc = _dwconv(zp, ucol, 3 * hy_w, hy_short_w[i], hy_short_b[i], row0, nb, l)
                z1 = _lconv(usc, 0, 0, usc, hy_w, 0, kf, 0, hy_skip[i, 0], fc, nb, l, hy_w)
                ys.append(_lconv(z1, 0, 0, usc, 2 * hy_w, 0, kf, 1, hy_skip[i, 1], fc, nb, l, hy_w))
            x, h, lg = _out_ln((o_p, o_s), tuple(ys), w_out_ab[i].astype(BF16), x, t, modl, grp,
                               ln_g[layer, 0], ln_b[layer, 0], wr, br, alpha)
            nkv = kv_w // HEAD_DIM
            kk = k_f[:tp].reshape(bp, lp, nkv, HEAD_DIM).transpose(0, 2, 1, 3)
            vv = zp[:tp, att_w + kv_w:att_w + 2 * kv_w].reshape(bp, lp, nkv, HEAD_DIM).transpose(0, 2, 1, 3)
            new_k = kk if new_k is None else jnp.concatenate([new_k, kk], axis=1)
            new_v = vv if new_v is None else jnp.concatenate([new_v, vv], axis=1)
        else:
            a = _mm_mod(x, t, modl, grp, 1, 0, cv_pw1[i].astype(BF16), glu=True)
            a = (_dwconv(a, 0, d, cv_dw_w[i], cv_dw_b[i], 0, bp, lp),
                 _dwconv(a, 0, d, cv_dw_w[i], cv_dw_b[i], tp, bs, ls))
            x, h, lg = _pw2_ln(a, cv_ln_g[i], cv_ln_b[i], cv_pw2[i].astype(BF16), x, t, modl, grp,
                               ln_g[layer, 0], ln_b[layer, 0], wr, br, alpha)
        x = _moe(h, lg, x, modl, grp, ln_g[layer, 1], ln_b[layer, 1], wg_bf, wu_bf, wd_bf, layer, ng, alpha,
                 layer == depth - 1)
    nkv = kv_w // HEAD_DIM
    n_attn = (depth + 1) // 2
    new_k = new_k.reshape(bp, n_attn, nkv, lp, HEAD_DIM)
    new_v = new_v.reshape(bp, n_attn, nkv, lp, HEAD_DIM)
    return (x[0].reshape(bp, lp, d), x[1].reshape(bs, ls, d), new_k, new_v)
```

```python
import functools
import math

import jax
import jax.numpy as jnp
import numpy as np
from jax import lax
from jax.experimental import pallas as pl
from jax.experimental.pallas import tpu as pltpu

F32 = jnp.float32
BF16 = jnp.bfloat16
HIGHEST = lax.Precision.HIGHEST

GRID_W = 64
HEAD_DIM = 128
ROPE_THETA = 10000.0
ROPE_PAIRS = HEAD_DIM // 4
HY_BANDS = 16
HY_TARGET = 1e-2
HY_FAST = 0.3
HY_SLOW = 1.5
TOP_K = 2
EPS = 1e-6

LANES = 128
SUBLANES = 8
VMEM_LIMIT = 56 * 1024 * 1024
MOE_BLK = 256
FFT_SPLIT = 64
FFT_DENSE_MAX = 1024
FFT_UNROLL = 8
ATTN_PARTS = 2


def _cp(*sem):
    return pltpu.CompilerParams(dimension_semantics=sem, vmem_limit_bytes=VMEM_LIMIT)


def _tile(n, target, mult=SUBLANES):
    if n <= target:
        return n
    for t in range(target, 0, -1):
        if n % t == 0 and t % mult == 0:
            return t
    return n


def _silu(x):
    return x * jax.nn.sigmoid(x)


def _ln_rows(x, g, b):
    mu = jnp.mean(x, axis=-1, keepdims=True)
    xc = x - mu
    var = jnp.mean(xc * xc, axis=-1, keepdims=True)
    return xc * lax.rsqrt(var + EPS) * g + b


def _ada_kernel(c_ref, w_ref, b_ref, o_ref):
    s = _silu(c_ref[...])
    o_ref[...] = jnp.dot(s, w_ref[...], precision=HIGHEST, preferred_element_type=F32) + b_ref[...]


def _ada(cond, w_ada, b_ada):
    depth, d, n = w_ada.shape
    g = cond.shape[0]
    tn = _tile(n, 1024, LANES)
    return pl.pallas_call(
        _ada_kernel,
        grid=(depth, n // tn),
        in_specs=[
            pl.BlockSpec((g, d), lambda l, j: (0, 0)),
            pl.BlockSpec((None, d, tn), lambda l, j: (l, 0, j)),
            pl.BlockSpec((None, 1, tn), lambda l, j: (l, 0, j)),
        ],
        out_specs=pl.BlockSpec((None, g, tn), lambda l, j: (l, 0, j)),
        out_shape=jax.ShapeDtypeStruct((depth, g, n), F32),
        compiler_params=_cp("arbitrary", "arbitrary"),
        name="ada_mod",
    )(cond, w_ada, b_ada.reshape(depth, 1, n))


def _load_rows(refs, ntp):
    if len(refs) == 1:
        return refs[0][...]
    return jnp.where(pl.program_id(0) < ntp, refs[0][...], refs[1][...])


def _row_operand(arr, tm, cols, ntp, nargs):
    if isinstance(arr, tuple):
        if nargs == 2:
            maps = (lambda i, j: (jnp.minimum(i, ntp - 1), 0), lambda i, j: (jnp.maximum(i - ntp, 0), 0))
        else:
            maps = (lambda i: (jnp.minimum(i, ntp - 1), 0), lambda i: (jnp.maximum(i - ntp, 0), 0))
        return list(arr), [pl.BlockSpec((tm, cols), m) for m in maps]
    one = (lambda i, j: (i, 0)) if nargs == 2 else (lambda i: (i, 0))
    return [arr], [pl.BlockSpec((tm, cols), one)]


def _mm_mod_kernel(*refs, nx, ntp):
    sc_ref, sh_ref, w_ref, o_ref, h_scr = refs[nx:]

    @pl.when(pl.program_id(1) == 0)
    def _():
        h_scr[...] = (_load_rows(refs[:nx], ntp) * (1.0 + sc_ref[...]) + sh_ref[...]).astype(BF16)

    o_ref[...] = jnp.dot(h_scr[...], w_ref[...], preferred_element_type=F32)


def _mm_glu_kernel(*refs, nx, ntp):
    sc_ref, sh_ref, wa_ref, wg_ref, o_ref, h_scr = refs[nx:]

    @pl.when(pl.program_id(1) == 0)
    def _():
        h_scr[...] = (_load_rows(refs[:nx], ntp) * (1.0 + sc_ref[...]) + sh_ref[...]).astype(BF16)

    a = jnp.dot(h_scr[...], wa_ref[...], preferred_element_type=F32)
    g = jnp.dot(h_scr[...], wg_ref[...], preferred_element_type=F32)
    o_ref[...] = a * jax.nn.sigmoid(g)


class _Groups:
    def __init__(self, tp, ls, nmod):
        self.tp, self.ls, self.nmod = tp, ls, nmod

    def gid(self, i, tm):
        row = i * tm
        return jnp.where(row < self.tp, 0, 1 + (row - self.tp) // self.ls)

    def mod_spec(self, which, tm, d, nargs=2):
        if nargs == 2:
            return pl.BlockSpec((None, 1, d), lambda i, j: (self.gid(i, tm) * self.nmod + which, 0, 0))
        return pl.BlockSpec((None, 1, d), lambda i: (self.gid(i, tm) * self.nmod + which, 0, 0))


def _mm_mod(x, t, modl, grp, which_sc, which_sh, w_bf, glu=False):
    d = w_bf.shape[0]
    n = w_bf.shape[1] // (2 if glu else 1)
    tm = _tile(math.gcd(grp.tp, grp.ls), 1024)
    tn = _tile(n, 512, LANES)
    ntp = grp.tp // tm
    args, specs = _row_operand(x, tm, d, ntp, 2)
    nx = len(args)
    specs += [
        grp.mod_spec(which_sc, tm, d),
        grp.mod_spec(which_sh, tm, d),
        pl.BlockSpec((d, tn), lambda i, j: (0, j)),
    ]
    args += [modl, modl, w_bf]
    if glu:
        noff = n // tn
        specs.append(pl.BlockSpec((d, tn), lambda i, j: (0, j + noff)))
        args.append(w_bf)
    return pl.pallas_call(
        functools.partial(_mm_glu_kernel if glu else _mm_mod_kernel, nx=nx, ntp=ntp),
        grid=(t // tm, n // tn),
        in_specs=specs,
        out_specs=pl.BlockSpec((tm, tn), lambda i, j: (i, j)),
        out_shape=jax.ShapeDtypeStruct((t, n), F32),
        scratch_shapes=[pltpu.VMEM((tm, d), BF16)],
        compiler_params=_cp("arbitrary", "arbitrary"),
        name="mm_glu" if glu else "mm_mod",
    )(*args)


def _qkprep_kernel(q_ref, k_ref, c_ref, s_ref, qg_ref, kg_ref, qo_ref, ko_ref, kb_ref, *, nq, nk, scale):
    c = c_ref[...]
    s = s_ref[...]
    lane = lax.broadcasted_iota(jnp.int32, c.shape, 1)
    first = (lane % (2 * ROPE_PAIRS)) < ROPE_PAIRS

    def prep(xh, g):
        ms = jnp.mean(xh * xh, axis=-1, keepdims=True)
        xn = xh * lax.rsqrt(ms + EPS) * g
        rot = jnp.where(first, pltpu.roll(xn, HEAD_DIM - ROPE_PAIRS, 1), pltpu.roll(xn, ROPE_PAIRS, 1))
        return xn * c + rot * s

    for h in range(nq):
        sl = slice(h * HEAD_DIM, (h + 1) * HEAD_DIM)
        qo_ref[:, sl] = (prep(q_ref[:, sl], qg_ref[...]) * scale).astype(BF16)
    for h in range(nk):
        sl = slice(h * HEAD_DIM, (h + 1) * HEAD_DIM)
        kh = prep(k_ref[:, sl], kg_ref[...])
        ko_ref[:, sl] = kh
        kb_ref[:, sl] = kh.astype(BF16)


def _qkprep(zp, cos_t, sin_t, q_g, k_g, att_w, kv_w):
    t = zp.shape[0]
    tm = _tile(t, 512)
    nq, nk = att_w // HEAD_DIM, kv_w // HEAD_DIM
    assert att_w % kv_w == 0
    kern = functools.partial(_qkprep_kernel, nq=nq, nk=nk, scale=HEAD_DIM ** -0.5)
    return pl.pallas_call(
        kern,
        grid=(t // tm,),
        in_specs=[
            pl.BlockSpec((tm, att_w), lambda i: (i, 0)),
            pl.BlockSpec((tm, kv_w), lambda i: (i, att_w // kv_w)),
            pl.BlockSpec((tm, HEAD_DIM), lambda i: (i, 0)),
            pl.BlockSpec((tm, HEAD_DIM), lambda i: (i, 0)),
            pl.BlockSpec((1, HEAD_DIM), lambda i: (0, 0)),
            pl.BlockSpec((1, HEAD_DIM), lambda i: (0, 0)),
        ],
        out_specs=[
            pl.BlockSpec((tm, att_w), lambda i: (i, 0)),
            pl.BlockSpec((tm, kv_w), lambda i: (i, 0)),
            pl.BlockSpec((tm, kv_w), lambda i: (i, 0)),
        ],
        out_shape=[
            jax.ShapeDtypeStruct((t, att_w), BF16),
            jax.ShapeDtypeStruct((t, kv_w), F32),
            jax.ShapeDtypeStruct((t, kv_w), BF16),
        ],
        compiler_params=_cp("arbitrary"),
        name="qk_prep",
    )(zp, zp, cos_t, sin_t, q_g.reshape(1, HEAD_DIM), k_g.reshape(1, HEAD_DIM))


def _attn_kernel(*refs, has_cache, g, tq):
    if has_cache:
        q_ref, k_ref, v_ref, kc_ref, vc_ref, o_ref = refs
    else:
        q_ref, k_ref, v_ref, o_ref = refs
    nt = (((1,), (1,)), ((), ()))
    nparts = ATTN_PARTS if g % ATTN_PARTS == 0 else 1
    hp = g // nparts
    for part in range(nparts):
        heads = range(part * hp, (part + 1) * hp)
        qs = jnp.concatenate([q_ref[:, h * HEAD_DIM:(h + 1) * HEAD_DIM] for h in heads], axis=0)
        s = lax.dot_general(qs, k_ref[...], nt, preferred_element_type=F32)
        m = jnp.max(s, axis=-1, keepdims=True)
        if has_cache:
            sc = lax.dot_general(qs, kc_ref[...].astype(BF16), nt, preferred_element_type=F32)
            m = jnp.maximum(m, jnp.max(sc, axis=-1, keepdims=True))
        p = jnp.exp(s - m)
        l = jnp.sum(p, axis=-1, keepdims=True)
        o = jnp.dot(p.astype(BF16), v_ref[...].astype(BF16), preferred_element_type=F32)
        if has_cache:
            pc = jnp.exp(sc - m)
            l = l + jnp.sum(pc, axis=-1, keepdims=True)
            o = o + jnp.dot(pc.astype(BF16), vc_ref[...].astype(BF16), preferred_element_type=F32)
        o = o / l
        for i, h in enumerate(heads):
            o_ref[:, h * HEAD_DIM:(h + 1) * HEAD_DIM] = o[i * tq:(i + 1) * tq].astype(BF16)


def _attention(q_bf, k_bf, zp, row0, nb, l, att_w, kv_w, cache_k=None, cache_v=None, layer=0):
    nkv = kv_w // HEAD_DIM
    g = att_w // kv_w
    tq = _tile(l, max(SUBLANES, 512 // g))
    rb = row0 // l
    assert row0 % l == 0
    vcol = (att_w + kv_w) // HEAD_DIM
    has_cache = cache_k is not None
    specs = [
        pl.BlockSpec((tq, g * HEAD_DIM), lambda b, h, i: ((row0 + b * l) // tq + i, h)),
        pl.BlockSpec((l, HEAD_DIM), lambda b, h, i: (rb + b, h)),
        pl.BlockSpec((l, HEAD_DIM), lambda b, h, i: (rb + b, vcol + h)),
    ]
    args = [q_bf, k_bf, zp]
    if has_cache:
        past = cache_k.shape[3]
        cspec = pl.BlockSpec((None, None, None, past, HEAD_DIM), lambda b, h, i: (b, layer, h, 0, 0))
        specs += [cspec, cspec]
        args += [cache_k, cache_v]
    kern = functools.partial(_attn_kernel, has_cache=has_cache, g=g, tq=tq)
    return pl.pallas_call(
        kern,
        grid=(nb, nkv, l // tq),
        in_specs=specs,
        out_specs=pl.BlockSpec((tq, g * HEAD_DIM), lambda b, h, i: (b * (l // tq) + i, h)),
        out_shape=jax.ShapeDtypeStruct((nb * l, att_w), BF16),
        compiler_params=_cp("arbitrary", "arbitrary", "arbitrary"),
        name="attn_cache" if has_cache else "attn",
    )(*args)


def _dwconv_kernel(x_ref, w_ref, b_ref, o_ref, pad, *, l, width, rc, off):
    ct = x_ref.shape[1]
    zeros = jnp.zeros((off, ct), F32)
    pad[0:off, :] = zeros
    pad[off + l:off + l + off, :] = zeros
    pad[off:off + l, :] = x_ref[...]
    bias = b_ref[...]
    shift = off - width // 2

    def chunk(ci, carry):
        base = pl.multiple_of(ci * rc, rc)
        win = pad[pl.ds(base, rc + 2 * off), :]
        acc = jnp.zeros((rc, ct), F32) + bias
        span = rc + 2 * off - SUBLANES
        for p in range(SUBLANES):
            taps = [k for k in range(width) if (k + shift) % SUBLANES == p]
            if taps:
                shifted = win[p:p + span, :]
                for k in taps:
                    o = k + shift - p
                    wk = jnp.concatenate([w_ref[SUBLANES * k:SUBLANES * (k + 1), :]] * (rc // SUBLANES), axis=0)
                    acc = acc + wk * shifted[o:o + rc, :]
        o_ref[pl.ds(base, rc), :] = acc
        return carry

    lax.fori_loop(0, l // rc, chunk, 0)


def _dwconv(x, col0, ncols, w, b, row0, nb, l):
    width = w.shape[0]
    off = 16
    assert width // 2 <= off
    ct = LANES if width > SUBLANES else _tile(ncols, 2 * LANES, LANES)
    rc = _tile(l, 64)
    assert rc % SUBLANES == 0
    wp = jnp.broadcast_to(w[:, None, :], (width, SUBLANES, ncols)).reshape(width * SUBLANES, ncols)
    assert col0 % ct == 0 and row0 % l == 0
    kern = functools.partial(_dwconv_kernel, l=l, width=width, rc=rc, off=off)
    return pl.pallas_call(
        kern,
        grid=(nb, ncols // ct),
        in_specs=[
            pl.BlockSpec((l, ct), lambda b_, j: (row0 // l + b_, col0 // ct + j)),
            pl.BlockSpec((width * SUBLANES, ct), lambda b_, j: (0, j)),
            pl.BlockSpec((1, ct), lambda b_, j: (0, j)),
        ],
        out_specs=pl.BlockSpec((l, ct), lambda b_, j: (b_, j)),
        out_shape=jax.ShapeDtypeStruct((nb * l, ncols), F32),
        scratch_shapes=[pltpu.VMEM((l + 2 * off, ct), F32)],
        compiler_params=_cp("arbitrary", "arbitrary"),
        name="dwconv%d" % width,
    )(x, wp, b.reshape(1, ncols))


def _fft_plan(l):
    n = 2 * l
    if n <= FFT_DENSE_MAX or n % FFT_SPLIT != 0:
        return n, 1
    return n // FFT_SPLIT, FFT_SPLIT


def _stack3_lhs(m):
    hi = m.astype(np.float32).astype(jnp.bfloat16)
    lo = (m - np.asarray(hi.astype(np.float32), np.float64)).astype(np.float32).astype(jnp.bfloat16)
    return jnp.asarray(np.concatenate([hi, lo, hi], axis=1))


def _split3(x):
    hi = x.astype(BF16)
    lo = (x - hi.astype(F32)).astype(BF16)
    return jnp.concatenate([hi, hi, lo], axis=0)


def _mm3(lhs3, x):
    return jnp.dot(lhs3, _split3(x), preferred_element_type=F32)


def _fft_consts(l):
    n1, n2 = _fft_plan(l)
    n = n1 * n2
    n1h = n1 // 2
    k1 = np.arange(n1, dtype=np.float64)[:, None]
    ang = 2.0 * np.pi * k1 * np.arange(n1h, dtype=np.float64)[None, :] / n1
    fr, fi = np.cos(ang), -np.sin(ang)
    a_fwd = np.block([[fr, -fi], [fi, fr]])
    ang = 2.0 * np.pi * k1 * np.arange(n1, dtype=np.float64)[None, :] / n1
    a_real = np.concatenate([np.cos(ang), -np.sin(ang)], axis=0)
    angi = 2.0 * np.pi * np.arange(n1h, dtype=np.float64)[:, None] * np.arange(n1, dtype=np.float64)[None, :] / n1
    cr, ci = np.cos(angi) / n, np.sin(angi) / n
    a_inv = np.block([[cr, -ci], [ci, cr]])
    k2 = np.arange(n2, dtype=np.float64)
    angb = 2.0 * np.pi * k2[:, None] * k2[None, :] / n2
    gr, gi = np.cos(angb), -np.sin(angb)
    b_fwd = np.block([[gr, -gi], [gi, gr]])
    b_inv = np.block([[gr, gi], [-gi, gr]])
    angt = 2.0 * np.pi * (np.arange(n1, dtype=np.float64)[:, None] * k2[None, :]).reshape(n, 1) / n
    twr = np.broadcast_to(np.cos(angt), (n, LANES)).astype(np.float32)
    twi = np.broadcast_to(-np.sin(angt), (n, LANES)).astype(np.float32)
    return dict(n1=n1, n2=n2, a_fwd=_stack3_lhs(a_fwd), a_real=_stack3_lhs(a_real), a_inv=_stack3_lhs(a_inv),
                b_fwd=_stack3_lhs(b_fwd), b_inv=_stack3_lhs(b_inv), twr=jnp.asarray(twr), twi=jnp.asarray(twi))


def _rows(ref, start, size, stride):
    if stride == 1:
        return ref[pl.ds(start, size), :]
    return ref[pl.ds(start, size, stride=stride), :]


def _stage_b_fwd(pr, pi, twr_ref, twi_ref, bfwd_ref, k1, n1, n2):
    xr = _rows(pr, k1, n2, n1)
    xi = _rows(pi, k1, n2, n1)
    r0 = pl.multiple_of(k1 * n2, n2)
    cr = twr_ref[pl.ds(r0, n2), :]
    ci = twi_ref[pl.ds(r0, n2), :]
    ar = xr * cr - xi * ci
    ai = xr * ci + xi * cr
    x = _mm3(bfwd_ref[...], jnp.concatenate([ar, ai], axis=0))
    return x[:n2], x[n2:], cr, ci


def _lconv_kernel(a_ref, g_ref, kf_ref, skip_ref, afwd_ref, ainv_ref, bfwd_ref, binv_ref, twr_ref, twi_ref,
                  o_ref, pr, pi, qr, qi, *, l, n1, n2):
    n1h = n1 // 2

    def stage_a(i2, carry):
        slab = jnp.concatenate([_rows(a_ref, i2, n1h, n2), _rows(a_ref, l + i2, n1h, n2)], axis=0)
        out = _mm3(afwd_ref[...], slab)
        r0 = pl.multiple_of(i2 * n1, n1)
        pr[pl.ds(r0, n1), :] = out[:n1]
        pi[pl.ds(r0, n1), :] = out[n1:]
        return carry

    lax.fori_loop(0, n2, stage_a, 0, unroll=min(n2, FFT_UNROLL))

    if n2 == 1:
        kr, ki = kf_ref[0], kf_ref[1]
        xr, xi = pr[...], pi[...]
        qr[...] = xr * kr - xi * ki
        qi[...] = xr * ki + xi * kr
    else:
        def stage_b(k1, carry):
            xr, xi, cr, ci = _stage_b_fwd(pr, pi, twr_ref, twi_ref, bfwd_ref, k1, n1, n2)
            r0 = pl.multiple_of(k1 * n2, n2)
            kr = kf_ref[0, pl.ds(r0, n2), :]
            ki = kf_ref[1, pl.ds(r0, n2), :]
            yr = xr * kr - xi * ki
            yi = xr * ki + xi * kr
            bv = _mm3(binv_ref[...], jnp.concatenate([yr, yi], axis=0))
            br, bi = bv[:n2], bv[n2:]
            qr[pl.ds(r0, n2), :] = br * cr + bi * ci
            qi[pl.ds(r0, n2), :] = bi * cr - br * ci
            return carry

        lax.fori_loop(0, n1, stage_b, 0, unroll=FFT_UNROLL)

    skip = skip_ref[...]

    def stage_c(i2, carry):
        slab = jnp.concatenate([_rows(qr, i2, n1, n2), _rows(qi, i2, n1, n2)], axis=0)
        y = _mm3(ainv_ref[...], slab)
        for half in range(2):
            start = half * l + i2
            a = _rows(a_ref, start, n1h, n2)
            g = _rows(g_ref, start, n1h, n2)
            val = g * (y[half * n1h:(half + 1) * n1h] + skip * a)
            if n2 == 1:
                o_ref[pl.ds(start, n1h), :] = val
            else:
                o_ref[pl.ds(start, n1h, stride=n2), :] = val
        return carry

    lax.fori_loop(0, n2, stage_c, 0, unroll=min(n2, FFT_UNROLL))


def _const_spec(shape):
    zeros = (0,) * len(shape)
    return pl.BlockSpec(shape, lambda a, b: zeros, pipeline_mode=pl.Buffered(1))


def _lconv(a, a_col0, a_row0, gate, g_col0, g_row0, kf, order, skip, fc, nb, l, c):
    n1, n2 = fc["n1"], fc["n2"]
    n = n1 * n2
    ct = LANES
    assert nb % 2 == 0 and a_row0 % (2 * l) == 0 and g_row0 % (2 * l) == 0
    single = pl.Buffered(1) if n2 > 1 else None
    kern = functools.partial(_lconv_kernel, l=l, n1=n1, n2=n2)
    return pl.pallas_call(
        kern,
        grid=(c // ct, nb // 2),
        in_specs=[
            pl.BlockSpec((2 * l, ct), lambda j, p: (a_row0 // (2 * l) + p, a_col0 // ct + j), pipeline_mode=single),
            pl.BlockSpec((2 * l, ct), lambda j, p: (g_row0 // (2 * l) + p, g_col0 // ct + j), pipeline_mode=single),
            pl.BlockSpec((None, 2, n, ct), lambda j, p: (order, 0, 0, j), pipeline_mode=single),
            pl.BlockSpec((1, ct), lambda j, p: (0, j)),
            _const_spec(fc["a_fwd"].shape), _const_spec(fc["a_inv"].shape),
            _const_spec(fc["b_fwd"].shape), _const_spec(fc["b_inv"].shape),
            _const_spec((n, LANES)), _const_spec((n, LANES)),
        ],
        out_specs=pl.BlockSpec((2 * l, ct), lambda j, p: (p, j), pipeline_mode=single),
        out_shape=jax.ShapeDtypeStruct((nb * l, c), F32),
        scratch_shapes=[pltpu.VMEM((n, ct), F32) for _ in range(4)],
        compiler_params=_cp("arbitrary", "arbitrary"),
        name="hyena_lconv",
    )(a, gate, kf, skip.reshape(1, c), fc["a_fwd"], fc["a_inv"], fc["b_fwd"], fc["b_inv"], fc["twr"], fc["twi"])


def _hyfilt_kernel(zz_ref, tt_ref, w1_ref, b1_ref, w2_ref, b2_ref, sf_ref, w3f_ref, w3b_ref, b3f_ref, b3b_ref,
                   dl_ref, areal_ref, bfwd_ref, twr_ref, twi_ref, o_ref, h_scr, kern_scr, pr, pi, *, l, n1, n2):
    rc = _tile(l, 256)
    nch = l // rc

    @pl.when((pl.program_id(0) == 0) & (pl.program_id(1) == 0))
    def _():
        def mlp(ci, carry):
            r0 = pl.multiple_of(ci * rc, rc)
            h = jnp.dot(zz_ref[pl.ds(r0, rc), :], w1_ref[...], precision=HIGHEST, preferred_element_type=F32)
            h = jnp.sin(sf_ref[0:1, :] * (h + b1_ref[...]))
            h = jnp.dot(h, w2_ref[...], precision=HIGHEST, preferred_element_type=F32) + b2_ref[...]
            h_scr[pl.ds(r0, rc), :] = jnp.sin(sf_ref[1:2, :] * h)
            return carry

        lax.fori_loop(0, 2 * nch, mlp, 0)

    def filt(w3_ref, b3_ref, first_chunk):
        def body(ci, acc):
            r0 = pl.multiple_of((first_chunk + ci) * rc, rc)
            hk = jnp.dot(h_scr[pl.ds(r0, rc), :], w3_ref[...], precision=HIGHEST, preferred_element_type=F32)
            win = jnp.exp(-tt_ref[pl.ds(r0, rc), :] * dl_ref[...])
            row = r0 + lax.broadcasted_iota(jnp.int32, (rc, 1), 0)
            k = jnp.where(row == l, 0.0, (hk + b3_ref[...]) * win)
            kern_scr[pl.ds(r0, rc), :] = k
            return acc + jnp.sum(jnp.abs(k), axis=0, keepdims=True)
        return body

    zero = jnp.zeros((1, kern_scr.shape[1]), F32)
    total = lax.fori_loop(0, nch, filt(w3f_ref, b3f_ref, 0), zero)
    total = lax.fori_loop(0, nch, filt(w3b_ref, b3b_ref, nch), total)
    inv = 1.0 / total

    def stage_a(i2, carry):
        out = _mm3(areal_ref[...], _rows(kern_scr, i2, n1, n2) * inv)
        r0 = pl.multiple_of(i2 * n1, n1)
        pr[pl.ds(r0, n1), :] = out[:n1]
        pi[pl.ds(r0, n1), :] = out[n1:]
        return carry

    lax.fori_loop(0, n2, stage_a, 0, unroll=min(n2, FFT_UNROLL))

    if n2 == 1:
        o_ref[0] = pr[...]
        o_ref[1] = pi[...]
    else:
        def stage_b(k1, carry):
            xr, xi, _, _ = _stage_b_fwd(pr, pi, twr_ref, twi_ref, bfwd_ref, k1, n1, n2)
            r0 = pl.multiple_of(k1 * n2, n2)
            o_ref[0, pl.ds(r0, n2), :] = xr
            o_ref[1, pl.ds(r0, n2), :] = xi
            return carry

        lax.fori_loop(0, n1, stage_b, 0, unroll=FFT_UNROLL)


def _hyena_filters(l, fc, w1, b1, w2, b2, w3, b3, sin_freq, c):
    n1, n2 = fc["n1"], fc["n2"]
    n = 2 * l
    order = w3.shape[1] // (2 * c)
    hid = w1.shape[1]
    ct = LANES
    pos = np.concatenate([np.arange(l), [0], np.arange(l - 1, 0, -1)]).astype(np.float32)
    t01 = np.linspace(0.0, 1.0, l, dtype=np.float32)[pos.astype(np.int64)]
    bands = np.linspace(1e-4, HY_BANDS - 1, HY_BANDS, dtype=np.float32)
    ang = (np.float32(2.0 * math.pi) * pos / np.float32(l))[:, None] * bands
    emb = 1 + 2 * HY_BANDS
    feat = np.zeros((n, hid), np.float32)
    feat[:, :emb] = np.concatenate([t01[:, None], np.cos(ang), -np.sin(ang)], axis=-1)
    w1p = jnp.zeros((hid, hid), F32).at[:emb].set(w1)
    tt = jnp.asarray(np.broadcast_to(t01[:, None], (n, LANES)).copy())
    dmin = math.log(HY_TARGET) / HY_SLOW
    dmax = math.log(HY_TARGET) / HY_FAST
    deltas = jnp.abs(jnp.linspace(dmin, dmax, c, dtype=F32)).reshape(1, c)
    cb = c // ct
    kern = functools.partial(_hyfilt_kernel, l=l, n1=n1, n2=n2)
    return pl.pallas_call(
        kern,
        grid=(order, cb),
        in_specs=[
            _const_spec((n, hid)), _const_spec((n, LANES)),
            _const_spec((hid, hid)), _const_spec((1, hid)), _const_spec((hid, hid)), _const_spec((1, hid)),
            _const_spec((2, hid)),
            pl.BlockSpec((hid, ct), lambda o, j: (0, (2 * o) * cb + j)),
            pl.BlockSpec((hid, ct), lambda o, j: (0, (2 * o + 1) * cb + j)),
            pl.BlockSpec((1, ct), lambda o, j: (0, (2 * o) * cb + j)),
            pl.BlockSpec((1, ct), lambda o, j: (0, (2 * o + 1) * cb + j)),
            pl.BlockSpec((1, ct), lambda o, j: (0, j)),
            _const_spec(fc["a_real"].shape), _const_spec(fc["b_fwd"].shape),
            _const_spec((n, LANES)), _const_spec((n, LANES)),
        ],
        out_specs=pl.BlockSpec((None, 2, n, ct), lambda o, j: (o, 0, 0, j), pipeline_mode=pl.Buffered(1)),
        out_shape=jax.ShapeDtypeStruct((order, 2, n, c), F32),
        scratch_shapes=[pltpu.VMEM((n, hid), F32), pltpu.VMEM((n, ct), F32), pltpu.VMEM((n, ct), F32),
                        pltpu.VMEM((n, ct), F32)],
        compiler_params=_cp("arbitrary", "arbitrary"),
        name="hyena_filters",
    )(jnp.asarray(feat), tt, w1p, b1.reshape(1, hid), w2, b2.reshape(1, hid), sin_freq, w3, w3,
      b3.reshape(1, -1), b3.reshape(1, -1), deltas, fc["a_real"], fc["b_fwd"], fc["twr"], fc["twi"])


def _res_ln_epilogue(acc, x, g_ref, lng_ref, lnb_ref, sc_ref, sh_ref, wr_ref, br_ref, xo_ref, h_ref, lg_ref, alpha):
    xn = _ln_rows(alpha * x + g_ref[...] * acc, lng_ref[...], lnb_ref[...])
    xo_ref[...] = xn
    h = xn * (1.0 + sc_ref[...]) + sh_ref[...]
    h_ref[...] = h
    d = h.shape[1]
    hh = h.astype(BF16)
    hl = (h - hh.astype(F32)).astype(BF16)
    lg = jnp.dot(hh, wr_ref[0:d, :], preferred_element_type=F32)
    lg = lg + jnp.dot(hh, wr_ref[d:2 * d, :], preferred_element_type=F32)
    lg = lg + jnp.dot(hl, wr_ref[0:d, :], preferred_element_type=F32)
    lg_ref[...] = lg + br_ref[...]


def _out_ln_kernel(*refs, no, ny, nx, ntp, alpha):
    o = _load_rows(refs[:no], ntp)
    y = _load_rows(refs[no:no + ny], ntp)
    w1_ref, w2_ref = refs[no + ny:no + ny + 2]
    rest = refs[no + ny + 2:]
    acc = jnp.dot(o, w1_ref[...], preferred_element_type=F32)
    acc = acc + jnp.dot(y.astype(BF16), w2_ref[...], preferred_element_type=F32)
    _res_ln_epilogue(acc, _load_rows(rest[:nx], ntp), *rest[nx:], alpha)


def _pw2_ln_kernel(*refs, na, nx, ntp, alpha):
    cg_ref, cb_ref, w_ref = refs[na:na + 3]
    rest = refs[na + 3:]
    a = _silu(_ln_rows(_load_rows(refs[:na], ntp), cg_ref[...], cb_ref[...])).astype(BF16)
    acc = jnp.dot(a, w_ref[...], preferred_element_type=F32)
    _res_ln_epilogue(acc, _load_rows(rest[:nx], ntp), *rest[nx:], alpha)


def _res_ln_call(kern, name, lead_args, lead_specs, x, t, modl, grp, which_g, which_sc, which_sh, ln_g, ln_b, wr, br,
                 tm):
    d, nr = wr.shape[0] // 2, wr.shape[1]
    row = lambda i: (i, 0)
    const = lambda i: (0, 0)
    xargs, xspecs = _row_operand(x, tm, d, grp.tp // tm, 1)
    specs = lead_specs + xspecs + [
        grp.mod_spec(which_g, tm, d, 1),
        pl.BlockSpec((1, d), const), pl.BlockSpec((1, d), const),
        grp.mod_spec(which_sc, tm, d, 1), grp.mod_spec(which_sh, tm, d, 1),
        pl.BlockSpec((2 * d, nr), const, pipeline_mode=pl.Buffered(1)), pl.BlockSpec((1, nr), const),
    ]
    return pl.pallas_call(
        functools.partial(kern, nx=len(xargs)),
        grid=(t // tm,),
        in_specs=specs,
        out_specs=[pl.BlockSpec((tm, d), row), pl.BlockSpec((tm, d), row), pl.BlockSpec((tm, nr), row)],
        out_shape=[jax.ShapeDtypeStruct((t, d), F32), jax.ShapeDtypeStruct((t, d), F32),
                   jax.ShapeDtypeStruct((t, nr), F32)],
        compiler_params=_cp("arbitrary"),
        name=name,
    )(*lead_args, *xargs, modl, ln_g.reshape(1, d), ln_b.reshape(1, d), modl, modl, wr, br)


def _out_ln(o_bf, y, w_out_bf, x, t, modl, grp, ln_g, ln_b, wr, br, alpha):
    d = w_out_bf.shape[1]
    ka = d // 2
    tm = _tile(math.gcd(grp.tp, grp.ls), 512)
    ntp = grp.tp // tm
    single = pl.Buffered(1)
    oargs, ospecs = _row_operand(o_bf, tm, ka, ntp, 1)
    yargs, yspecs = _row_operand(y, tm, ka, ntp, 1)
    lead_specs = ospecs + yspecs + [
        pl.BlockSpec((ka, d), lambda i: (0, 0), pipeline_mode=single),
        pl.BlockSpec((ka, d), lambda i: (1, 0), pipeline_mode=single),
    ]
    kern = functools.partial(_out_ln_kernel, no=len(oargs), ny=len(yargs), ntp=ntp, alpha=alpha)
    return _res_ln_call(kern, "out_proj_ln", oargs + yargs + [w_out_bf, w_out_bf], lead_specs, x, t, modl, grp,
                        2, 4, 3, ln_g, ln_b, wr, br, tm)


def _pw2_ln(a, cv_g, cv_b, w_bf, x, t, modl, grp, ln_g, ln_b, wr, br, alpha):
    d = w_bf.shape[0]
    tm = _tile(math.gcd(grp.tp, grp.ls), 512)
    ntp = grp.tp // tm
    aargs, aspecs = _row_operand(a, tm, d, ntp, 1)
    lead_specs = aspecs + [
        pl.BlockSpec((1, d), lambda i: (0, 0)), pl.BlockSpec((1, d), lambda i: (0, 0)),
        pl.BlockSpec((d, d), lambda i: (0, 0), pipeline_mode=pl.Buffered(1)),
    ]
    kern = functools.partial(_pw2_ln_kernel, na=len(aargs), ntp=ntp, alpha=alpha)
    return _res_ln_call(kern, "conv_proj_ln", aargs + [cv_g.reshape(1, d), cv_b.reshape(1, d), w_bf], lead_specs,
                        x, t, modl, grp, 2, 4, 3, ln_g, ln_b, wr, br, tm)


def _route_kernel(lg_ref, tri_ref, idx_ref, gate_ref, cnt_ref, carry, *, ng, epg):
    @pl.when(pl.program_id(0) == 0)
    def _():
        carry[...] = jnp.zeros_like(carry)

    lg = lg_ref[...]
    lane = lax.broadcasted_iota(jnp.int32, lg.shape, 1)
    neg = -jnp.inf
    big = lg.shape[1]
    gmask = lane < ng
    lgm = jnp.where(gmask, lg, neg)
    mg = jnp.max(lgm, axis=-1, keepdims=True)
    g = jnp.min(jnp.where(lgm == mg, lane, big), axis=-1, keepdims=True)
    gate_g = 1.0 / jnp.sum(jnp.exp(lgm - mg), axis=-1, keepdims=True)
    el = lane - ng
    lo = g * epg
    emask = (el >= lo) & (el < lo + epg)
    le = jnp.where(emask, lg, neg)
    m1 = jnp.max(le, axis=-1, keepdims=True)
    i1 = jnp.min(jnp.where(le == m1, lane, big), axis=-1, keepdims=True)
    le2 = jnp.where(lane == i1, neg, le)
    m2 = jnp.max(le2, axis=-1, keepdims=True)
    i2 = jnp.min(jnp.where(le2 == m2, lane, big), axis=-1, keepdims=True)
    r = jnp.exp(m2 - m1)
    den = 1.0 + r
    g1 = gate_g / den
    g2 = gate_g * r / den
    e1 = i1 - ng
    e2 = i2 - ng
    oh1 = lane == e1
    oh2 = lane == e2
    oh1f = jnp.where(oh1, 1.0, 0.0)
    oh2f = jnp.where(oh2, 1.0, 0.0)
    p1 = jnp.dot(tri_ref[...], oh1f.astype(BF16), preferred_element_type=F32)
    p2 = jnp.dot(tri_ref[...], oh2f.astype(BF16), preferred_element_type=F32)
    c = carry[0:1, :]
    tot1 = jnp.sum(oh1f, axis=0, keepdims=True)
    tot2 = jnp.sum(oh2f, axis=0, keepdims=True)
    r1 = jnp.sum(jnp.where(oh1, p1 + c, 0.0), axis=-1, keepdims=True)
    r2 = jnp.sum(jnp.where(oh2, p2 + c + tot1, 0.0), axis=-1, keepdims=True)
    newc = c + tot1 + tot2
    carry[...] = jnp.broadcast_to(newc, carry.shape)
    cnt_ref[...] = jnp.broadcast_to(newc, cnt_ref.shape)
    zi = jnp.zeros_like(lane)
    idx_ref[...] = jnp.where(lane == 0, e1, jnp.where(lane == 1, e2, jnp.where(
        lane == 2, r1.astype(jnp.int32), jnp.where(lane == 3, r2.astype(jnp.int32), zi))))
    gate_ref[...] = jnp.where(lane == 0, g1, jnp.where(lane == 1, g2, 0.0))


def _route(lg, ng, epg):
    t, nr = lg.shape
    tm = _tile(t, 512)
    tri = jnp.asarray(np.tril(np.ones((tm, tm), np.float32), -1)).astype(BF16)
    kern = functools.partial(_route_kernel, ng=ng, epg=epg)
    return pl.pallas_call(
        kern,
        grid=(t // tm,),
        in_specs=[pl.BlockSpec((tm, nr), lambda i: (i, 0)), pl.BlockSpec((tm, tm), lambda i: (0, 0))],
        out_specs=[pl.BlockSpec((tm, nr), lambda i: (i, 0)), pl.BlockSpec((tm, nr), lambda i: (i, 0)),
                   pl.BlockSpec((SUBLANES, nr), lambda i: (0, 0))],
        out_shape=[jax.ShapeDtypeStruct((t, nr), jnp.int32), jax.ShapeDtypeStruct((t, nr), F32),
                   jax.ShapeDtypeStruct((SUBLANES, nr), F32)],
        scratch_shapes=[pltpu.VMEM((SUBLANES, nr), F32)],
        compiler_params=_cp("arbitrary"),
        name="moe_route",
    )(lg, tri)


def _ffn_kernel(be_ref, nu_ref, tok_ref, dst_ref, h_ref, wg_ref, wu_ref, wd_ref, o_ref, xb0, xb1, yb0, yb1, gsem, ssem,
                *, blk, nblk, nchunk):
    del be_ref
    b = pl.program_id(0)
    nu = nu_ref[0]
    xbufs = (xb0, xb1)
    ybufs = (yb0, yb1)

    def gather_copy(src_row, s, r):
        return pltpu.make_async_copy(h_ref.at[pl.ds(src_row, 1)], xbufs[s].at[pl.ds(r, 1)], gsem)

    def scatter_copy(s, r, dst_row):
        return pltpu.make_async_copy(ybufs[s].at[pl.ds(r, 1)], o_ref.at[pl.ds(dst_row, 1)], ssem)

    def wait_rows(make_copy):
        def body(r, carry):
            make_copy(r).wait()
            return carry

        lax.fori_loop(0, blk, body, 0, unroll=8)

    @pl.when(b == 0)
    def _():
        yb0[...] = jnp.zeros_like(yb0)
        yb1[...] = jnp.zeros_like(yb1)

        def body(r, carry):
            gather_copy(tok_ref[r], 0, r).start()
            return carry

        lax.fori_loop(0, blk, body, 0)

    @pl.when(b <= nu + 1)
    def _():
        wait_rows(lambda r: gather_copy(0, 0, r))

    @pl.when((b >= 1) & (b <= nu + 1))
    def _():
        wait_rows(lambda r: scatter_copy(0, r, 0))

    nxt = jnp.minimum(b + 1, nblk - 1) * blk
    prv = b * blk
    de = wg_ref.shape[1]
    hc = de // nchunk
    per = -(-blk // ((nchunk - 1) * SUBLANES)) * SUBLANES
    bounds = [min(c * per, blk) for c in range(nchunk)] + [blk] * 2

    def step(slot):
        other = 1 - slot
        x = xbufs[slot][...].astype(BF16)
        for c in range(nchunk):
            for r in range(bounds[c], bounds[c + 1] if c < nchunk - 1 else bounds[c]):
                gather_copy(tok_ref[nxt + r], other, r).start(priority=r % 2)
                scatter_copy(other, r, dst_ref[prv + r]).start(priority=r % 2)
            cols = slice(c * hc, (c + 1) * hc)
            gate = jnp.dot(x, wg_ref[:, cols], preferred_element_type=F32)
            up = jnp.dot(x, wu_ref[:, cols], preferred_element_type=F32)
            hid = (_silu(gate) * up).astype(BF16)
            part = jnp.dot(hid, wd_ref[cols, :], preferred_element_type=F32)
            if c == 0:
                ybufs[slot][...] = part
            else:
                ybufs[slot][...] = ybufs[slot][...] + part

    @pl.when((b % 2 == 0) & (b <= nu))
    def _():
        step(0)

    @pl.when((b % 2 == 1) & (b <= nu))
    def _():
        step(1)

    @pl.when((b == nblk) & (b <= nu))
    def _():
        wait_rows(lambda r: gather_copy(0, 0, r))
        wait_rows(lambda r: scatter_copy(0, r, 0))


def _ffn(blk_exp, nused, tok, dstp, h, wg_bf, wu_bf, wd_bf, layer, nrows_out):
    d = h.shape[1]
    de = wg_bf.shape[3]
    nblk = tok.shape[0] // MOE_BLK
    kern = functools.partial(_ffn_kernel, blk=MOE_BLK, nblk=nblk, nchunk=2)
    wmap = lambda b, be, nu, tk, ds_: (layer, be[jnp.minimum(b, nblk - 1)], 0, 0)
    buf = pltpu.VMEM((MOE_BLK, d), F32)
    return pl.pallas_call(
        kern,
        grid_spec=pltpu.PrefetchScalarGridSpec(
            num_scalar_prefetch=4,
            grid=(nblk + 1,),
            in_specs=[
                pl.BlockSpec(memory_space=pl.ANY),
                pl.BlockSpec((None, None, d, de), wmap),
                pl.BlockSpec((None, None, d, de), wmap),
                pl.BlockSpec((None, None, de, d), wmap),
            ],
            out_specs=pl.BlockSpec(memory_space=pl.ANY),
            scratch_shapes=[buf, buf, buf, buf, pltpu.SemaphoreType.DMA(()), pltpu.SemaphoreType.DMA(())],
        ),
        out_shape=jax.ShapeDtypeStruct((nrows_out, d), F32),
        compiler_params=_cp("arbitrary"),
        name="moe_ffn",
    )(blk_exp, nused, tok, dstp, h, wg_bf, wu_bf, wd_bf)


def _combine_kernel(ya_ref, yb_ref, x_ref, gt_ref, g_ref, lng_ref, lnb_ref, *o_refs, alpha, ntp):
    gt = gt_ref[...]
    y = gt[:, 0:1] * ya_ref[...] + gt[:, 1:2] * yb_ref[...]
    val = _ln_rows(alpha * x_ref[...] + g_ref[...] * y, lng_ref[...], lnb_ref[...])
    if len(o_refs) == 1:
        o_refs[0][...] = val
    else:
        i = pl.program_id(0)

        @pl.when(i < ntp)
        def _():
            o_refs[0][...] = val

        @pl.when(i >= ntp)
        def _():
            o_refs[1][...] = val


def _combine(y2, x, gates, modl, grp, which_g, ln_g, ln_b, alpha, split):
    t, d = x.shape
    nr = gates.shape[1]
    tm = _tile(math.gcd(grp.tp, grp.ls), 256)
    ntp = grp.tp // tm
    kern = functools.partial(_combine_kernel, alpha=alpha, ntp=ntp)
    assert TOP_K == 2
    nt = t // tm
    if split:
        out_specs = [pl.BlockSpec((tm, d), lambda i: (jnp.minimum(i, ntp - 1), 0)),
                     pl.BlockSpec((tm, d), lambda i: (jnp.maximum(i - ntp, 0), 0))]
        out_shape = [jax.ShapeDtypeStruct((grp.tp, d), F32), jax.ShapeDtypeStruct((t - grp.tp, d), F32)]
    else:
        out_specs = pl.BlockSpec((tm, d), lambda i: (i, 0))
        out_shape = jax.ShapeDtypeStruct((t, d), F32)
    return pl.pallas_call(
        kern,
        grid=(t // tm,),
        in_specs=[
            pl.BlockSpec((tm, d), lambda i: (i, 0)),
            pl.BlockSpec((tm, d), lambda i: (nt + i, 0)),
            pl.BlockSpec((tm, d), lambda i: (i, 0)),
            pl.BlockSpec((tm, nr), lambda i: (i, 0)),
            grp.mod_spec(which_g, tm, d, 1),
            pl.BlockSpec((1, d), lambda i: (0, 0)),
            pl.BlockSpec((1, d), lambda i: (0, 0)),
        ],
        out_specs=out_specs,
        out_shape=out_shape,
        compiler_params=_cp("arbitrary"),
        name="moe_combine_ln",
    )(y2, y2, x, gates, modl, ln_g.reshape(1, d), ln_b.reshape(1, d))


def _moe(h, lg, x, modl, grp, ln_g, ln_b, wg_bf, wu_bf, wd_bf, layer, ng, alpha, split):
    t, d = h.shape
    ne = wg_bf.shape[1]
    na = t * TOP_K
    idx, gates, cnt = _route(lg, ng, ne // ng)
    experts = idx[:, 0:TOP_K]
    ranks = idx[:, TOP_K:2 * TOP_K]
    counts = cnt[0, :ne].astype(jnp.int32)
    padded = (counts + MOE_BLK - 1) // MOE_BLK * MOE_BLK
    pends = jnp.cumsum(padded)
    pstarts = pends - padded
    dest = (jnp.take(pstarts, experts) + ranks).reshape(-1).astype(jnp.int32)
    nblk = -(-na // MOE_BLK) + ne
    ns = nblk * MOE_BLK
    blk_start = jnp.arange(nblk, dtype=jnp.int32) * MOE_BLK
    blk_exp = jnp.minimum(jnp.sum(pends[None, :] <= blk_start[:, None], axis=1), ne - 1).astype(jnp.int32)
    code = jnp.full((ns,), -1, jnp.int32).at[dest].set(jnp.arange(na, dtype=jnp.int32), unique_indices=True)
    spare = na + jnp.arange(ns, dtype=jnp.int32) % MOE_BLK
    tok = jnp.where(code >= 0, code // TOP_K, 0)
    dst = jnp.where(code >= 0, (code % TOP_K) * t + code // TOP_K, spare)
    dstp = jnp.concatenate([na + jnp.arange(MOE_BLK, dtype=jnp.int32), dst])
    nused = (pends[-1:] // MOE_BLK).astype(jnp.int32)
    y2 = _ffn(blk_exp, nused, tok, dstp, h, wg_bf, wu_bf, wd_bf, layer, na + MOE_BLK)
    return _combine(y2, x, gates, modl, grp, 5, ln_g, ln_b, alpha, split)


def _rope_tables(tp, nb, l):
    rows = l // GRID_W
    row = np.repeat(np.arange(rows, dtype=np.float32), GRID_W)
    col = np.tile(np.arange(GRID_W, dtype=np.float32), rows)
    inv_freq = (np.float32(ROPE_THETA) ** (-np.arange(ROPE_PAIRS, dtype=np.float32) / np.float32(ROPE_PAIRS)))
    ar = row[:, None] * inv_freq
    ac = col[:, None] * inv_freq
    ang = np.concatenate([ar, ar, ac, ac], axis=-1).astype(np.float64)
    cos = np.cos(ang)
    sin = np.sin(ang)
    sign = np.tile(np.concatenate([-np.ones(ROPE_PAIRS), np.ones(ROPE_PAIRS)]), 2)
    cos_all = np.concatenate([np.ones((tp, HEAD_DIM)), np.tile(cos, (nb, 1))], axis=0)
    sin_all = np.concatenate([np.zeros((tp, HEAD_DIM)), np.tile(sin * sign, (nb, 1))], axis=0)
    return jnp.asarray(cos_all.astype(np.float32)), jnp.asarray(sin_all.astype(np.float32))


def kernel(x_prompt, x_sample, cache_k, cache_v, c, c_ctx, w_ada, b_ada, ln_g, ln_b, w_in_ab, q_norm_g, k_norm_g, hy_short_w, hy_short_b, hy_f_w1, hy_f_b1, hy_f_w2, hy_f_b2, hy_f_w3, hy_f_b3, hy_sin_freq, hy_skip, w_out_ab, cv_pw1, cv_dw_w, cv_dw_b, cv_ln_g, cv_ln_b, cv_pw2, moe_w_grp, moe_b_grp, moe_w_exp, moe_b_exp, moe_w_gate, moe_w_up, moe_w_down):
    bp, lp, d = x_prompt.shape
    bs, ls, _ = x_sample.shape
    depth = w_ada.shape[0]
    tp, ts = bp * lp, bs * ls
    t = tp + ts
    kv_w = cache_k.shape[2] * HEAD_DIM
    hy_w = hy_skip.shape[2]
    att_w = d - hy_w
    ng = moe_w_grp.shape[2]
    ne = moe_w_exp.shape[2]
    alpha = (2 * depth) ** 0.25
    nmod = 6
    ngrp = 1 + bs
    gpad = -(-ngrp // SUBLANES) * SUBLANES
    grp = _Groups(tp, ls, nmod)

    x = jnp.concatenate([x_prompt.reshape(tp, d), x_sample.reshape(ts, d)], axis=0)
    cond = jnp.zeros((gpad, d), F32).at[0].set(c_ctx).at[1:ngrp].set(c)
    mod = _ada(cond, w_ada, b_ada)
    nrt = LANES
    wg_bf, wu_bf, wd_bf = moe_w_gate.astype(BF16), moe_w_up.astype(BF16), moe_w_down.astype(BF16)
    new_k, new_v = None, None
    for layer in range(depth):
        i = layer // 2
        modl = mod[layer].reshape(gpad * nmod, 1, d)
        wr = jnp.zeros((d, nrt), F32).at[:, :ng].set(moe_w_grp[layer]).at[:, ng:ng + ne].set(moe_w_exp[layer])
        wr_hi = wr.astype(BF16)
        wr = jnp.concatenate([wr_hi, (wr - wr_hi.astype(F32)).astype(BF16)], axis=0)
        br = jnp.zeros((1, nrt), F32).at[0, :ng].set(moe_b_grp[layer]).at[0, ng:ng + ne].set(moe_b_exp[layer])
        if layer % 2 == 0:
            zp = _mm_mod(x, t, modl, grp, 1, 0, w_in_ab[i].astype(BF16))
            cos_t, sin_t = _rope_tables(tp, bs, ls)
            q_bf, k_f, k_bf = _qkprep(zp, cos_t, sin_t, q_norm_g[i], k_norm_g[i], att_w, kv_w)
            o_p = _attention(q_bf, k_bf, zp, 0, bp, lp, att_w, kv_w)
            o_s = _attention(q_bf, k_bf, zp, tp, bs, ls, att_w, kv_w, cache_k, cache_v, i)
            ucol = att_w + 2 * kv_w
            ys = []
            for row0, nb, l in ((0, bp, lp), (tp, bs, ls)):
                fc = _fft_consts(l)
                kf = _hyena_filters(l, fc, hy_f_w1[i], hy_f_b1[i], hy_f_w2[i], hy_f_b2[i], hy_f_w3[i], hy_f_b3[i],
                                    hy_sin_freq[i], hy_w)
                usc = _dwconv(zp, ucol, 3 * hy_w, hy_short_w[i], hy_short_b[i], row0, nb, l)
                z1 = _lconv(usc, 0, 0, usc, hy_w, 0, kf, 0, hy_skip[i, 0], fc, nb, l, hy_w)
                ys.append(_lconv(z1, 0, 0, usc, 2 * hy_w, 0, kf, 1, hy_skip[i, 1], fc, nb, l, hy_w))
            x, h, lg = _out_ln((o_p, o_s), tuple(ys), w_out_ab[i].astype(BF16), x, t, modl, grp,
                               ln_g[layer, 0], ln_b[layer, 0], wr, br, alpha)
            nkv = kv_w // HEAD_DIM
            kk = k_f[:tp].reshape(bp, lp, nkv, HEAD_DIM).transpose(0, 2, 1, 3)
            vv = zp[:tp, att_w + kv_w:att_w + 2 * kv_w].reshape(bp, lp, nkv, HEAD_DIM).transpose(0, 2, 1, 3)
            new_k = kk if new_k is None else jnp.concatenate([new_k, kk], axis=1)
            new_v = vv if new_v is None else jnp.concatenate([new_v, vv], axis=1)
        else:
            a = _mm_mod(x, t, modl, grp, 1, 0, cv_pw1[i].astype(BF16), glu=True)
            a = (_dwconv(a, 0, d, cv_dw_w[i], cv_dw_b[i], 0, bp, lp),
                 _dwconv(a, 0, d, cv_dw_w[i], cv_dw_b[i], tp, bs, ls))
            x, h, lg = _pw2_ln(a, cv_ln_g[i], cv_ln_b[i], cv_pw2[i].astype(BF16), x, t, modl, grp,
                               ln_g[layer, 0], ln_b[layer, 0], wr, br, alpha)
        x = _moe(h, lg, x, modl, grp, ln_g[layer, 1], ln_b[layer, 1], wg_bf, wu_bf, wd_bf, layer, ng, alpha,
                 layer == depth - 1)
    nkv = kv_w // HEAD_DIM
    n_attn = (depth + 1) // 2
    new_k = new_k.reshape(bp, n_attn, nkv, lp, HEAD_DIM)
    new_v = new_v.reshape(bp, n_attn, nkv, lp, HEAD_DIM)
    return (x[0].reshape(bp, lp, d), x[1].reshape(bs, ls, d), new_k, new_v)
```

```python
import functools
import math

import jax
import jax.numpy as jnp
import numpy as np
from jax import lax
from jax.experimental import pallas as pl
from jax.experimental.pallas import tpu as pltpu

F32 = jnp.float32
BF16 = jnp.bfloat16
HIGHEST = lax.Precision.HIGHEST

GRID_W = 64
HEAD_DIM = 128
ROPE_THETA = 10000.0
ROPE_PAIRS = HEAD_DIM // 4
HY_BANDS = 16
HY_TARGET = 1e-2
HY_FAST = 0.3
HY_SLOW = 1.5
TOP_K = 2
EPS = 1e-6

LANES = 128
SUBLANES = 8
VMEM_LIMIT = 56 * 1024 * 1024
MOE_BLK = 256
FFT_SPLIT = 32
FFT_DENSE_MAX = 1024
FFT_UNROLL = 8
ATTN_PARTS = 2


def _cp(*sem):
    return pltpu.CompilerParams(dimension_semantics=sem, vmem_limit_bytes=VMEM_LIMIT)


def _tile(n, target, mult=SUBLANES):
    if n <= target:
        return n
    for t in range(target, 0, -1):
        if n % t == 0 and t % mult == 0:
            return t
    return n


def _silu(x):
    return x * jax.nn.sigmoid(x)


def _ln_rows(x, g, b):
    mu = jnp.mean(x, axis=-1, keepdims=True)
    xc = x - mu
    var = jnp.mean(xc * xc, axis=-1, keepdims=True)
    return xc * lax.rsqrt(var + EPS) * g + b


def _ada_kernel(c_ref, w_ref, b_ref, o_ref):
    s = _silu(c_ref[...])
    o_ref[...] = jnp.dot(s, w_ref[...], precision=HIGHEST, preferred_element_type=F32) + b_ref[...]


def _ada(cond, w_ada, b_ada):
    depth, d, n = w_ada.shape
    g = cond.shape[0]
    tn = _tile(n, 1024, LANES)
    return pl.pallas_call(
        _ada_kernel,
        grid=(depth, n // tn),
        in_specs=[
            pl.BlockSpec((g, d), lambda l, j: (0, 0)),
            pl.BlockSpec((None, d, tn), lambda l, j: (l, 0, j)),
            pl.BlockSpec((None, 1, tn), lambda l, j: (l, 0, j)),
        ],
        out_specs=pl.BlockSpec((None, g, tn), lambda l, j: (l, 0, j)),
        out_shape=jax.ShapeDtypeStruct((depth, g, n), F32),
        compiler_params=_cp("arbitrary", "arbitrary"),
        name="ada_mod",
    )(cond, w_ada, b_ada.reshape(depth, 1, n))


def _load_rows(refs, ntp):
    if len(refs) == 1:
        return refs[0][...]
    return jnp.where(pl.program_id(0) < ntp, refs[0][...], refs[1][...])


def _row_operand(arr, tm, cols, ntp, nargs):
    if isinstance(arr, tuple):
        if nargs == 2:
            maps = (lambda i, j: (jnp.minimum(i, ntp - 1), 0), lambda i, j: (jnp.maximum(i - ntp, 0), 0))
        else:
            maps = (lambda i: (jnp.minimum(i, ntp - 1), 0), lambda i: (jnp.maximum(i - ntp, 0), 0))
        return list(arr), [pl.BlockSpec((tm, cols), m) for m in maps]
    one = (lambda i, j: (i, 0)) if nargs == 2 else (lambda i: (i, 0))
    return [arr], [pl.BlockSpec((tm, cols), one)]


def _mm_mod_kernel(*refs, nx, ntp):
    sc_ref, sh_ref, w_ref, o_ref, h_scr = refs[nx:]

    @pl.when(pl.program_id(1) == 0)
    def _():
        h_scr[...] = (_load_rows(refs[:nx], ntp) * (1.0 + sc_ref[...]) + sh_ref[...]).astype(BF16)

    o_ref[...] = jnp.dot(h_scr[...], w_ref[...], preferred_element_type=F32)


def _mm_glu_kernel(*refs, nx, ntp):
    sc_ref, sh_ref, wa_ref, wg_ref, o_ref, h_scr = refs[nx:]

    @pl.when(pl.program_id(1) == 0)
    def _():
        h_scr[...] = (_load_rows(refs[:nx], ntp) * (1.0 + sc_ref[...]) + sh_ref[...]).astype(BF16)

    a = jnp.dot(h_scr[...], wa_ref[...], preferred_element_type=F32)
    g = jnp.dot(h_scr[...], wg_ref[...], preferred_element_type=F32)
    o_ref[...] = a * jax.nn.sigmoid(g)


class _Groups:
    def __init__(self, tp, ls, nmod):
        self.tp, self.ls, self.nmod = tp, ls, nmod

    def gid(self, i, tm):
        row = i * tm
        return jnp.where(row < self.tp, 0, 1 + (row - self.tp) // self.ls)

    def mod_spec(self, which, tm, d, nargs=2):
        if nargs == 2:
            return pl.BlockSpec((None, 1, d), lambda i, j: (self.gid(i, tm) * self.nmod + which, 0, 0))
        return pl.BlockSpec((None, 1, d), lambda i: (self.gid(i, tm) * self.nmod + which, 0, 0))


def _mm_mod(x, t, modl, grp, which_sc, which_sh, w_bf, glu=False):
    d = w_bf.shape[0]
    n = w_bf.shape[1] // (2 if glu else 1)
    tm = _tile(math.gcd(grp.tp, grp.ls), 1024)
    tn = _tile(n, 512, LANES)
    ntp = grp.tp // tm
    args, specs = _row_operand(x, tm, d, ntp, 2)
    nx = len(args)
    specs += [
        grp.mod_spec(which_sc, tm, d),
        grp.mod_spec(which_sh, tm, d),
        pl.BlockSpec((d, tn), lambda i, j: (0, j)),
    ]
    args += [modl, modl, w_bf]
    if glu:
        noff = n // tn
        specs.append(pl.BlockSpec((d, tn), lambda i, j: (0, j + noff)))
        args.append(w_bf)
    return pl.pallas_call(
        functools.partial(_mm_glu_kernel if glu else _mm_mod_kernel, nx=nx, ntp=ntp),
        grid=(t // tm, n // tn),
        in_specs=specs,
        out_specs=pl.BlockSpec((tm, tn), lambda i, j: (i, j)),
        out_shape=jax.ShapeDtypeStruct((t, n), F32),
        scratch_shapes=[pltpu.VMEM((tm, d), BF16)],
        compiler_params=_cp("arbitrary", "arbitrary"),
        name="mm_glu" if glu else "mm_mod",
    )(*args)


def _qkprep_kernel(q_ref, k_ref, c_ref, s_ref, qg_ref, kg_ref, qo_ref, ko_ref, kb_ref, *, nq, nk, scale):
    c = c_ref[...]
    s = s_ref[...]
    lane = lax.broadcasted_iota(jnp.int32, c.shape, 1)
    first = (lane % (2 * ROPE_PAIRS)) < ROPE_PAIRS

    def prep(xh, g):
        ms = jnp.mean(xh * xh, axis=-1, keepdims=True)
        xn = xh * lax.rsqrt(ms + EPS) * g
        rot = jnp.where(first, pltpu.roll(xn, HEAD_DIM - ROPE_PAIRS, 1), pltpu.roll(xn, ROPE_PAIRS, 1))
        return xn * c + rot * s

    for h in range(nq):
        sl = slice(h * HEAD_DIM, (h + 1) * HEAD_DIM)
        qo_ref[:, sl] = (prep(q_ref[:, sl], qg_ref[...]) * scale).astype(BF16)
    for h in range(nk):
        sl = slice(h * HEAD_DIM, (h + 1) * HEAD_DIM)
        kh = prep(k_ref[:, sl], kg_ref[...])
        ko_ref[:, sl] = kh
        kb_ref[:, sl] = kh.astype(BF16)


def _qkprep(zp, cos_t, sin_t, q_g, k_g, att_w, kv_w):
    t = zp.shape[0]
    tm = _tile(t, 512)
    nq, nk = att_w // HEAD_DIM, kv_w // HEAD_DIM
    assert att_w % kv_w == 0
    kern = functools.partial(_qkprep_kernel, nq=nq, nk=nk, scale=HEAD_DIM ** -0.5)
    return pl.pallas_call(
        kern,
        grid=(t // tm,),
        in_specs=[
            pl.BlockSpec((tm, att_w), lambda i: (i, 0)),
            pl.BlockSpec((tm, kv_w), lambda i: (i, att_w // kv_w)),
            pl.BlockSpec((tm, HEAD_DIM), lambda i: (i, 0)),
            pl.BlockSpec((tm, HEAD_DIM), lambda i: (i, 0)),
            pl.BlockSpec((1, HEAD_DIM), lambda i: (0, 0)),
            pl.BlockSpec((1, HEAD_DIM), lambda i: (0, 0)),
        ],
        out_specs=[
            pl.BlockSpec((tm, att_w), lambda i: (i, 0)),
            pl.BlockSpec((tm, kv_w), lambda i: (i, 0)),
            pl.BlockSpec((tm, kv_w), lambda i: (i, 0)),
        ],
        out_shape=[
            jax.ShapeDtypeStruct((t, att_w), BF16),
            jax.ShapeDtypeStruct((t, kv_w), F32),
            jax.ShapeDtypeStruct((t, kv_w), BF16),
        ],
        compiler_params=_cp("arbitrary"),
        name="qk_prep",
    )(zp, zp, cos_t, sin_t, q_g.reshape(1, HEAD_DIM), k_g.reshape(1, HEAD_DIM))


def _attn_kernel(*refs, has_cache, g, tq):
    if has_cache:
        q_ref, k_ref, v_ref, kc_ref, vc_ref, o_ref = refs
    else:
        q_ref, k_ref, v_ref, o_ref = refs
    nt = (((1,), (1,)), ((), ()))
    nparts = ATTN_PARTS if g % ATTN_PARTS == 0 else 1
    hp = g // nparts
    for part in range(nparts):
        heads = range(part * hp, (part + 1) * hp)
        qs = jnp.concatenate([q_ref[:, h * HEAD_DIM:(h + 1) * HEAD_DIM] for h in heads], axis=0)
        s = lax.dot_general(qs, k_ref[...], nt, preferred_element_type=F32)
        m = jnp.max(s, axis=-1, keepdims=True)
        if has_cache:
            sc = lax.dot_general(qs, kc_ref[...].astype(BF16), nt, preferred_element_type=F32)
            m = jnp.maximum(m, jnp.max(sc, axis=-1, keepdims=True))
        p = jnp.exp(s - m)
        l = jnp.sum(p, axis=-1, keepdims=True)
        o = jnp.dot(p.astype(BF16), v_ref[...].astype(BF16), preferred_element_type=F32)
        if has_cache:
            pc = jnp.exp(sc - m)
            l = l + jnp.sum(pc, axis=-1, keepdims=True)
            o = o + jnp.dot(pc.astype(BF16), vc_ref[...].astype(BF16), preferred_element_type=F32)
        o = o / l
        for i, h in enumerate(heads):
            o_ref[:, h * HEAD_DIM:(h + 1) * HEAD_DIM] = o[i * tq:(i + 1) * tq].astype(BF16)


def _attention(q_bf, k_bf, zp, row0, nb, l, att_w, kv_w, cache_k=None, cache_v=None, layer=0):
    nkv = kv_w // HEAD_DIM
    g = att_w // kv_w
    tq = _tile(l, max(SUBLANES, 512 // g))
    rb = row0 // l
    assert row0 % l == 0
    vcol = (att_w + kv_w) // HEAD_DIM
    has_cache = cache_k is not None
    specs = [
        pl.BlockSpec((tq, g * HEAD_DIM), lambda b, h, i: ((row0 + b * l) // tq + i, h)),
        pl.BlockSpec((l, HEAD_DIM), lambda b, h, i: (rb + b, h)),
        pl.BlockSpec((l, HEAD_DIM), lambda b, h, i: (rb + b, vcol + h)),
    ]
    args = [q_bf, k_bf, zp]
    if has_cache:
        past = cache_k.shape[3]
        cspec = pl.BlockSpec((None, None, None, past, HEAD_DIM), lambda b, h, i: (b, layer, h, 0, 0))
        specs += [cspec, cspec]
        args += [cache_k, cache_v]
    kern = functools.partial(_attn_kernel, has_cache=has_cache, g=g, tq=tq)
    return pl.pallas_call(
        kern,
        grid=(nb, nkv, l // tq),
        in_specs=specs,
        out_specs=pl.BlockSpec((tq, g * HEAD_DIM), lambda b, h, i: (b * (l // tq) + i, h)),
        out_shape=jax.ShapeDtypeStruct((nb * l, att_w), BF16),
        compiler_params=_cp("arbitrary", "arbitrary", "arbitrary"),
        name="attn_cache" if has_cache else "attn",
    )(*args)


def _dwconv_kernel(x_ref, w_ref, b_ref, o_ref, pad, *, l, width, rc, off):
    ct = x_ref.shape[1]
    zeros = jnp.zeros((off, ct), F32)
    pad[0:off, :] = zeros
    pad[off + l:off + l + off, :] = zeros
    pad[off:off + l, :] = x_ref[...]
    bias = b_ref[...]
    shift = off - width // 2

    def chunk(ci, carry):
        base = pl.multiple_of(ci * rc, rc)
        win = pad[pl.ds(base, rc + 2 * off), :]
        acc = jnp.zeros((rc, ct), F32) + bias
        span = rc + 2 * off - SUBLANES
        for p in range(SUBLANES):
            taps = [k for k in range(width) if (k + shift) % SUBLANES == p]
            if taps:
                shifted = win[p:p + span, :]
                for k in taps:
                    o = k + shift - p
                    wk = jnp.concatenate([w_ref[SUBLANES * k:SUBLANES * (k + 1), :]] * (rc // SUBLANES), axis=0)
                    acc = acc + wk * shifted[o:o + rc, :]
        o_ref[pl.ds(base, rc), :] = acc
        return carry

    lax.fori_loop(0, l // rc, chunk, 0)


def _dwconv(x, col0, ncols, w, b, row0, nb, l):
    width = w.shape[0]
    off = 16
    assert width // 2 <= off
    ct = LANES if width > SUBLANES else _tile(ncols, 2 * LANES, LANES)
    rc = _tile(l, 64)
    assert rc % SUBLANES == 0
    wp = jnp.broadcast_to(w[:, None, :], (width, SUBLANES, ncols)).reshape(width * SUBLANES, ncols)
    assert col0 % ct == 0 and row0 % l == 0
    kern = functools.partial(_dwconv_kernel, l=l, width=width, rc=rc, off=off)
    return pl.pallas_call(
        kern,
        grid=(nb, ncols // ct),
        in_specs=[
            pl.BlockSpec((l, ct), lambda b_, j: (row0 // l + b_, col0 // ct + j)),
            pl.BlockSpec((width * SUBLANES, ct), lambda b_, j: (0, j)),
            pl.BlockSpec((1, ct), lambda b_, j: (0, j)),
        ],
        out_specs=pl.BlockSpec((l, ct), lambda b_, j: (b_, j)),
        out_shape=jax.ShapeDtypeStruct((nb * l, ncols), F32),
        scratch_shapes=[pltpu.VMEM((l + 2 * off, ct), F32)],
        compiler_params=_cp("arbitrary", "arbitrary"),
        name="dwconv%d" % width,
    )(x, wp, b.reshape(1, ncols))


def _fft_plan(l):
    n = 2 * l
    if n <= FFT_DENSE_MAX or n % FFT_SPLIT != 0:
        return n, 1
    return n // FFT_SPLIT, FFT_SPLIT


def _stack3_lhs(m):
    hi = m.astype(np.float32).astype(jnp.bfloat16)
    lo = (m - np.asarray(hi.astype(np.float32), np.float64)).astype(np.float32).astype(jnp.bfloat16)
    return jnp.asarray(np.concatenate([hi, lo, hi], axis=1))


def _split3(x):
    hi = x.astype(BF16)
    lo = (x - hi.astype(F32)).astype(BF16)
    return jnp.concatenate([hi, hi, lo], axis=0)


def _mm3(lhs3, x):
    return jnp.dot(lhs3, _split3(x), preferred_element_type=F32)


def _fft_consts(l):
    n1, n2 = _fft_plan(l)
    n = n1 * n2
    n1h = n1 // 2
    k1 = np.arange(n1, dtype=np.float64)[:, None]
    ang = 2.0 * np.pi * k1 * np.arange(n1h, dtype=np.float64)[None, :] / n1
    fr, fi = np.cos(ang), -np.sin(ang)
    a_fwd = np.block([[fr, -fi], [fi, fr]])
    ang = 2.0 * np.pi * k1 * np.arange(n1, dtype=np.float64)[None, :] / n1
    a_real = np.concatenate([np.cos(ang), -np.sin(ang)], axis=0)
    angi = 2.0 * np.pi * np.arange(n1h, dtype=np.float64)[:, None] * np.arange(n1, dtype=np.float64)[None, :] / n1
    cr, ci = np.cos(angi) / n, np.sin(angi) / n
    a_inv = np.block([[cr, -ci], [ci, cr]])
    k2 = np.arange(n2, dtype=np.float64)
    angb = 2.0 * np.pi * k2[:, None] * k2[None, :] / n2
    gr, gi = np.cos(angb), -np.sin(angb)
    b_fwd = np.block([[gr, -gi], [gi, gr]])
    b_inv = np.block([[gr, gi], [-gi, gr]])
    angt = 2.0 * np.pi * (np.arange(n1, dtype=np.float64)[:, None] * k2[None, :]).reshape(n, 1) / n
    twr = np.broadcast_to(np.cos(angt), (n, LANES)).astype(np.float32)
    twi = np.broadcast_to(-np.sin(angt), (n, LANES)).astype(np.float32)
    return dict(n1=n1, n2=n2, a_fwd=_stack3_lhs(a_fwd), a_real=_stack3_lhs(a_real), a_inv=_stack3_lhs(a_inv),
                b_fwd=_stack3_lhs(b_fwd), b_inv=_stack3_lhs(b_inv), twr=jnp.asarray(twr), twi=jnp.asarray(twi))


def _rows(ref, start, size, stride):
    if stride == 1:
        return ref[pl.ds(start, size), :]
    return ref[pl.ds(start, size, stride=stride), :]


def _stage_b_fwd(pr, pi, twr_ref, twi_ref, bfwd_ref, k1, n1, n2):
    xr = _rows(pr, k1, n2, n1)
    xi = _rows(pi, k1, n2, n1)
    r0 = pl.multiple_of(k1 * n2, n2)
    cr = twr_ref[pl.ds(r0, n2), :]
    ci = twi_ref[pl.ds(r0, n2), :]
    ar = xr * cr - xi * ci
    ai = xr * ci + xi * cr
    x = _mm3(bfwd_ref[...], jnp.concatenate([ar, ai], axis=0))
    return x[:n2], x[n2:], cr, ci


def _lconv_kernel(a_ref, g_ref, kf_ref, skip_ref, afwd_ref, ainv_ref, bfwd_ref, binv_ref, twr_ref, twi_ref,
                  o_ref, pr, pi, qr, qi, *, l, n1, n2):
    n1h = n1 // 2

    def stage_a(i2, carry):
        slab = jnp.concatenate([_rows(a_ref, i2, n1h, n2), _rows(a_ref, l + i2, n1h, n2)], axis=0)
        out = _mm3(afwd_ref[...], slab)
        r0 = pl.multiple_of(i2 * n1, n1)
        pr[pl.ds(r0, n1), :] = out[:n1]
        pi[pl.ds(r0, n1), :] = out[n1:]
        return carry

    lax.fori_loop(0, n2, stage_a, 0, unroll=min(n2, FFT_UNROLL))

    if n2 == 1:
        kr, ki = kf_ref[0], kf_ref[1]
        xr, xi = pr[...], pi[...]
        qr[...] = xr * kr - xi * ki
        qi[...] = xr * ki + xi * kr
    else:
        def stage_b(k1, carry):
            xr, xi, cr, ci = _stage_b_fwd(pr, pi, twr_ref, twi_ref, bfwd_ref, k1, n1, n2)
            r0 = pl.multiple_of(k1 * n2, n2)
            kr = kf_ref[0, pl.ds(r0, n2), :]
            ki = kf_ref[1, pl.ds(r0, n2), :]
            yr = xr * kr - xi * ki
            yi = xr * ki + xi * kr
            bv = _mm3(binv_ref[...], jnp.concatenate([yr, yi], axis=0))
            br, bi = bv[:n2], bv[n2:]
            qr[pl.ds(r0, n2), :] = br * cr + bi * ci
            qi[pl.ds(r0, n2), :] = bi * cr - br * ci
            return carry

        lax.fori_loop(0, n1, stage_b, 0, unroll=FFT_UNROLL)

    skip = skip_ref[...]

    def stage_c(i2, carry):
        slab = jnp.concatenate([_rows(qr, i2, n1, n2), _rows(qi, i2, n1, n2)], axis=0)
        y = _mm3(ainv_ref[...], slab)
        for half in range(2):
            start = half * l + i2
            a = _rows(a_ref, start, n1h, n2)
            g = _rows(g_ref, start, n1h, n2)
            val = g * (y[half * n1h:(half + 1) * n1h] + skip * a)
            if n2 == 1:
                o_ref[pl.ds(start, n1h), :] = val
            else:
                o_ref[pl.ds(start, n1h, stride=n2), :] = val
        return carry

    lax.fori_loop(0, n2, stage_c, 0, unroll=min(n2, FFT_UNROLL))


def _const_spec(shape):
    zeros = (0,) * len(shape)
    return pl.BlockSpec(shape, lambda a, b: zeros, pipeline_mode=pl.Buffered(1))


def _lconv(a, a_col0, a_row0, gate, g_col0, g_row0, kf, order, skip, fc, nb, l, c):
    n1, n2 = fc["n1"], fc["n2"]
    n = n1 * n2
    ct = LANES
    assert nb % 2 == 0 and a_row0 % (2 * l) == 0 and g_row0 % (2 * l) == 0
    single = pl.Buffered(1) if n2 > 1 else None
    kern = functools.partial(_lconv_kernel, l=l, n1=n1, n2=n2)
    return pl.pallas_call(
        kern,
        grid=(c // ct, nb // 2),
        in_specs=[
            pl.BlockSpec((2 * l, ct), lambda j, p: (a_row0 // (2 * l) + p, a_col0 // ct + j), pipeline_mode=single),
            pl.BlockSpec((2 * l, ct), lambda j, p: (g_row0 // (2 * l) + p, g_col0 // ct + j), pipeline_mode=single),
            pl.BlockSpec((None, 2, n, ct), lambda j, p: (order, 0, 0, j), pipeline_mode=single),
            pl.BlockSpec((1, ct), lambda j, p: (0, j)),
            _const_spec(fc["a_fwd"].shape), _const_spec(fc["a_inv"].shape),
            _const_spec(fc["b_fwd"].shape), _const_spec(fc["b_inv"].shape),
            _const_spec((n, LANES)), _const_spec((n, LANES)),
        ],
        out_specs=pl.BlockSpec((2 * l, ct), lambda j, p: (p, j), pipeline_mode=single),
        out_shape=jax.ShapeDtypeStruct((nb * l, c), F32),
        scratch_shapes=[pltpu.VMEM((n, ct), F32) for _ in range(4)],
        compiler_params=_cp("arbitrary", "arbitrary"),
        name="hyena_lconv",
    )(a, gate, kf, skip.reshape(1, c), fc["a_fwd"], fc["a_inv"], fc["b_fwd"], fc["b_inv"], fc["twr"], fc["twi"])


def _hyfilt_kernel(zz_ref, tt_ref, w1_ref, b1_ref, w2_ref, b2_ref, sf_ref, w3f_ref, w3b_ref, b3f_ref, b3b_ref,
                   dl_ref, areal_ref, bfwd_ref, twr_ref, twi_ref, o_ref, h_scr, kern_scr, pr, pi, *, l, n1, n2):
    rc = _tile(l, 256)
    nch = l // rc

    @pl.when((pl.program_id(0) == 0) & (pl.program_id(1) == 0))
    def _():
        def mlp(ci, carry):
            r0 = pl.multiple_of(ci * rc, rc)
            h = jnp.dot(zz_ref[pl.ds(r0, rc), :], w1_ref[...], precision=HIGHEST, preferred_element_type=F32)
            h = jnp.sin(sf_ref[0:1, :] * (h + b1_ref[...]))
            h = jnp.dot(h, w2_ref[...], precision=HIGHEST, preferred_element_type=F32) + b2_ref[...]
            h_scr[pl.ds(r0, rc), :] = jnp.sin(sf_ref[1:2, :] * h)
            return carry

        lax.fori_loop(0, 2 * nch, mlp, 0)

    def filt(w3_ref, b3_ref, first_chunk):
        def body(ci, acc):
            r0 = pl.multiple_of((first_chunk + ci) * rc, rc)
            hk = jnp.dot(h_scr[pl.ds(r0, rc), :], w3_ref[...], precision=HIGHEST, preferred_element_type=F32)
            win = jnp.exp(-tt_ref[pl.ds(r0, rc), :] * dl_ref[...])
            row = r0 + lax.broadcasted_iota(jnp.int32, (rc, 1), 0)
            k = jnp.where(row == l, 0.0, (hk + b3_ref[...]) * win)
            kern_scr[pl.ds(r0, rc), :] = k
            return acc + jnp.sum(jnp.abs(k), axis=0, keepdims=True)
        return body

    zero = jnp.zeros((1, kern_scr.shape[1]), F32)
    total = lax.fori_loop(0, nch, filt(w3f_ref, b3f_ref, 0), zero)
    total = lax.fori_loop(0, nch, filt(w3b_ref, b3b_ref, nch), total)
    inv = 1.0 / total

    def stage_a(i2, carry):
        out = _mm3(areal_ref[...], _rows(kern_scr, i2, n1, n2) * inv)
        r0 = pl.multiple_of(i2 * n1, n1)
        pr[pl.ds(r0, n1), :] = out[:n1]
        pi[pl.ds(r0, n1), :] = out[n1:]
        return carry

    lax.fori_loop(0, n2, stage_a, 0, unroll=min(n2, FFT_UNROLL))

    if n2 == 1:
        o_ref[0] = pr[...]
        o_ref[1] = pi[...]
    else:
        def stage_b(k1, carry):
            xr, xi, _, _ = _stage_b_fwd(pr, pi, twr_ref, twi_ref, bfwd_ref, k1, n1, n2)
            r0 = pl.multiple_of(k1 * n2, n2)
            o_ref[0, pl.ds(r0, n2), :] = xr
            o_ref[1, pl.ds(r0, n2), :] = xi
            return carry

        lax.fori_loop(0, n1, stage_b, 0, unroll=FFT_UNROLL)


def _hyena_filters(l, fc, w1, b1, w2, b2, w3, b3, sin_freq, c):
    n1, n2 = fc["n1"], fc["n2"]
    n = 2 * l
    order = w3.shape[1] // (2 * c)
    hid = w1.shape[1]
    ct = LANES
    pos = np.concatenate([np.arange(l), [0], np.arange(l - 1, 0, -1)]).astype(np.float32)
    t01 = np.linspace(0.0, 1.0, l, dtype=np.float32)[pos.astype(np.int64)]
    bands = np.linspace(1e-4, HY_BANDS - 1, HY_BANDS, dtype=np.float32)
    ang = (np.float32(2.0 * math.pi) * pos / np.float32(l))[:, None] * bands
    emb = 1 + 2 * HY_BANDS
    feat = np.zeros((n, hid), np.float32)
    feat[:, :emb] = np.concatenate([t01[:, None], np.cos(ang), -np.sin(ang)], axis=-1)
    w1p = jnp.zeros((hid, hid), F32).at[:emb].set(w1)
    tt = jnp.asarray(np.broadcast_to(t01[:, None], (n, LANES)).copy())
    dmin = math.log(HY_TARGET) / HY_SLOW
    dmax = math.log(HY_TARGET) / HY_FAST
    deltas = jnp.abs(jnp.linspace(dmin, dmax, c, dtype=F32)).reshape(1, c)
    cb = c // ct
    kern = functools.partial(_hyfilt_kernel, l=l, n1=n1, n2=n2)
    return pl.pallas_call(
        kern,
        grid=(order, cb),
        in_specs=[
            _const_spec((n, hid)), _const_spec((n, LANES)),
            _const_spec((hid, hid)), _const_spec((1, hid)), _const_spec((hid, hid)), _const_spec((1, hid)),
            _const_spec((2, hid)),
            pl.BlockSpec((hid, ct), lambda o, j: (0, (2 * o) * cb + j)),
            pl.BlockSpec((hid, ct), lambda o, j: (0, (2 * o + 1) * cb + j)),
            pl.BlockSpec((1, ct), lambda o, j: (0, (2 * o) * cb + j)),
            pl.BlockSpec((1, ct), lambda o, j: (0, (2 * o + 1) * cb + j)),
            pl.BlockSpec((1, ct), lambda o, j: (0, j)),
            _const_spec(fc["a_real"].shape), _const_spec(fc["b_fwd"].shape),
            _const_spec((n, LANES)), _const_spec((n, LANES)),
        ],
        out_specs=pl.BlockSpec((None, 2, n, ct), lambda o, j: (o, 0, 0, j), pipeline_mode=pl.Buffered(1)),
        out_shape=jax.ShapeDtypeStruct((order, 2, n, c), F32),
        scratch_shapes=[pltpu.VMEM((n, hid), F32), pltpu.VMEM((n, ct), F32), pltpu.VMEM((n, ct), F32),
                        pltpu.VMEM((n, ct), F32)],
        compiler_params=_cp("arbitrary", "arbitrary"),
        name="hyena_filters",
    )(jnp.asarray(feat), tt, w1p, b1.reshape(1, hid), w2, b2.reshape(1, hid), sin_freq, w3, w3,
      b3.reshape(1, -1), b3.reshape(1, -1), deltas, fc["a_real"], fc["b_fwd"], fc["twr"], fc["twi"])


def _res_ln_epilogue(acc, x, g_ref, lng_ref, lnb_ref, sc_ref, sh_ref, wr_ref, br_ref, xo_ref, h_ref, lg_ref, alpha):
    xn = _ln_rows(alpha * x + g_ref[...] * acc, lng_ref[...], lnb_ref[...])
    xo_ref[...] = xn
    h = xn * (1.0 + sc_ref[...]) + sh_ref[...]
    h_ref[...] = h
    d = h.shape[1]
    hh = h.astype(BF16)
    hl = (h - hh.astype(F32)).astype(BF16)
    lg = jnp.dot(hh, wr_ref[0:d, :], preferred_element_type=F32)
    lg = lg + jnp.dot(hh, wr_ref[d:2 * d, :], preferred_element_type=F32)
    lg = lg + jnp.dot(hl, wr_ref[0:d, :], preferred_element_type=F32)
    lg_ref[...] = lg + br_ref[...]


def _out_ln_kernel(*refs, no, ny, nx, ntp, alpha):
    o = _load_rows(refs[:no], ntp)
    y = _load_rows(refs[no:no + ny], ntp)
    w1_ref, w2_ref = refs[no + ny:no + ny + 2]
    rest = refs[no + ny + 2:]
    acc = jnp.dot(o, w1_ref[...], preferred_element_type=F32)
    acc = acc + jnp.dot(y.astype(BF16), w2_ref[...], preferred_element_type=F32)
    _res_ln_epilogue(acc, _load_rows(rest[:nx], ntp), *rest[nx:], alpha)


def _pw2_ln_kernel(*refs, na, nx, ntp, alpha):
    cg_ref, cb_ref, w_ref = refs[na:na + 3]
    rest = refs[na + 3:]
    a = _silu(_ln_rows(_load_rows(refs[:na], ntp), cg_ref[...], cb_ref[...])).astype(BF16)
    acc = jnp.dot(a, w_ref[...], preferred_element_type=F32)
    _res_ln_epilogue(acc, _load_rows(rest[:nx], ntp), *rest[nx:], alpha)


def _res_ln_call(kern, name, lead_args, lead_specs, x, t, modl, grp, which_g, which_sc, which_sh, ln_g, ln_b, wr, br,
                 tm):
    d, nr = wr.shape[0] // 2, wr.shape[1]
    row = lambda i: (i, 0)
    const = lambda i: (0, 0)
    xargs, xspecs = _row_operand(x, tm, d, grp.tp // tm, 1)
    specs = lead_specs + xspecs + [
        grp.mod_spec(which_g, tm, d, 1),
        pl.BlockSpec((1, d), const), pl.BlockSpec((1, d), const),
        grp.mod_spec(which_sc, tm, d, 1), grp.mod_spec(which_sh, tm, d, 1),
        pl.BlockSpec((2 * d, nr), const, pipeline_mode=pl.Buffered(1)), pl.BlockSpec((1, nr), const),
    ]
    return pl.pallas_call(
        functools.partial(kern, nx=len(xargs)),
        grid=(t // tm,),
        in_specs=specs,
        out_specs=[pl.BlockSpec((tm, d), row), pl.BlockSpec((tm, d), row), pl.BlockSpec((tm, nr), row)],
        out_shape=[jax.ShapeDtypeStruct((t, d), F32), jax.ShapeDtypeStruct((t, d), F32),
                   jax.ShapeDtypeStruct((t, nr), F32)],
        compiler_params=_cp("arbitrary"),
        name=name,
    )(*lead_args, *xargs, modl, ln_g.reshape(1, d), ln_b.reshape(1, d), modl, modl, wr, br)


def _out_ln(o_bf, y, w_out_bf, x, t, modl, grp, ln_g, ln_b, wr, br, alpha):
    d = w_out_bf.shape[1]
    ka = d // 2
    tm = _tile(math.gcd(grp.tp, grp.ls), 512)
    ntp = grp.tp // tm
    single = pl.Buffered(1)
    oargs, ospecs = _row_operand(o_bf, tm, ka, ntp, 1)
    yargs, yspecs = _row_operand(y, tm, ka, ntp, 1)
    lead_specs = ospecs + yspecs + [
        pl.BlockSpec((ka, d), lambda i: (0, 0), pipeline_mode=single),
        pl.BlockSpec((ka, d), lambda i: (1, 0), pipeline_mode=single),
    ]
    kern = functools.partial(_out_ln_kernel, no=len(oargs), ny=len(yargs), ntp=ntp, alpha=alpha)
    return _res_ln_call(kern, "out_proj_ln", oargs + yargs + [w_out_bf, w_out_bf], lead_specs, x, t, modl, grp,
                        2, 4, 3, ln_g, ln_b, wr, br, tm)


def _pw2_ln(a, cv_g, cv_b, w_bf, x, t, modl, grp, ln_g, ln_b, wr, br, alpha):
    d = w_bf.shape[0]
    tm = _tile(math.gcd(grp.tp, grp.ls), 512)
    ntp = grp.tp // tm
    aargs, aspecs = _row_operand(a, tm, d, ntp, 1)
    lead_specs = aspecs + [
        pl.BlockSpec((1, d), lambda i: (0, 0)), pl.BlockSpec((1, d), lambda i: (0, 0)),
        pl.BlockSpec((d, d), lambda i: (0, 0), pipeline_mode=pl.Buffered(1)),
    ]
    kern = functools.partial(_pw2_ln_kernel, na=len(aargs), ntp=ntp, alpha=alpha)
    return _res_ln_call(kern, "conv_proj_ln", aargs + [cv_g.reshape(1, d), cv_b.reshape(1, d), w_bf], lead_specs,
                        x, t, modl, grp, 2, 4, 3, ln_g, ln_b, wr, br, tm)


def _route_kernel(lg_ref, tri_ref, idx_ref, gate_ref, cnt_ref, carry, *, ng, epg):
    @pl.when(pl.program_id(0) == 0)
    def _():
        carry[...] = jnp.zeros_like(carry)

    lg = lg_ref[...]
    lane = lax.broadcasted_iota(jnp.int32, lg.shape, 1)
    neg = -jnp.inf
    big = lg.shape[1]
    gmask = lane < ng
    lgm = jnp.where(gmask, lg, neg)
    mg = jnp.max(lgm, axis=-1, keepdims=True)
    g = jnp.min(jnp.where(lgm == mg, lane, big), axis=-1, keepdims=True)
    gate_g = 1.0 / jnp.sum(jnp.exp(lgm - mg), axis=-1, keepdims=True)
    el = lane - ng
    lo = g * epg
    emask = (el >= lo) & (el < lo + epg)
    le = jnp.where(emask, lg, neg)
    m1 = jnp.max(le, axis=-1, keepdims=True)
    i1 = jnp.min(jnp.where(le == m1, lane, big), axis=-1, keepdims=True)
    le2 = jnp.where(lane == i1, neg, le)
    m2 = jnp.max(le2, axis=-1, keepdims=True)
    i2 = jnp.min(jnp.where(le2 == m2, lane, big), axis=-1, keepdims=True)
    r = jnp.exp(m2 - m1)
    den = 1.0 + r
    g1 = gate_g / den
    g2 = gate_g * r / den
    e1 = i1 - ng
    e2 = i2 - ng
    oh1 = lane == e1
    oh2 = lane == e2
    oh1f = jnp.where(oh1, 1.0, 0.0)
    oh2f = jnp.where(oh2, 1.0, 0.0)
    p1 = jnp.dot(tri_ref[...], oh1f.astype(BF16), preferred_element_type=F32)
    p2 = jnp.dot(tri_ref[...], oh2f.astype(BF16), preferred_element_type=F32)
    c = carry[0:1, :]
    tot1 = jnp.sum(oh1f, axis=0, keepdims=True)
    tot2 = jnp.sum(oh2f, axis=0, keepdims=True)
    r1 = jnp.sum(jnp.where(oh1, p1 + c, 0.0), axis=-1, keepdims=True)
    r2 = jnp.sum(jnp.where(oh2, p2 + c + tot1, 0.0), axis=-1, keepdims=True)
    newc = c + tot1 + tot2
    carry[...] = jnp.broadcast_to(newc, carry.shape)
    cnt_ref[...] = jnp.broadcast_to(newc, cnt_ref.shape)
    zi = jnp.zeros_like(lane)
    idx_ref[...] = jnp.where(lane == 0, e1, jnp.where(lane == 1, e2, jnp.where(
        lane == 2, r1.astype(jnp.int32), jnp.where(lane == 3, r2.astype(jnp.int32), zi))))
    gate_ref[...] = jnp.where(lane == 0, g1, jnp.where(lane == 1, g2, 0.0))


def _route(lg, ng, epg):
    t, nr = lg.shape
    tm = _tile(t, 512)
    tri = jnp.asarray(np.tril(np.ones((tm, tm), np.float32), -1)).astype(BF16)
    kern = functools.partial(_route_kernel, ng=ng, epg=epg)
    return pl.pallas_call(
        kern,
        grid=(t // tm,),
        in_specs=[pl.BlockSpec((tm, nr), lambda i: (i, 0)), pl.BlockSpec((tm, tm), lambda i: (0, 0))],
        out_specs=[pl.BlockSpec((tm, nr), lambda i: (i, 0)), pl.BlockSpec((tm, nr), lambda i: (i, 0)),
                   pl.BlockSpec((SUBLANES, nr), lambda i: (0, 0))],
        out_shape=[jax.ShapeDtypeStruct((t, nr), jnp.int32), jax.ShapeDtypeStruct((t, nr), F32),
                   jax.ShapeDtypeStruct((SUBLANES, nr), F32)],
        scratch_shapes=[pltpu.VMEM((SUBLANES, nr), F32)],
        compiler_params=_cp("arbitrary"),
        name="moe_route",
    )(lg, tri)


def _ffn_kernel(be_ref, tok_ref, dst_ref, h_ref, wg_ref, wu_ref, wd_ref, o_ref, xb0, xb1, yb0, yb1, gsem, ssem,
                *, blk, nblk, nchunk):
    del be_ref
    b = pl.program_id(0)
    xbufs = (xb0, xb1)
    ybufs = (yb0, yb1)

    def gather_copy(src_row, s, r):
        return pltpu.make_async_copy(h_ref.at[pl.ds(src_row, 1)], xbufs[s].at[pl.ds(r, 1)], gsem)

    def scatter_copy(s, r, dst_row):
        return pltpu.make_async_copy(ybufs[s].at[pl.ds(r, 1)], o_ref.at[pl.ds(dst_row, 1)], ssem)

    def wait_rows(make_copy):
        def body(r, carry):
            make_copy(r).wait()
            return carry

        lax.fori_loop(0, blk, body, 0, unroll=8)

    @pl.when(b == 0)
    def _():
        yb0[...] = jnp.zeros_like(yb0)
        yb1[...] = jnp.zeros_like(yb1)

        def body(r, carry):
            gather_copy(tok_ref[r], 0, r).start()
            return carry

        lax.fori_loop(0, blk, body, 0)

    wait_rows(lambda r: gather_copy(0, 0, r))

    @pl.when(b >= 1)
    def _():
        wait_rows(lambda r: scatter_copy(0, r, 0))

    nxt = jnp.minimum(b + 1, nblk - 1) * blk
    prv = b * blk
    de = wg_ref.shape[1]
    hc = de // nchunk
    per = -(-blk // ((nchunk - 1) * SUBLANES)) * SUBLANES
    bounds = [min(c * per, blk) for c in range(nchunk)] + [blk] * 2

    def step(slot):
        other = 1 - slot
        x = xbufs[slot][...].astype(BF16)
        for c in range(nchunk):
            for r in range(bounds[c], bounds[c + 1] if c < nchunk - 1 else bounds[c]):
                gather_copy(tok_ref[nxt + r], other, r).start(priority=r % 2)
                scatter_copy(other, r, dst_ref[prv + r]).start(priority=r % 2)
            cols = slice(c * hc, (c + 1) * hc)
            gate = jnp.dot(x, wg_ref[:, cols], preferred_element_type=F32)
            up = jnp.dot(x, wu_ref[:, cols], preferred_element_type=F32)
            hid = (_silu(gate) * up).astype(BF16)
            part = jnp.dot(hid, wd_ref[cols, :], preferred_element_type=F32)
            if c == 0:
                ybufs[slot][...] = part
            else:
                ybufs[slot][...] = ybufs[slot][...] + part

    @pl.when(b % 2 == 0)
    def _():
        step(0)

    @pl.when(b % 2 == 1)
    def _():
        step(1)

    @pl.when(b == nblk)
    def _():
        wait_rows(lambda r: gather_copy(0, 0, r))
        wait_rows(lambda r: scatter_copy(0, r, 0))


def _ffn(blk_exp, tok, dstp, h, wg_bf, wu_bf, wd_bf, layer, nrows_out):
    d = h.shape[1]
    de = wg_bf.shape[3]
    nblk = tok.shape[0] // MOE_BLK
    kern = functools.partial(_ffn_kernel, blk=MOE_BLK, nblk=nblk, nchunk=2)
    wmap = lambda b, be, tk, ds_: (layer, be[jnp.minimum(b, nblk - 1)], 0, 0)
    buf = pltpu.VMEM((MOE_BLK, d), F32)
    return pl.pallas_call(
        kern,
        grid_spec=pltpu.PrefetchScalarGridSpec(
            num_scalar_prefetch=3,
            grid=(nblk + 1,),
            in_specs=[
                pl.BlockSpec(memory_space=pl.ANY),
                pl.BlockSpec((None, None, d, de), wmap),
                pl.BlockSpec((None, None, d, de), wmap),
                pl.BlockSpec((None, None, de, d), wmap),
            ],
            out_specs=pl.BlockSpec(memory_space=pl.ANY),
            scratch_shapes=[buf, buf, buf, buf, pltpu.SemaphoreType.DMA(()), pltpu.SemaphoreType.DMA(())],
        ),
        out_shape=jax.ShapeDtypeStruct((nrows_out, d), F32),
        compiler_params=_cp("arbitrary"),
        name="moe_ffn",
    )(blk_exp, tok, dstp, h, wg_bf, wu_bf, wd_bf)


def _combine_kernel(ya_ref, yb_ref, x_ref, gt_ref, g_ref, lng_ref, lnb_ref, *o_refs, alpha, ntp):
    gt = gt_ref[...]
    y = gt[:, 0:1] * ya_ref[...] + gt[:, 1:2] * yb_ref[...]
    val = _ln_rows(alpha * x_ref[...] + g_ref[...] * y, lng_ref[...], lnb_ref[...])
    if len(o_refs) == 1:
        o_refs[0][...] = val
    else:
        i = pl.program_id(0)

        @pl.when(i < ntp)
        def _():
            o_refs[0][...] = val

        @pl.when(i >= ntp)
        def _():
            o_refs[1][...] = val


def _combine(y2, x, gates, modl, grp, which_g, ln_g, ln_b, alpha, split):
    t, d = x.shape
    nr = gates.shape[1]
    tm = _tile(math.gcd(grp.tp, grp.ls), 256)
    ntp = grp.tp // tm
    kern = functools.partial(_combine_kernel, alpha=alpha, ntp=ntp)
    assert TOP_K == 2
    nt = t // tm
    if split:
        out_specs = [pl.BlockSpec((tm, d), lambda i: (jnp.minimum(i, ntp - 1), 0)),
                     pl.BlockSpec((tm, d), lambda i: (jnp.maximum(i - ntp, 0), 0))]
        out_shape = [jax.ShapeDtypeStruct((grp.tp, d), F32), jax.ShapeDtypeStruct((t - grp.tp, d), F32)]
    else:
        out_specs = pl.BlockSpec((tm, d), lambda i: (i, 0))
        out_shape = jax.ShapeDtypeStruct((t, d), F32)
    return pl.pallas_call(
        kern,
        grid=(t // tm,),
        in_specs=[
            pl.BlockSpec((tm, d), lambda i: (i, 0)),
            pl.BlockSpec((tm, d), lambda i: (nt + i, 0)),
            pl.BlockSpec((tm, d), lambda i: (i, 0)),
            pl.BlockSpec((tm, nr), lambda i: (i, 0)),
            grp.mod_spec(which_g, tm, d, 1),
            pl.BlockSpec((1, d), lambda i: (0, 0)),
            pl.BlockSpec((1, d), lambda i: (0, 0)),
        ],
        out_specs=out_specs,
        out_shape=out_shape,
        compiler_params=_cp("arbitrary"),
        name="moe_combine_ln",
    )(y2, y2, x, gates, modl, ln_g.reshape(1, d), ln_b.reshape(1, d))


def _moe(h, lg, x, modl, grp, ln_g, ln_b, wg_bf, wu_bf, wd_bf, layer, ng, alpha, split):
    t, d = h.shape
    ne = wg_bf.shape[1]
    na = t * TOP_K
    idx, gates, cnt = _route(lg, ng, ne // ng)
    experts = idx[:, 0:TOP_K]
    ranks = idx[:, TOP_K:2 * TOP_K]
    counts = cnt[0, :ne].astype(jnp.int32)
    padded = (counts + MOE_BLK - 1) // MOE_BLK * MOE_BLK
    pends = jnp.cumsum(padded)
    pstarts = pends - padded
    dest = (jnp.take(pstarts, experts) + ranks).reshape(-1).astype(jnp.int32)
    nblk = -(-na // MOE_BLK) + ne
    ns = nblk * MOE_BLK
    blk_start = jnp.arange(nblk, dtype=jnp.int32) * MOE_BLK
    blk_exp = jnp.minimum(jnp.sum(pends[None, :] <= blk_start[:, None], axis=1), ne - 1).astype(jnp.int32)
    code = jnp.full((ns,), -1, jnp.int32).at[dest].set(jnp.arange(na, dtype=jnp.int32), unique_indices=True)
    spare = na + jnp.arange(ns, dtype=jnp.int32) % MOE_BLK
    tok = jnp.where(code >= 0, code // TOP_K, 0)
    dst = jnp.where(code >= 0, (code % TOP_K) * t + code // TOP_K, spare)
    dstp = jnp.concatenate([na + jnp.arange(MOE_BLK, dtype=jnp.int32), dst])
    y2 = _ffn(blk_exp, tok, dstp, h, wg_bf, wu_bf, wd_bf, layer, na + MOE_BLK)
    return _combine(y2, x, gates, modl, grp, 5, ln_g, ln_b, alpha, split)


def _rope_tables(tp, nb, l):
    rows = l // GRID_W
    row = np.repeat(np.arange(rows, dtype=np.float32), GRID_W)
    col = np.tile(np.arange(GRID_W, dtype=np.float32), rows)
    inv_freq = (np.float32(ROPE_THETA) ** (-np.arange(ROPE_PAIRS, dtype=np.float32) / np.float32(ROPE_PAIRS)))
    ar = row[:, None] * inv_freq
    ac = col[:, None] * inv_freq
    ang = np.concatenate([ar, ar, ac, ac], axis=-1).astype(np.float64)
    cos = np.cos(ang)
    sin = np.sin(ang)
    sign = np.tile(np.concatenate([-np.ones(ROPE_PAIRS), np.ones(ROPE_PAIRS)]), 2)
    cos_all = np.concatenate([np.ones((tp, HEAD_DIM)), np.tile(cos, (nb, 1))], axis=0)
    sin_all = np.concatenate([np.zeros((tp, HEAD_DIM)), np.tile(sin * sign, (nb, 1))], axis=0)
    return jnp.asarray(cos_all.astype(np.float32)), jnp.asarray(sin_all.astype(np.float32))


def kernel(x_prompt, x_sample, cache_k, cache_v, c, c_ctx, w_ada, b_ada, ln_g, ln_b, w_in_ab, q_norm_g, k_norm_g, hy_short_w, hy_short_b, hy_f_w1, hy_f_b1, hy_f_w2, hy_f_b2, hy_f_w3, hy_f_b3, hy_sin_freq, hy_skip, w_out_ab, cv_pw1, cv_dw_w, cv_dw_b, cv_ln_g, cv_ln_b, cv_pw2, moe_w_grp, moe_b_grp, moe_w_exp, moe_b_exp, moe_w_gate, moe_w_up, moe_w_down):
    bp, lp, d = x_prompt.shape
    bs, ls, _ = x_sample.shape
    depth = w_ada.shape[0]
    tp, ts = bp * lp, bs * ls
    t = tp + ts
    kv_w = cache_k.shape[2] * HEAD_DIM
    hy_w = hy_skip.shape[2]
    att_w = d - hy_w
    ng = moe_w_grp.shape[2]
    ne = moe_w_exp.shape[2]
    alpha = (2 * depth) ** 0.25
    nmod = 6
    ngrp = 1 + bs
    gpad = -(-ngrp // SUBLANES) * SUBLANES
    grp = _Groups(tp, ls, nmod)

    x = jnp.concatenate([x_prompt.reshape(tp, d), x_sample.reshape(ts, d)], axis=0)
    cond = jnp.zeros((gpad, d), F32).at[0].set(c_ctx).at[1:ngrp].set(c)
    mod = _ada(cond, w_ada, b_ada)
    nrt = LANES
    wg_bf, wu_bf, wd_bf = moe_w_gate.astype(BF16), moe_w_up.astype(BF16), moe_w_down.astype(BF16)
    new_k, new_v = None, None
    for layer in range(depth):
        i = layer // 2
        modl = mod[layer].reshape(gpad * nmod, 1, d)
        wr = jnp.zeros((d, nrt), F32).at[:, :ng].set(moe_w_grp[layer]).at[:, ng:ng + ne].set(moe_w_exp[layer])
        wr_hi = wr.astype(BF16)
        wr = jnp.concatenate([wr_hi, (wr - wr_hi.astype(F32)).astype(BF16)], axis=0)
        br = jnp.zeros((1, nrt), F32).at[0, :ng].set(moe_b_grp[layer]).at[0, ng:ng + ne].set(moe_b_exp[layer])
        if layer % 2 == 0:
            zp = _mm_mod(x, t, modl, grp, 1, 0, w_in_ab[i].astype(BF16))
            cos_t, sin_t = _rope_tables(tp, bs, ls)
            q_bf, k_f, k_bf = _qkprep(zp, cos_t, sin_t, q_norm_g[i], k_norm_g[i], att_w, kv_w)
            o_p = _attention(q_bf, k_bf, zp, 0, bp, lp, att_w, kv_w)
            o_s = _attention(q_bf, k_bf, zp, tp, bs, ls, att_w, kv_w, cache_k, cache_v, i)
            ucol = att_w + 2 * kv_w
            ys = []
            for row0, nb, l in ((0, bp, lp), (tp, bs, ls)):
                fc = _fft_consts(l)
                kf = _hyena_filters(l, fc, hy_f_w1[i], hy_f_b1[i], hy_f_w2[i], hy_f_b2[i], hy_f_w3[i], hy_f_b3[i],
                                    hy_sin_freq[i], hy_w)
                usc = _dwconv(zp, ucol, 3 * hy_w, hy_short_w[i], hy_short_b[i], row0, nb, l)
                z1 = _lconv(usc, 0, 0, usc, hy_w, 0, kf, 0, hy_skip[i, 0], fc, nb, l, hy_w)
                ys.append(_lconv(z1, 0, 0, usc, 2 * hy_w, 0, kf, 1, hy_skip[i, 1], fc, nb, l, hy_w))
            x, h, lg = _out_ln((o_p, o_s), tuple(ys), w_out_ab[i].astype(BF16), x, t, modl, grp,
                               ln_g[layer, 0], ln_b[layer, 0], wr, br, alpha)
            nkv = kv_w // HEAD_DIM
            kk = k_f[:tp].reshape(bp, lp, nkv, HEAD_DIM).transpose(0, 2, 1, 3)
            vv = zp[:tp, att_w + kv_w:att_w + 2 * kv_w].reshape(bp, lp, nkv, HEAD_DIM).transpose(0, 2, 1, 3)
            new_k = kk if new_k is None else jnp.concatenate([new_k, kk], axis=1)
            new_v = vv if new_v is None else jnp.concatenate([new_v, vv], axis=1)
        else:
            a = _mm_mod(x, t, modl, grp, 1, 0, cv_pw1[i].astype(BF16), glu=True)
            a = (_dwconv(a, 0, d, cv_dw_w[i], cv_dw_b[i], 0, bp, lp),
                 _dwconv(a, 0, d, cv_dw_w[i], cv_dw_b[i], tp, bs, ls))
            x, h, lg = _pw2_ln(a, cv_ln_g[i], cv_ln_b[i], cv_pw2[i].astype(BF16), x, t, modl, grp,
                               ln_g[layer, 0], ln_b[layer, 0], wr, br, alpha)
        x = _moe(h, lg, x, modl, grp, ln_g[layer, 1], ln_b[layer, 1], wg_bf, wu_bf, wd_bf, layer, ng, alpha,
                 layer == depth - 1)
    nkv = kv_w // HEAD_DIM
    n_attn = (depth + 1) // 2
    new_k = new_k.reshape(bp, n_attn, nkv, lp, HEAD_DIM)
    new_v = new_v.reshape(bp, n_attn, nkv, lp, HEAD_DIM)
    return (x[0].reshape(bp, lp, d), x[1].reshape(bs, ls, d), new_k, new_v)
```
